```python
import math
import jax
import jax.numpy as jnp
from jax import lax
import numpy as np

D_MODEL = 1024
BATCH = 8
SEQ = 4096
DEPTH = 4

GRID_W = 64
CTX_LEN = 256
ROPE_THETA = 10000.0
RMS_EPS = 1e-6
Q_BLOCK = 128
HEAD_DIM = 64
N_BRANCH = 4
BRANCH_HEADS = D_MODEL // (N_BRANCH * HEAD_DIM)
BRANCH_WIDTH = BRANCH_HEADS * HEAD_DIM
NA_WIN_H = 8
NA_WIN_W = 16
GQA_KV_HEADS = 2
DIFF_QK_DIM = 32
MLA_Q_RANK = D_MODEL // 4
MLA_KV_RANK = D_MODEL // 8
MLA_NOPE_DIM = 64
MLA_ROPE_DIM = 32
MLA_QK_DIM = MLA_NOPE_DIM + MLA_ROPE_DIM
D_FF = 128 * ((8 * D_MODEL // 3 + 127) // 128)
N_MOD = 9
NA_COLS = 3 * BRANCH_WIDTH
GQA_COLS = BRANCH_WIDTH + 2 * GQA_KV_HEADS * HEAD_DIM
DIFF_COLS = 2 * BRANCH_HEADS * 2 * DIFF_QK_DIM + BRANCH_WIDTH
MLA_COLS = MLA_Q_RANK + MLA_KV_RANK + MLA_ROPE_DIM
IN_COLS = NA_COLS + GQA_COLS + DIFF_COLS + MLA_COLS
IN_SPLITS = (NA_COLS, NA_COLS + GQA_COLS, NA_COLS + GQA_COLS + DIFF_COLS)

kernel_name = 'hybrid_gated_multimixer_diffusion_trunk'


def rms_norm(x, g):
    xf = x.astype(jnp.float32)
    y = xf * lax.rsqrt(jnp.mean(jnp.square(xf), axis=-1, keepdims=True) + RMS_EPS)
    return (y * g.astype(jnp.float32)).astype(x.dtype)


def modulate(x, g, shift, scale):
    return rms_norm(x, g) * (1 + scale[:, None, :]) + shift[:, None, :]


def ffn_half_step(x, g, shift, scale, gate, w1, w3, w2):
    h = modulate(x, g, shift, scale)
    return x + 0.5 * gate[:, None, :] * ((jax.nn.silu(h @ w1) * (h @ w3)) @ w2)


def axial_rope_tables(n_tok, rot_dim):
    t = jnp.arange(n_tok, dtype=jnp.int32)
    row = (t // GRID_W).astype(jnp.float32)
    col = (t % GRID_W).astype(jnp.float32)
    n_freq = rot_dim // 4
    inv = ROPE_THETA ** (-jnp.arange(n_freq, dtype=jnp.float32) / n_freq)
    ar = row[:, None] * inv[None, :]
    ac = col[:, None] * inv[None, :]
    ang = jnp.concatenate([ar, ar, ac, ac], axis=-1)
    return jnp.cos(ang), jnp.sin(ang)


def apply_axial_rope(x, cos, sin):
    xf = x.astype(jnp.float32)
    x1, x2, x3, x4 = jnp.split(xf, 4, axis=-1)
    rot = jnp.concatenate([-x2, x1, -x4, x3], axis=-1)
    shp = (x.shape[1],) + (1,) * (x.ndim - 3) + (x.shape[-1],)
    return (xf * cos.reshape(shp) + rot * sin.reshape(shp)).astype(x.dtype)


def ctx_attention(q, k, v, scale):
    s = jnp.einsum('bqgrd,bkgd->bgrqk', q, k, preferred_element_type=jnp.float32) * scale
    p = jax.nn.softmax(s, axis=-1).astype(v.dtype)
    return jnp.einsum('bgrqk,bkgd->bqgrd', p, v)


def joint_attention(q_pos, q_nopos, k_lat, v_lat, k_ctx, v_ctx, scale):
    B, S, G, R, dk = q_pos.shape
    nb = S // Q_BLOCK

    def to_blocks(q):
        return jnp.moveaxis(q.reshape(B, nb, Q_BLOCK, G, R, dk), 1, 0)

    def block(qs):
        qb, qcb = qs
        s_lat = jnp.einsum('bqgrd,bkgd->bgrqk', qb, k_lat, preferred_element_type=jnp.float32)
        s_ctx = jnp.einsum('bqgrd,bkgd->bgrqk', qcb, k_ctx, preferred_element_type=jnp.float32)
        p = jax.nn.softmax(jnp.concatenate([s_lat, s_ctx], axis=-1) * scale, axis=-1).astype(v_lat.dtype)
        return (jnp.einsum('bgrqk,bkgd->bqgrd', p[..., :S], v_lat)
                + jnp.einsum('bgrqk,bkgd->bqgrd', p[..., S:], v_ctx))

    o = lax.map(block, (to_blocks(q_pos), to_blocks(q_nopos)))
    return jnp.moveaxis(o, 0, 1).reshape(B, S, G, R, v_lat.shape[-1])


def neighbourhood_attention(q, k, v, k_ctx, v_ctx, rpb, scale):
    B, S, H, d = q.shape
    rows = S // GRID_W
    wh = min(NA_WIN_H, rows)
    r = jnp.arange(rows)
    r0 = jnp.clip(r - wh // 2, 0, rows - wh)
    key_rows = r0[:, None] + jnp.arange(wh)[None, :]
    col = jnp.arange(GRID_W)
    c0 = jnp.clip(col - NA_WIN_W // 2, 0, GRID_W - NA_WIN_W)
    in_win = (col[None, :] >= c0[:, None]) & (col[None, :] < c0[:, None] + NA_WIN_W)
    qg = q.reshape(B, rows, GRID_W, H, d)
    kg = k.reshape(B, rows, GRID_W, H, d)[:, key_rows]
    vg = v.reshape(B, rows, GRID_W, H, d)[:, key_rows]
    s_nb = jnp.einsum('brqhd,brakhd->bhrqak', qg, kg, preferred_element_type=jnp.float32) * scale
    d_row = key_rows - r[:, None] + (NA_WIN_H - 1)
    d_col = jnp.clip(col[None, :] - col[:, None], -(NA_WIN_W - 1), NA_WIN_W - 1) + (NA_WIN_W - 1)
    bias = rpb[:, d_row[:, None, :, None], d_col[None, :, None, :]]
    s_nb = jnp.where(in_win[:, None, :], s_nb + bias.astype(jnp.float32), -jnp.inf)
    s_nb = s_nb.reshape(B, H, rows, GRID_W, wh * GRID_W)
    s_ctx = jnp.einsum('brqhd,blhd->bhrql', qg, k_ctx, preferred_element_type=jnp.float32) * scale
    p = jax.nn.softmax(jnp.concatenate([s_nb, s_ctx], axis=-1), axis=-1).astype(v.dtype)
    p_nb = p[..., :wh * GRID_W].reshape(B, H, rows, GRID_W, wh, GRID_W)
    o = (jnp.einsum('bhrqak,brakhd->brqhd', p_nb, vg)
         + jnp.einsum('bhrql,blhd->brqhd', p[..., wh * GRID_W:], v_ctx))
    return o.reshape(B, S, H, d)


def mixer_na(p_lat, p_ctx, g_q, g_k, rpb, need_ctx):
    def project(p):
        b, n = p.shape[:2]
        p = p.reshape(b, n, 3, BRANCH_HEADS, HEAD_DIM)
        return rms_norm(p[:, :, 0], g_q), rms_norm(p[:, :, 1], g_k), p[:, :, 2]
    q, k, v = project(p_lat)
    qc, kc, vc = project(p_ctx)
    scale = HEAD_DIM ** -0.5
    o_lat = neighbourhood_attention(q, k, v, kc, vc, rpb, scale).reshape(p_lat.shape[0], p_lat.shape[1], BRANCH_WIDTH)
    o_ctx = None
    if need_ctx:
        o_ctx = ctx_attention(qc[:, :, :, None], kc, vc, scale).reshape(p_ctx.shape[0], p_ctx.shape[1], BRANCH_WIDTH)
    return o_lat, o_ctx


def mixer_gqa(p_lat, p_ctx, g_q, g_k, cos, sin, need_ctx):
    R = BRANCH_HEADS // GQA_KV_HEADS
    def project(p):
        b, n = p.shape[:2]
        q, k, v = jnp.split(p, [BRANCH_WIDTH, BRANCH_WIDTH + GQA_KV_HEADS * HEAD_DIM], axis=-1)
        q = rms_norm(q.reshape(b, n, GQA_KV_HEADS, R, HEAD_DIM), g_q)
        k = rms_norm(k.reshape(b, n, GQA_KV_HEADS, HEAD_DIM), g_k)
        return q, k, v.reshape(b, n, GQA_KV_HEADS, HEAD_DIM)
    q, k, v = project(p_lat)
    qc, kc, vc = project(p_ctx)
    scale = HEAD_DIM ** -0.5
    o = joint_attention(apply_axial_rope(q, cos, sin), q, apply_axial_rope(k, cos, sin), v, kc, vc, scale)
    o_lat = o.reshape(p_lat.shape[0], p_lat.shape[1], BRANCH_WIDTH)
    o_ctx = None
    if need_ctx:
        o_ctx = ctx_attention(qc, kc, vc, scale).reshape(p_ctx.shape[0], p_ctx.shape[1], BRANCH_WIDTH)
    return o_lat, o_ctx


def mixer_diff(p_lat, p_ctx, g_q, g_k, lam_vecs, g_sub, lam_init, cos, sin, need_ctx):
    H, dq = BRANCH_HEADS, DIFF_QK_DIM
    def project(p):
        b, n = p.shape[:2]
        q, k, v = jnp.split(p, [H * 2 * dq, 2 * H * 2 * dq], axis=-1)
        q = rms_norm(q.reshape(b, n, H, 2, dq), g_q)
        k = rms_norm(k.reshape(b, n, H, 2, dq), g_k)
        return q, k, v.reshape(b, n, H, HEAD_DIM)
    q, k, v = project(p_lat)
    qc, kc, vc = project(p_ctx)
    lv = lam_vecs.astype(jnp.float32)
    lam = (jnp.exp(jnp.sum(lv[0] * lv[1])) - jnp.exp(jnp.sum(lv[2] * lv[3])) + lam_init).astype(v.dtype)
    scale = dq ** -0.5

    def combine(o1, o2):
        b, n = o1.shape[:2]
        return (rms_norm(o1 - lam * o2, g_sub) * (1 - lam_init)).reshape(b, n, BRANCH_WIDTH)

    qr, kr = apply_axial_rope(q, cos, sin), apply_axial_rope(k, cos, sin)
    o1 = joint_attention(qr[:, :, :, 0:1], q[:, :, :, 0:1], kr[:, :, :, 0], v, kc[:, :, :, 0], vc, scale)
    o2 = joint_attention(qr[:, :, :, 1:2], q[:, :, :, 1:2], kr[:, :, :, 1], v, kc[:, :, :, 1], vc, scale)
    o_lat = combine(o1, o2)
    o_ctx = None
    if need_ctx:
        o_ctx = combine(ctx_attention(qc[:, :, :, 0:1], kc[:, :, :, 0], vc, scale),
                        ctx_attention(qc[:, :, :, 1:2], kc[:, :, :, 1], vc, scale))
    return o_lat, o_ctx


def mixer_mla(p_lat, p_ctx, g_cq, g_ckv, w_uq, w_ukv, g_q, g_k, cos, sin, need_ctx):
    H = BRANCH_HEADS
    def project(p):
        b, n = p.shape[:2]
        c_q, c_kv, k_rope = jnp.split(p, [MLA_Q_RANK, MLA_Q_RANK + MLA_KV_RANK], axis=-1)
        q = (rms_norm(c_q, g_cq) @ w_uq).reshape(b, n, H, MLA_QK_DIM)
        kv = (rms_norm(c_kv, g_ckv) @ w_ukv).reshape(b, n, H, MLA_NOPE_DIM + HEAD_DIM)
        k = jnp.concatenate([kv[..., :MLA_NOPE_DIM],
                             jnp.broadcast_to(k_rope[:, :, None, :], (b, n, H, MLA_ROPE_DIM))], axis=-1)
        return rms_norm(q, g_q)[:, :, :, None], rms_norm(k, g_k), kv[..., MLA_NOPE_DIM:]

    def rope_tail(t):
        return jnp.concatenate([t[..., :MLA_NOPE_DIM], apply_axial_rope(t[..., MLA_NOPE_DIM:], cos, sin)], axis=-1)

    q, k, v = project(p_lat)
    qc, kc, vc = project(p_ctx)
    scale = MLA_QK_DIM ** -0.5
    o = joint_attention(rope_tail(q), q, rope_tail(k), v, kc, vc, scale)
    o_lat = o.reshape(p_lat.shape[0], p_lat.shape[1], BRANCH_WIDTH)
    o_ctx = None
    if need_ctx:
        o_ctx = ctx_attention(qc, kc, vc, scale).reshape(p_ctx.shape[0], p_ctx.shape[1], BRANCH_WIDTH)
    return o_lat, o_ctx


def merge_branches(n, outs, w_gate, b_gate, w_branch, w_out):
    y = jax.nn.sigmoid(n @ w_gate[0] + b_gate[0]) * (outs[0] @ w_branch[0])
    for j in range(1, N_BRANCH):
        y = y + jax.nn.sigmoid(n @ w_gate[j] + b_gate[j]) * (outs[j] @ w_branch[j])
    return y @ w_out


def setup_inputs(seed: int = 0) -> dict:
    key = jax.random.key(seed)
    keys = iter(jax.random.split(key, 32))
    f32 = jnp.float32

    def normal(shape, std):
        return jax.random.normal(next(keys), shape, f32) * std

    def gain(shape):
        return 1.0 + normal(shape, 0.05)

    L, D = DEPTH, D_MODEL
    return {
        'x': normal((BATCH, SEQ, D), 1.0),
        'c': normal((BATCH, D), 1.0),
        'ctx': normal((BATCH, CTX_LEN, D), 1.0),
        'c_ctx': normal((D,), 1.0),
        'w_ada': normal((L, D, N_MOD * D), 0.5 * D ** -0.5),
        'b_ada': normal((L, N_MOD * D), 0.02),
        'norm_g': gain((L, 3, D)),
        'ffn_w1': normal((L, 2, D, D_FF), D ** -0.5),
        'ffn_w3': normal((L, 2, D, D_FF), D ** -0.5),
        'ffn_w2': normal((L, 2, D_FF, D), D_FF ** -0.5),
        'w_in': normal((L, D, IN_COLS), D ** -0.5),
        'na_g_q': gain((L, HEAD_DIM)),
        'na_g_k': gain((L, HEAD_DIM)),
        'na_rpb': normal((L, BRANCH_HEADS, 2 * NA_WIN_H - 1, 2 * NA_WIN_W - 1), 0.1),
        'gqa_g_q': gain((L, HEAD_DIM)),
        'gqa_g_k': gain((L, HEAD_DIM)),
        'diff_g_q': gain((L, DIFF_QK_DIM)),
        'diff_g_k': gain((L, DIFF_QK_DIM)),
        'diff_lambda': normal((L, 4, DIFF_QK_DIM), 0.1),
        'diff_g_sub': gain((L, HEAD_DIM)),
        'mla_g_cq': gain((L, MLA_Q_RANK)),
        'mla_g_ckv': gain((L, MLA_KV_RANK)),
        'mla_w_uq': normal((L, MLA_Q_RANK, BRANCH_HEADS * MLA_QK_DIM), MLA_Q_RANK ** -0.5),
        'mla_w_ukv': normal((L, MLA_KV_RANK, BRANCH_HEADS * (MLA_NOPE_DIM + HEAD_DIM)), MLA_KV_RANK ** -0.5),
        'mla_g_q': gain((L, MLA_QK_DIM)),
        'mla_g_k': gain((L, MLA_QK_DIM)),
        'w_gate': normal((L, N_BRANCH, D, D), D ** -0.5),
        'b_gate': normal((L, N_BRANCH, D), 0.02),
        'w_branch': normal((L, N_BRANCH, BRANCH_WIDTH, D), BRANCH_WIDTH ** -0.5),
        'w_out': normal((L, D, D), D ** -0.5),
    }


def reference(x, c, ctx, c_ctx, w_ada, b_ada, norm_g, ffn_w1, ffn_w3, ffn_w2, w_in,
              na_g_q, na_g_k, na_rpb, gqa_g_q, gqa_g_k, diff_g_q, diff_g_k, diff_lambda, diff_g_sub,
              mla_g_cq, mla_g_ckv, mla_w_uq, mla_w_ukv, mla_g_q, mla_g_k,
              w_gate, b_gate, w_branch, w_out):
    S = x.shape[1]
    cos64, sin64 = axial_rope_tables(S, HEAD_DIM)
    cos_d, sin_d = axial_rope_tables(S, DIFF_QK_DIM)
    cos_m, sin_m = axial_rope_tables(S, MLA_ROPE_DIM)
    silu_c = jax.nn.silu(c)
    silu_cc = jax.nn.silu(c_ctx)[None, :]
    xl, xc = x, ctx
    for i in range(DEPTH):
        need_ctx = i < DEPTH - 1
        mod_l = jnp.split(silu_c @ w_ada[i] + b_ada[i], N_MOD, axis=-1)
        mod_c = jnp.split(silu_cc @ w_ada[i] + b_ada[i], N_MOD, axis=-1)
        xl = ffn_half_step(xl, norm_g[i, 0], mod_l[0], mod_l[1], mod_l[2], ffn_w1[i, 0], ffn_w3[i, 0], ffn_w2[i, 0])
        xc = ffn_half_step(xc, norm_g[i, 0], mod_c[0], mod_c[1], mod_c[2], ffn_w1[i, 0], ffn_w3[i, 0], ffn_w2[i, 0])
        nl = modulate(xl, norm_g[i, 1], mod_l[3], mod_l[4])
        nc = modulate(xc, norm_g[i, 1], mod_c[3], mod_c[4])
        pl = jnp.split(nl @ w_in[i], IN_SPLITS, axis=-1)
        pc = jnp.split(nc @ w_in[i], IN_SPLITS, axis=-1)
        lam_init = 0.8 - 0.6 * math.exp(-0.3 * i)
        outs = [
            mixer_na(pl[0], pc[0], na_g_q[i], na_g_k[i], na_rpb[i], need_ctx),
            mixer_gqa(pl[1], pc[1], gqa_g_q[i], gqa_g_k[i], cos64, sin64, need_ctx),
            mixer_diff(pl[2], pc[2], diff_g_q[i], diff_g_k[i], diff_lambda[i], diff_g_sub[i], lam_init,
                       cos_d, sin_d, need_ctx),
            mixer_mla(pl[3], pc[3], mla_g_cq[i], mla_g_ckv[i], mla_w_uq[i], mla_w_ukv[i], mla_g_q[i], mla_g_k[i],
                      cos_m, sin_m, need_ctx),
        ]
        xl = xl + mod_l[5][:, None, :] * merge_branches(nl, [o[0] for o in outs], w_gate[i], b_gate[i], w_branch[i], w_out[i])
        if need_ctx:
            xc = xc + mod_c[5][:, None, :] * merge_branches(nc, [o[1] for o in outs], w_gate[i], b_gate[i], w_branch[i], w_out[i])
            xc = ffn_half_step(xc, norm_g[i, 2], mod_c[6], mod_c[7], mod_c[8], ffn_w1[i, 1], ffn_w3[i, 1], ffn_w2[i, 1])
        xl = ffn_half_step(xl, norm_g[i, 2], mod_l[6], mod_l[7], mod_l[8], ffn_w1[i, 1], ffn_w3[i, 1], ffn_w2[i, 1])
    return xl
```

```python
import functools
import math

import jax
import jax.numpy as jnp
import numpy as np
from jax import lax
from jax.experimental import pallas as pl
from jax.experimental.pallas import tpu as pltpu

F32 = jnp.float32
BF16 = jnp.bfloat16

D_MODEL = 1024
GRID_W = 64
ROPE_THETA = 10000.0
RMS_EPS = 1e-6
HEAD_DIM = 64
N_BRANCH = 4
BRANCH_HEADS = 4
BRANCH_WIDTH = 256
NA_WIN_H = 8
NA_WIN_W = 16
DIFF_QK_DIM = 32
MLA_Q_RANK = 256
MLA_KV_RANK = 128
MLA_NOPE_DIM = 64
MLA_ROPE_DIM = 32
MLA_QK_DIM = 96
D_FF = 2816
N_MOD = 9
LOG2E = 1.4426950408889634

LANES = 128
TOKEN_TILE = 512
Q_TILE = 256
KEY_CHUNK = 512
FF_TILE = 1408
NA_KEY_ROWS = 12
NEG_BIG = -1e30
VMEM_LIMIT = 56 * 1024 * 1024

C_NA_Q, C_NA_K, C_NA_V = 0, 256, 512
C_GQ_Q, C_GQ_K, C_GQ_V = 768, 1024, 1152
C_DF_Q, C_DF_K, C_DF_V = 1280, 1536, 1792
C_ML_CQ, C_ML_CKV, C_ML_ROPE = 2048, 2304, 2432
IN_COLS_PAD = 2560


def _cparams(sem):
    return pltpu.CompilerParams(dimension_semantics=sem, vmem_limit_bytes=VMEM_LIMIT)


def _dot(a, b):
    return jnp.dot(a, b, preferred_element_type=F32)


def _dot_nt(a, b):
    return lax.dot_general(a, b, (((1,), (1,)), ((), ())), preferred_element_type=F32)


def _silu(a):
    return a * jax.nn.sigmoid(a)


def _modulated_norm(x, g, shift, scale):
    ms = jnp.mean(x * x, axis=-1, keepdims=True)
    return (x * lax.rsqrt(ms + RMS_EPS) * g) * (1.0 + scale) + shift


def _mod_kernel(c_ref, w_ref, b_ref, o_ref):
    c = c_ref[...]
    o_ref[...] = _dot(_silu(c).astype(BF16), w_ref[...].astype(BF16)) + b_ref[...]


def _mod_call(c16, w_ada, b_ada):
    depth, d, nd = w_ada.shape
    tn = 1024
    return pl.pallas_call(
        _mod_kernel,
        grid=(depth, nd // tn),
        in_specs=[
            pl.BlockSpec((16, d), lambda l, j: (0, 0)),
            pl.BlockSpec((None, d, tn), lambda l, j: (l, 0, j)),
            pl.BlockSpec((None, 1, tn), lambda l, j: (l, 0, j)),
        ],
        out_specs=pl.BlockSpec((None, 16, tn), lambda l, j: (l, 0, j)),
        out_shape=jax.ShapeDtypeStruct((depth, 16, nd), F32),
        compiler_params=_cparams(("arbitrary", "arbitrary")),
        name="adaln_mod",
    )(c16, w_ada, b_ada.reshape(depth, 1, nd))


def _ffn_kernel(x_ref, mod_ref, g_ref, w1_ref, w3_ref, w2_ref, o_ref, h_scr, acc_scr, *, row0, nf):
    j = pl.program_id(1)

    @pl.when(j == 0)
    def _():
        h = _modulated_norm(x_ref[...], g_ref[...], mod_ref[row0:row0 + 1, :], mod_ref[row0 + 1:row0 + 2, :])
        h_scr[...] = h.astype(BF16)

    h = h_scr[...]
    a = _dot(h, w1_ref[...])
    b = _dot(h, w3_ref[...])
    contrib = _dot((_silu(a) * b).astype(BF16), w2_ref[...])

    @pl.when(j == 0)
    def _():
        acc_scr[...] = contrib

    @pl.when(j > 0)
    def _():
        acc_scr[...] += contrib

    @pl.when(j == nf - 1)
    def _():
        o_ref[...] = x_ref[...] + (0.5 * mod_ref[row0 + 2:row0 + 3, :]) * acc_scr[...]


def _ffn_call(x, mod, g, w1, w3, w2, layer, half, row0, n_tiles, mod_idx):
    d = x.shape[1]
    tm, tf = TOKEN_TILE, FF_TILE
    nf = D_FF // tf
    return pl.pallas_call(
        functools.partial(_ffn_kernel, row0=row0, nf=nf),
        grid=(n_tiles, nf),
        in_specs=[
            pl.BlockSpec((tm, d), lambda i, j: (i, 0)),
            pl.BlockSpec((None, N_MOD, d), lambda i, j: (mod_idx(i), 0, 0)),
            pl.BlockSpec((1, d), lambda i, j: (0, 0)),
            pl.BlockSpec((None, None, d, tf), lambda i, j: (layer, half, 0, j)),
            pl.BlockSpec((None, None, d, tf), lambda i, j: (layer, half, 0, j)),
            pl.BlockSpec((None, None, tf, d), lambda i, j: (layer, half, j, 0)),
        ],
        out_specs=pl.BlockSpec((tm, d), lambda i, j: (i, 0)),
        out_shape=jax.ShapeDtypeStruct((n_tiles * tm, d), F32),
        scratch_shapes=[pltpu.VMEM((tm, d), BF16), pltpu.VMEM((tm, d), F32)],
        compiler_params=_cparams(("arbitrary", "arbitrary")),
        name="ffn_half_step",
    )(x, mod, g, w1, w3, w2)


def _segnorm(x, bd, seg_len, gain):
    x2 = x * x
    hi = x2.astype(BF16)
    lo = (x2 - hi.astype(F32)).astype(BF16)
    ss = _dot(hi, bd) + _dot(lo, bd)
    return x * lax.rsqrt(ss * (1.0 / seg_len) + RMS_EPS) * gain


def _tile_lanes(t, width):
    reps = width // t.shape[-1]
    return t if reps == 1 else jnp.concatenate([t] * reps, axis=-1)


def _rope(x, cos, sin_signed, quarter):
    width = x.shape[-1]
    lane = lax.broadcasted_iota(jnp.int32, x.shape, 1)
    first = (lane % (2 * quarter)) < quarter
    rot = jnp.where(first, pltpu.roll(x, width - quarter, 1), pltpu.roll(x, quarter, 1))
    return x * _tile_lanes(cos, width) + rot * _tile_lanes(sin_signed, width)


def _proj_kernel(x_ref, mod_ref, g_ref, w_ref, vec_ref, bd64_ref, bd32_ref, bd128_ref,
                 wuq_ref, wuk_ref, wuv_ref, c64_ref, s64_ref, c32_ref, s32_ref, cm_ref, sm_ref,
                 n_ref, naq_ref, nak_ref, nav_ref,
                 gqr_ref, gqn_ref, gk_ref, gv_ref,
                 dqr_ref, dqn_ref, dk_ref, dv_ref,
                 mqr_ref, mqn_ref, mk_ref, mv_ref):
    n = _modulated_norm(x_ref[...], g_ref[...], mod_ref[3:4, :], mod_ref[4:5, :]).astype(BF16)
    n_ref[...] = n
    p = _dot(n, w_ref[...])
    bd64, bd32, bd128 = bd64_ref[...], bd32_ref[...], bd128_ref[...]
    c64, s64 = c64_ref[...], s64_ref[...]
    c32, s32 = c32_ref[...], s32_ref[...]
    cm, sm = cm_ref[...], sm_ref[...]

    def vec(row, width):
        return vec_ref[row:row + 1, 0:width]

    naq_ref[...] = _segnorm(p[:, C_NA_Q:C_NA_Q + 256], bd64, HEAD_DIM, vec(0, 256)).astype(BF16)
    nak_ref[...] = _segnorm(p[:, C_NA_K:C_NA_K + 256], bd64, HEAD_DIM, vec(1, 256)).astype(BF16)
    nav_ref[...] = p[:, C_NA_V:C_NA_V + 256].astype(BF16)

    q = _segnorm(p[:, C_GQ_Q:C_GQ_Q + 256], bd64, HEAD_DIM, vec(2, 256))
    gqn_ref[...] = q.astype(BF16)
    gqr_ref[...] = _rope(q, c64, s64, HEAD_DIM // 4).astype(BF16)
    k = _segnorm(p[:, C_GQ_K:C_GQ_K + 128], bd64[0:128, 0:128], HEAD_DIM, vec(3, 128))
    gk_ref[...] = _rope(k, c64, s64, HEAD_DIM // 4).astype(BF16)
    gv_ref[...] = p[:, C_GQ_V:C_GQ_V + 128].astype(BF16)

    q = _segnorm(p[:, C_DF_Q:C_DF_Q + 256], bd32, DIFF_QK_DIM, vec(4, 256))
    dqn_ref[...] = q.astype(BF16)
    dqr_ref[...] = _rope(q, c32, s32, DIFF_QK_DIM // 4).astype(BF16)
    k = _segnorm(p[:, C_DF_K:C_DF_K + 256], bd32, DIFF_QK_DIM, vec(5, 256))
    dk_ref[...] = _rope(k, c32, s32, DIFF_QK_DIM // 4).astype(BF16)
    dv_ref[...] = p[:, C_DF_V:C_DF_V + 256].astype(BF16)

    cq = p[:, C_ML_CQ:C_ML_CQ + 256]
    cq = cq * lax.rsqrt(jnp.mean(cq * cq, axis=-1, keepdims=True) + RMS_EPS) * vec(6, 256)
    q = _segnorm(_dot(cq.astype(BF16), wuq_ref[...]), bd128, MLA_QK_DIM, vec(8, 512))
    mqn_ref[...] = q.astype(BF16)
    mqr_ref[...] = _rope(q, cm, sm, MLA_ROPE_DIM // 4).astype(BF16)
    ckv = p[:, C_ML_CKV:C_ML_CKV + 128]
    ckv = (ckv * lax.rsqrt(jnp.mean(ckv * ckv, axis=-1, keepdims=True) + RMS_EPS) * vec(7, 128)).astype(BF16)
    k_rope = pltpu.roll(p[:, C_ML_ROPE:C_ML_ROPE + 128], MLA_NOPE_DIM, 1)
    k = _dot(ckv, wuk_ref[...]) + _tile_lanes(k_rope, 512)
    k = _segnorm(k, bd128, MLA_QK_DIM, vec(9, 512))
    mk_ref[...] = _rope(k, cm, sm, MLA_ROPE_DIM // 4).astype(BF16)
    mv_ref[...] = _dot(ckv, wuv_ref[...]).astype(BF16)


_PROJ_OUT_WIDTHS = (1024, 256, 256, 256, 256, 256, 128, 128, 256, 256, 256, 256, 512, 512, 512, 256)


def _proj_call(x, mod, g, w_in, vecs, bds, wuq, wuk, wuv, tables, layer, n_tiles, mod_idx, tab_idx):
    d = x.shape[1]
    tm = TOKEN_TILE
    const2 = lambda i: (0, 0)
    in_specs = [
        pl.BlockSpec((tm, d), lambda i: (i, 0)),
        pl.BlockSpec((None, N_MOD, d), lambda i: (mod_idx(i), 0, 0)),
        pl.BlockSpec((1, d), const2),
        pl.BlockSpec((None, d, IN_COLS_PAD), lambda i: (layer, 0, 0)),
        pl.BlockSpec((None, 16, 512), lambda i: (layer, 0, 0)),
        pl.BlockSpec((256, 256), const2),
        pl.BlockSpec((256, 256), const2),
        pl.BlockSpec((512, 512), const2),
        pl.BlockSpec((None, 256, 512), lambda i: (layer, 0, 0)),
        pl.BlockSpec((None, 128, 512), lambda i: (layer, 0, 0)),
        pl.BlockSpec((None, 128, 256), lambda i: (layer, 0, 0)),
    ] + [pl.BlockSpec((tm, LANES), lambda i: (tab_idx(i), 0))] * 6
    rows = n_tiles * tm
    return pl.pallas_call(
        _proj_kernel,
        grid=(n_tiles,),
        in_specs=in_specs,
        out_specs=[pl.BlockSpec((tm, w), lambda i: (i, 0)) for w in _PROJ_OUT_WIDTHS],
        out_shape=[jax.ShapeDtypeStruct((rows, w), BF16) for w in _PROJ_OUT_WIDTHS],
        compiler_params=_cparams(("arbitrary",)),
        name="mixer_projection",
    )(x, mod, g, w_in, vecs, *bds, wuq, wuk, wuv, *tables)


def _softmax_step(state, q, k, v, bias=None):
    m, l, acc = state
    s = _dot_nt(q, k)
    if bias is not None:
        s = s + bias
    m_new = jnp.maximum(m, jnp.max(s, axis=-1, keepdims=True))
    alpha = jnp.exp2(m - m_new)
    p = jnp.exp2(s - m_new)
    l = alpha * l + jnp.sum(p, axis=-1, keepdims=True)
    acc = alpha * acc + _dot(p.astype(BF16), v)
    return m_new, l, acc


def _init_state(tq):
    return (jnp.full((tq, 1), NEG_BIG, F32), jnp.zeros((tq, 1), F32), jnp.zeros((tq, LANES), F32))


def _masked_q(q_ref, slab, lo, hi):
    q = q_ref[:, slab * LANES:(slab + 1) * LANES]
    if lo == 0 and hi == LANES:
        return q
    lane = lax.broadcasted_iota(jnp.int32, q.shape, 1)
    return jnp.where((lane >= lo) & (lane < hi), q, jnp.zeros_like(q))


def _joint_kernel(*refs, subs, n_lat_tiles, seq, diff_lam_init):
    if diff_lam_init is None:
        qr_ref, qn_ref, kl_ref, kc_ref, vl_ref, vc_ref, o_ref = refs
    else:
        qr_ref, qn_ref, kl_ref, kc_ref, vl_ref, vc_ref, lam_ref, gsub_ref, o_ref = refs
    t = pl.program_id(2)
    tq = o_ref.shape[0]
    init = tuple(_init_state(tq) for _ in subs)

    def latent_part(states):
        qs = [_masked_q(qr_ref, slab, lo, hi) for (slab, lo, hi, _) in subs]

        def body(c, st):
            off = pl.multiple_of(c * KEY_CHUNK, KEY_CHUNK)
            v = vl_ref[pl.ds(off, KEY_CHUNK), :]
            out = []
            for i, (slab, _, _, _) in enumerate(subs):
                k = kl_ref[pl.ds(off, KEY_CHUNK), slab * LANES:(slab + 1) * LANES]
                out.append(_softmax_step(st[i], qs[i], k, v))
            return tuple(out)

        return lax.fori_loop(0, seq // KEY_CHUNK, body, states)

    states = lax.cond(t < n_lat_tiles, latent_part, lambda st: st, init)
    vc = vc_ref[...]
    outs = []
    for i, (slab, lo, hi, _) in enumerate(subs):
        kc = kc_ref[:, slab * LANES:(slab + 1) * LANES]
        _, l, acc = _softmax_step(states[i], _masked_q(qn_ref, slab, lo, hi), kc, vc)
        outs.append(acc / l)

    lane = lax.broadcasted_iota(jnp.int32, (tq, LANES), 1)
    low = lane < HEAD_DIM
    if diff_lam_init is None:
        by_half = {vh: o for (_, _, _, vh), o in zip(subs, outs)}
        o_ref[...] = jnp.where(low, by_half[0], by_half[1]).astype(o_ref.dtype)
    else:
        lv = lam_ref[...]
        lam = (jnp.exp(jnp.sum(lv[0:1, :] * lv[1:2, :], axis=-1, keepdims=True))
               - jnp.exp(jnp.sum(lv[2:3, :] * lv[3:4, :], axis=-1, keepdims=True)) + diff_lam_init)
        d = jnp.where(low, outs[0] - lam * outs[1], outs[2] - lam * outs[3])
        d2 = d * d
        ss_lo = jnp.sum(jnp.where(low, d2, 0.0), axis=-1, keepdims=True)
        ss_hi = jnp.sum(jnp.where(low, 0.0, d2), axis=-1, keepdims=True)
        ms = jnp.where(low, ss_lo, ss_hi) * (1.0 / HEAD_DIM)
        o_ref[...] = ((d * lax.rsqrt(ms + RMS_EPS) * gsub_ref[...]) * (1.0 - diff_lam_init)).astype(o_ref.dtype)


def _joint_call(qr, qn, k, v, out_rows, *, subs, q_width, k_width, k_blocks, batch, seq, ctx_len,
                with_ctx_queries, name, lam=None, gsub=None, diff_lam_init=None):
    tq = Q_TILE
    n_lat = seq // tq
    nqt = n_lat + (1 if with_ctx_queries else 0)
    ctx_row0 = batch * seq // ctx_len

    def qrow(b, t):
        return jnp.where(t < n_lat, b * n_lat + t, batch * n_lat + b)

    kcol = (lambda blk: blk) if k_blocks > 1 else (lambda blk: 0)
    in_specs = [
        pl.BlockSpec((tq, q_width), lambda b, blk, t: (qrow(b, t), blk)),
        pl.BlockSpec((tq, q_width), lambda b, blk, t: (qrow(b, t), blk)),
        pl.BlockSpec((seq, k_width), lambda b, blk, t: (b, kcol(blk))),
        pl.BlockSpec((ctx_len, k_width), lambda b, blk, t: (ctx_row0 + b, kcol(blk))),
        pl.BlockSpec((seq, LANES), lambda b, blk, t: (b, kcol(blk))),
        pl.BlockSpec((ctx_len, LANES), lambda b, blk, t: (ctx_row0 + b, kcol(blk))),
    ]
    args = [qr, qn, k, k, v, v]
    if diff_lam_init is not None:
        in_specs += [pl.BlockSpec((4, DIFF_QK_DIM), lambda b, blk, t: (0, 0)),
                     pl.BlockSpec((1, LANES), lambda b, blk, t: (0, 0))]
        args += [lam, gsub]
    return pl.pallas_call(
        functools.partial(_joint_kernel, subs=subs, n_lat_tiles=n_lat, seq=seq, diff_lam_init=diff_lam_init),
        grid=(batch, 2, nqt),
        in_specs=in_specs,
        out_specs=pl.BlockSpec((tq, LANES), lambda b, blk, t: (qrow(b, t), blk)),
        out_shape=jax.ShapeDtypeStruct((out_rows, BRANCH_WIDTH), BF16),
        compiler_params=_cparams(("arbitrary", "arbitrary", "arbitrary")),
        name=name,
    )(*args)


def _na_kernel(q_ref, kl_ref, kc_ref, vl_ref, vc_ref, ab_ref, o_ref, *, n_lat_tiles, grid_rows):
    t = pl.program_id(2)
    tq = o_ref.shape[0]
    subs = ((0, HEAD_DIM), (HEAD_DIM, LANES))
    init = tuple(_init_state(tq) for _ in subs)
    qs = [_masked_q(q_ref, 0, lo, hi) for lo, hi in subs]
    n_keys = NA_KEY_ROWS * GRID_W

    def latent_part(states):
        band0 = jnp.clip(t * (tq // GRID_W) - NA_WIN_H // 2, 0, grid_rows - NA_KEY_ROWS)
        off = pl.multiple_of(band0 * GRID_W, GRID_W)
        k = kl_ref[pl.ds(off, n_keys), :]
        v = vl_ref[pl.ds(off, n_keys), :]
        return tuple(_softmax_step(states[i], qs[i], k, v, bias=ab_ref[i]) for i in range(2))

    states = lax.cond(t < n_lat_tiles, latent_part, lambda st: st, init)
    kc, vc = kc_ref[...], vc_ref[...]
    outs = []
    for i in range(2):
        _, l, acc = _softmax_step(states[i], qs[i], kc, vc)
        outs.append(acc / l)
    lane = lax.broadcasted_iota(jnp.int32, (tq, LANES), 1)
    o_ref[...] = jnp.where(lane < HEAD_DIM, outs[0], outs[1]).astype(o_ref.dtype)


def _na_call(q, k, v, ab, out_rows, *, batch, seq, ctx_len, with_ctx_queries):
    tq = Q_TILE
    n_lat = seq // tq
    nqt = n_lat + (1 if with_ctx_queries else 0)
    ctx_row0 = batch * seq // ctx_len
    n_keys = NA_KEY_ROWS * GRID_W

    def qrow(b, t):
        return jnp.where(t < n_lat, b * n_lat + t, batch * n_lat + b)

    def pattern(t):
        return jnp.where(t == 0, 0, jnp.where(t >= n_lat - 1, 2, 1))

    return pl.pallas_call(
        functools.partial(_na_kernel, n_lat_tiles=n_lat, grid_rows=seq // GRID_W),
        grid=(batch, 2, nqt),
        in_specs=[
            pl.BlockSpec((tq, LANES), lambda b, blk, t: (qrow(b, t), blk)),
            pl.BlockSpec((seq, LANES), lambda b, blk, t: (b, blk)),
            pl.BlockSpec((ctx_len, LANES), lambda b, blk, t: (ctx_row0 + b, blk)),
            pl.BlockSpec((seq, LANES), lambda b, blk, t: (b, blk)),
            pl.BlockSpec((ctx_len, LANES), lambda b, blk, t: (ctx_row0 + b, blk)),
            pl.BlockSpec((None, None, 2, tq, n_keys), lambda b, blk, t: (pattern(t), blk, 0, 0, 0)),
        ],
        out_specs=pl.BlockSpec((tq, LANES), lambda b, blk, t: (qrow(b, t), blk)),
        out_shape=jax.ShapeDtypeStruct((out_rows, BRANCH_WIDTH), BF16),
        compiler_params=_cparams(("arbitrary", "arbitrary", "arbitrary")),
        name="neighbourhood_attention",
    )(q, k, k, v, v, ab)


def _merge_kernel(x_ref, mod_ref, n_ref, o0_ref, o1_ref, o2_ref, o3_ref, wg_ref, bg_ref, wb_ref, wo_ref,
                  out_ref, y_scr):
    j = pl.program_id(1)
    gate = jax.nn.sigmoid(_dot(n_ref[...], wg_ref[...]) + bg_ref[...])
    for jj, o_ref in enumerate((o0_ref, o1_ref, o2_ref, o3_ref)):
        @pl.when(j == jj)
        def _(o_ref=o_ref, jj=jj):
            contrib = gate * _dot(o_ref[...], wb_ref[...])
            if jj == 0:
                y_scr[...] = contrib
            else:
                y_scr[...] += contrib

    @pl.when(j == N_BRANCH - 1)
    def _():
        out_ref[...] = x_ref[...] + mod_ref[5:6, :] * _dot(y_scr[...].astype(BF16), wo_ref[...])


def _merge_call(x, mod, n, outs, w_gate, b_gate, w_branch, w_out, layer, n_tiles, mod_idx):
    d = x.shape[1]
    tm = TOKEN_TILE
    row = lambda i, j: (i, 0)
    return pl.pallas_call(
        _merge_kernel,
        grid=(n_tiles, N_BRANCH),
        in_specs=[
            pl.BlockSpec((tm, d), row),
            pl.BlockSpec((None, N_MOD, d), lambda i, j: (mod_idx(i), 0, 0)),
            pl.BlockSpec((tm, d), row),
        ] + [pl.BlockSpec((tm, BRANCH_WIDTH), row)] * 4 + [
            pl.BlockSpec((None, None, d, d), lambda i, j: (layer, j, 0, 0)),
            pl.BlockSpec((None, None, 1, d), lambda i, j: (layer, j, 0, 0)),
            pl.BlockSpec((None, None, BRANCH_WIDTH, d), lambda i, j: (layer, j, 0, 0)),
            pl.BlockSpec((None, d, d), lambda i, j: (layer, 0, 0)),
        ],
        out_specs=pl.BlockSpec((tm, d), row),
        out_shape=jax.ShapeDtypeStruct((n_tiles * tm, d), F32),
        scratch_shapes=[pltpu.VMEM((tm, d), F32)],
        compiler_params=_cparams(("arbitrary", "arbitrary")),
        name="branch_merge",
    )(x, mod, n, *outs, w_gate, b_gate, w_branch, w_out)


def _rope_tables(seq, rot_dim, pad_rows):
    t = jnp.arange(seq, dtype=jnp.int32)
    row = (t // GRID_W).astype(F32)
    col = (t % GRID_W).astype(F32)
    n_freq = rot_dim // 4
    inv = ROPE_THETA ** (-jnp.arange(n_freq, dtype=F32) / n_freq)
    ar = row[:, None] * inv[None, :]
    ac = col[:, None] * inv[None, :]
    ang = jnp.concatenate([ar, ar, ac, ac], axis=-1)
    sign = jnp.tile(jnp.concatenate([-jnp.ones((n_freq,), F32), jnp.ones((n_freq,), F32)]), 2)
    cos = jnp.concatenate([jnp.cos(ang), jnp.ones((pad_rows, rot_dim), F32)], axis=0)
    sin = jnp.concatenate([jnp.sin(ang) * sign[None, :], jnp.zeros((pad_rows, rot_dim), F32)], axis=0)
    return cos, sin


def _all_rope_tables(seq, pad_rows):
    c64, s64 = _rope_tables(seq, HEAD_DIM, pad_rows)
    c32, s32 = _rope_tables(seq, DIFF_QK_DIM, pad_rows)
    rows = seq + pad_rows
    ones = lambda w: jnp.ones((rows, w), F32)
    zeros = lambda w: jnp.zeros((rows, w), F32)
    pad = LANES - MLA_QK_DIM
    cm = jnp.concatenate([ones(MLA_NOPE_DIM), c32, ones(pad)], axis=-1)
    sm = jnp.concatenate([zeros(MLA_NOPE_DIM), s32, zeros(pad)], axis=-1)
    return (jnp.tile(c64, (1, 2)), jnp.tile(s64, (1, 2)), jnp.tile(c32, (1, 4)), jnp.tile(s32, (1, 4)), cm, sm)


def _block_diag_ones(width, seg):
    idx = np.arange(width) // seg
    return jnp.asarray(idx[:, None] == idx[None, :], dtype=BF16)


def _na_bias_tables(rpb, grid_rows):
    rows_per_tile = Q_TILE // GRID_W
    wh = min(NA_WIN_H, grid_rows)
    kinds = ((0, 0), (rows_per_tile, 0), (grid_rows - rows_per_tile, grid_rows - NA_KEY_ROWS))
    qi = np.arange(Q_TILE)
    ki = np.arange(NA_KEY_ROWS * GRID_W)
    tabs = []
    for q_row0, k_row0 in kinds:
        r = q_row0 + qi // GRID_W
        c = qi % GRID_W
        kr = k_row0 + ki // GRID_W
        kc = ki % GRID_W
        r0 = np.clip(r - wh // 2, 0, grid_rows - wh)
        c0 = np.clip(c - NA_WIN_W // 2, 0, GRID_W - NA_WIN_W)
        valid = ((kr[None, :] >= r0[:, None]) & (kr[None, :] < r0[:, None] + wh)
                 & (kc[None, :] >= c0[:, None]) & (kc[None, :] < c0[:, None] + NA_WIN_W))
        d_row = np.clip(kr[None, :] - r[:, None] + (NA_WIN_H - 1), 0, 2 * NA_WIN_H - 2)
        d_col = np.clip(kc[None, :] - c[:, None], -(NA_WIN_W - 1), NA_WIN_W - 1) + (NA_WIN_W - 1)
        bias = rpb[:, :, d_row, d_col].astype(F32) * LOG2E
        tabs.append(jnp.where(jnp.asarray(valid)[None, None], bias, NEG_BIG))
    ab = jnp.stack(tabs, axis=1)
    depth = rpb.shape[0]
    return ab.reshape(depth, 3, 2, 2, Q_TILE, NA_KEY_ROWS * GRID_W)


def _pad_heads(w, head_w, to_w):
    lead = w.shape[:-1]
    w = w.reshape(lead + (BRANCH_HEADS, head_w))
    w = jnp.pad(w, [(0, 0)] * len(lead) + [(0, 0), (0, to_w - head_w)])
    return w.reshape(lead + (BRANCH_HEADS * to_w,))


def _gqa_head_order(a, axis):
    parts = jnp.split(a, BRANCH_HEADS, axis=axis)
    return jnp.concatenate([parts[0], parts[2], parts[1], parts[3]], axis=axis)


def kernel(x, c, ctx, c_ctx, w_ada, b_ada, norm_g, ffn_w1, ffn_w3, ffn_w2, w_in, na_g_q, na_g_k, na_rpb, gqa_g_q, gqa_g_k, diff_g_q, diff_g_k, diff_lambda, diff_g_sub, mla_g_cq, mla_g_ckv, mla_w_uq, mla_w_ukv, mla_g_q, mla_g_k, w_gate, b_gate, w_branch, w_out):
    batch, seq, d = x.shape
    ctx_len = ctx.shape[1]
    depth = w_ada.shape[0]
    tm = TOKEN_TILE
    assert d == D_MODEL and ctx_len == Q_TILE and seq % tm == 0 and (batch * ctx_len) % tm == 0
    assert seq % KEY_CHUNK == 0 and seq // GRID_W >= NA_KEY_ROWS and batch < 16
    n_lat_rows = batch * seq
    lat_tiles = n_lat_rows // tm
    all_tiles = lat_tiles + batch * ctx_len // tm
    tiles_per_batch = seq // tm

    def mod_idx(i):
        return jnp.where(i < lat_tiles, i // tiles_per_batch, batch)

    def tab_idx(i):
        return jnp.where(i < lat_tiles, i % tiles_per_batch, tiles_per_batch)

    c16 = jnp.concatenate([c, c_ctx[None, :], jnp.zeros((16 - batch - 1, d), F32)], axis=0)
    mods = _mod_call(c16, w_ada, b_ada).reshape(depth, 16, N_MOD, d)
    w1, w3, w2 = ffn_w1.astype(BF16), ffn_w3.astype(BF16), ffn_w2.astype(BF16)
    na_w, gq_w, df_w, ml_w = jnp.split(w_in, [768, 1280, 2048], axis=-1)
    w_in_p = jnp.concatenate([
        na_w, _gqa_head_order(gq_w[..., :256], -1), gq_w[..., 256:], df_w, ml_w,
        jnp.zeros((depth, d, IN_COLS_PAD - w_in.shape[-1]), F32)], axis=-1).astype(BF16)
    wuq = _pad_heads(mla_w_uq, MLA_QK_DIM, LANES).astype(BF16)
    wukv = mla_w_ukv.reshape(depth, MLA_KV_RANK, BRANCH_HEADS, MLA_NOPE_DIM + HEAD_DIM)
    wuk = _pad_heads(wukv[..., :MLA_NOPE_DIM].reshape(depth, MLA_KV_RANK, -1), MLA_NOPE_DIM, LANES).astype(BF16)
    wuv = wukv[..., MLA_NOPE_DIM:].reshape(depth, MLA_KV_RANK, BRANCH_WIDTH).astype(BF16)

    def vec_row(v, reps=1, scale=1.0):
        v = jnp.tile(v, (1, reps)) * scale
        return jnp.pad(v, ((0, 0), (0, 512 - v.shape[-1])))

    pad_g = lambda g: jnp.pad(g, ((0, 0), (0, LANES - MLA_QK_DIM)))
    vec_rows = [
        vec_row(na_g_q, 4, HEAD_DIM ** -0.5 * LOG2E), vec_row(na_g_k, 4),
        vec_row(gqa_g_q, 4, HEAD_DIM ** -0.5 * LOG2E), vec_row(gqa_g_k, 2),
        vec_row(diff_g_q, 8, DIFF_QK_DIM ** -0.5 * LOG2E), vec_row(diff_g_k, 8),
        vec_row(mla_g_cq), vec_row(mla_g_ckv),
        vec_row(pad_g(mla_g_q), 4, MLA_QK_DIM ** -0.5 * LOG2E), vec_row(pad_g(mla_g_k), 4),
    ]
    vecs = jnp.stack(vec_rows + [jnp.zeros_like(vec_rows[0])] * (16 - len(vec_rows)), axis=1)
    bds = (_block_diag_ones(256, HEAD_DIM), _block_diag_ones(256, DIFF_QK_DIM), _block_diag_ones(512, LANES))
    tables = _all_rope_tables(seq, tm)
    na_ab = _na_bias_tables(na_rpb, seq // GRID_W)
    gsub = jnp.tile(diff_g_sub, (1, 2)).reshape(depth, 1, LANES)
    wg, wo = w_gate.astype(BF16), w_out.astype(BF16)
    wb = jnp.concatenate([w_branch[:, 0:1], _gqa_head_order(w_branch[:, 1:2], 2), w_branch[:, 2:]], axis=1).astype(BF16)
    bg = b_gate.reshape(depth, N_BRANCH, 1, d)

    xa = jnp.concatenate([x.reshape(n_lat_rows, d), ctx.reshape(batch * ctx_len, d)], axis=0)

    gqa_subs = ((0, 0, 64, 0), (0, 64, 128, 1))
    diff_subs = ((0, 0, 32, 0), (0, 32, 64, 0), (0, 64, 96, 1), (0, 96, 128, 1))
    mla_subs = ((0, 0, 128, 0), (1, 0, 128, 1))

    for i in range(depth):
        need_ctx = i < depth - 1
        mod = mods[i]
        g = norm_g[i].reshape(3, 1, d)
        xa = _ffn_call(xa, mod, g[0], w1, w3, w2, i, 0, 0, all_tiles, mod_idx)
        (n, naq, nak, nav, gqr, gqn, gk, gv, dqr, dqn, dk, dv, mqr, mqn, mk, mv) = _proj_call(
            xa, mod, g[1], w_in_p, vecs, bds, wuq, wuk, wuv, tables, i, all_tiles, mod_idx, tab_idx)
        out_rows = xa.shape[0] if need_ctx else n_lat_rows
        common = dict(batch=batch, seq=seq, ctx_len=ctx_len, with_ctx_queries=need_ctx)
        lam_init = 0.8 - 0.6 * math.exp(-0.3 * i)
        outs = [
            _na_call(naq, nak, nav, na_ab[i], out_rows, **common),
            _joint_call(gqr, gqn, gk, gv, out_rows, subs=gqa_subs, q_width=128, k_width=128, k_blocks=1,
                        name="gqa_attention", **common),
            _joint_call(dqr, dqn, dk, dv, out_rows, subs=diff_subs, q_width=128, k_width=128, k_blocks=2,
                        name="diff_attention", lam=diff_lambda[i], gsub=gsub[i], diff_lam_init=lam_init, **common),
            _joint_call(mqr, mqn, mk, mv, out_rows, subs=mla_subs, q_width=256, k_width=256, k_blocks=2,
                        name="mla_attention", **common),
        ]
        n_tiles = all_tiles if need_ctx else lat_tiles
        xa = _merge_call(xa, mod, n, outs, wg, bg, wb, wo, i, n_tiles, mod_idx)
        xa = _ffn_call(xa, mod, g[2], w1, w3, w2, i, 1, 6, n_tiles, mod_idx)
    return xa.reshape(batch, seq, d)
```

```python
import functools
import math

import jax
import jax.numpy as jnp
import numpy as np
from jax import lax
from jax.experimental import pallas as pl
from jax.experimental.pallas import tpu as pltpu

F32 = jnp.float32
BF16 = jnp.bfloat16

D_MODEL = 1024
GRID_W = 64
ROPE_THETA = 10000.0
RMS_EPS = 1e-6
HEAD_DIM = 64
N_BRANCH = 4
BRANCH_HEADS = 4
BRANCH_WIDTH = 256
NA_WIN_H = 8
NA_WIN_W = 16
DIFF_QK_DIM = 32
MLA_Q_RANK = 256
MLA_KV_RANK = 128
MLA_NOPE_DIM = 64
MLA_ROPE_DIM = 32
MLA_QK_DIM = 96
D_FF = 2816
N_MOD = 9
LOG2E = 1.4426950408889634

LANES = 128
TOKEN_TILE = 512
Q_TILE = 256
KEY_CHUNK = 512
FF_TILE = 1408
NA_KEY_ROWS = 12
NEG_BIG = -1e30
SCORE_BOUND = 40.0
VMEM_LIMIT = 56 * 1024 * 1024

C_NA_Q, C_NA_K, C_NA_V = 0, 256, 512
C_GQ_Q, C_GQ_K, C_GQ_V = 768, 1024, 1152
C_DF_Q, C_DF_K, C_DF_V = 1280, 1536, 1792
C_ML_CQ, C_ML_CKV, C_ML_ROPE = 2048, 2304, 2432
IN_COLS_PAD = 2560


def _cparams(sem):
    return pltpu.CompilerParams(dimension_semantics=sem, vmem_limit_bytes=VMEM_LIMIT)


def _dot(a, b):
    return jnp.dot(a, b, preferred_element_type=F32)


def _dot_nt(a, b):
    return lax.dot_general(a, b, (((1,), (1,)), ((), ())), preferred_element_type=F32)


def _silu(a):
    return a * jax.nn.sigmoid(a)


def _modulated_norm(x, g, shift, scale):
    ms = jnp.mean(x * x, axis=-1, keepdims=True)
    return (x * lax.rsqrt(ms + RMS_EPS) * g) * (1.0 + scale) + shift


def _mod_kernel(c_ref, w_ref, b_ref, o_ref):
    c = c_ref[...]
    o_ref[...] = _dot(_silu(c).astype(BF16), w_ref[...].astype(BF16)) + b_ref[...]


def _mod_call(c16, w_ada, b_ada):
    depth, d, nd = w_ada.shape
    tn = 1024
    return pl.pallas_call(
        _mod_kernel,
        grid=(depth, nd // tn),
        in_specs=[
            pl.BlockSpec((16, d), lambda l, j: (0, 0)),
            pl.BlockSpec((None, d, tn), lambda l, j: (l, 0, j)),
            pl.BlockSpec((None, 1, tn), lambda l, j: (l, 0, j)),
        ],
        out_specs=pl.BlockSpec((None, 16, tn), lambda l, j: (l, 0, j)),
        out_shape=jax.ShapeDtypeStruct((depth, 16, nd), F32),
        compiler_params=_cparams(("arbitrary", "arbitrary")),
        name="adaln_mod",
    )(c16, w_ada, b_ada.reshape(depth, 1, nd))


def _ffn_kernel(x_ref, mod_ref, g_ref, w1_ref, w3_ref, w2_ref, o_ref, h_scr, acc_scr, *, row0, nf):
    j = pl.program_id(1)

    @pl.when(j == 0)
    def _():
        h = _modulated_norm(x_ref[...], g_ref[...], mod_ref[row0:row0 + 1, :], mod_ref[row0 + 1:row0 + 2, :])
        h_scr[...] = h.astype(BF16)

    h = h_scr[...]
    a = _dot(h, w1_ref[...])
    b = _dot(h, w3_ref[...])
    contrib = _dot((_silu(a) * b).astype(BF16), w2_ref[...])

    @pl.when(j == 0)
    def _():
        acc_scr[...] = contrib

    @pl.when(j > 0)
    def _():
        acc_scr[...] += contrib

    @pl.when(j == nf - 1)
    def _():
        o_ref[...] = x_ref[...] + (0.5 * mod_ref[row0 + 2:row0 + 3, :]) * acc_scr[...]


def _ffn_call(x, mod, g, w1, w3, w2, layer, half, row0, n_tiles, mod_idx):
    d = x.shape[1]
    tm, tf = TOKEN_TILE, FF_TILE
    nf = D_FF // tf
    return pl.pallas_call(
        functools.partial(_ffn_kernel, row0=row0, nf=nf),
        grid=(n_tiles, nf),
        in_specs=[
            pl.BlockSpec((tm, d), lambda i, j: (i, 0)),
            pl.BlockSpec((None, N_MOD, d), lambda i, j: (mod_idx(i), 0, 0)),
            pl.BlockSpec((1, d), lambda i, j: (0, 0)),
            pl.BlockSpec((None, None, d, tf), lambda i, j: (layer, half, 0, j)),
            pl.BlockSpec((None, None, d, tf), lambda i, j: (layer, half, 0, j)),
            pl.BlockSpec((None, None, tf, d), lambda i, j: (layer, half, j, 0)),
        ],
        out_specs=pl.BlockSpec((tm, d), lambda i, j: (i, 0)),
        out_shape=jax.ShapeDtypeStruct((n_tiles * tm, d), F32),
        scratch_shapes=[pltpu.VMEM((tm, d), BF16), pltpu.VMEM((tm, d), F32)],
        compiler_params=_cparams(("arbitrary", "arbitrary")),
        name="ffn_half_step",
    )(x, mod, g, w1, w3, w2)


def _segnorm(x, bd, seg_len, gain):
    x2 = x * x
    hi = x2.astype(BF16)
    lo = (x2 - hi.astype(F32)).astype(BF16)
    ss = _dot(hi, bd) + _dot(lo, bd)
    return x * lax.rsqrt(ss * (1.0 / seg_len) + RMS_EPS) * gain


def _tile_lanes(t, width):
    reps = width // t.shape[-1]
    return t if reps == 1 else jnp.concatenate([t] * reps, axis=-1)


def _rope(x, cos, sin_signed, quarter):
    width = x.shape[-1]
    lane = lax.broadcasted_iota(jnp.int32, x.shape, 1)
    first = (lane % (2 * quarter)) < quarter
    rot = jnp.where(first, pltpu.roll(x, width - quarter, 1), pltpu.roll(x, quarter, 1))
    return x * _tile_lanes(cos, width) + rot * _tile_lanes(sin_signed, width)


def _proj_kernel(x_ref, mod_ref, g_ref, w_ref, vec_ref, bd64_ref, bd32_ref, bd128_ref,
                 wuq_ref, wuk_ref, wuv_ref, c64_ref, s64_ref, c32_ref, s32_ref, cm_ref, sm_ref,
                 n_ref, naq_ref, nak_ref, nav_ref,
                 gqr_ref, gqn_ref, gk_ref, gv_ref,
                 dqr_ref, dqn_ref, dk_ref, dv_ref,
                 mqr_ref, mqn_ref, mk_ref, mv_ref):
    n = _modulated_norm(x_ref[...], g_ref[...], mod_ref[3:4, :], mod_ref[4:5, :]).astype(BF16)
    n_ref[...] = n
    p = _dot(n, w_ref[...])
    bd64, bd32, bd128 = bd64_ref[...], bd32_ref[...], bd128_ref[...]
    c64, s64 = c64_ref[...], s64_ref[...]
    c32, s32 = c32_ref[...], s32_ref[...]
    cm, sm = cm_ref[...], sm_ref[...]

    def vec(row, width):
        return vec_ref[row:row + 1, 0:width]

    naq_ref[...] = _segnorm(p[:, C_NA_Q:C_NA_Q + 256], bd64, HEAD_DIM, vec(0, 256)).astype(BF16)
    nak_ref[...] = _segnorm(p[:, C_NA_K:C_NA_K + 256], bd64, HEAD_DIM, vec(1, 256)).astype(BF16)
    nav_ref[...] = p[:, C_NA_V:C_NA_V + 256].astype(BF16)

    q = _segnorm(p[:, C_GQ_Q:C_GQ_Q + 256], bd64, HEAD_DIM, vec(2, 256))
    gqn_ref[...] = q.astype(BF16)
    gqr_ref[...] = _rope(q, c64, s64, HEAD_DIM // 4).astype(BF16)
    k = _segnorm(p[:, C_GQ_K:C_GQ_K + 128], bd64[0:128, 0:128], HEAD_DIM, vec(3, 128))
    gk_ref[...] = _rope(k, c64, s64, HEAD_DIM // 4).astype(BF16)
    gv_ref[...] = p[:, C_GQ_V:C_GQ_V + 128].astype(BF16)

    q = _segnorm(p[:, C_DF_Q:C_DF_Q + 256], bd32, DIFF_QK_DIM, vec(4, 256))
    dqn_ref[...] = q.astype(BF16)
    dqr_ref[...] = _rope(q, c32, s32, DIFF_QK_DIM // 4).astype(BF16)
    k = _segnorm(p[:, C_DF_K:C_DF_K + 256], bd32, DIFF_QK_DIM, vec(5, 256))
    dk_ref[...] = _rope(k, c32, s32, DIFF_QK_DIM // 4).astype(BF16)
    dv_ref[...] = p[:, C_DF_V:C_DF_V + 256].astype(BF16)

    cq = p[:, C_ML_CQ:C_ML_CQ + 256]
    cq = cq * lax.rsqrt(jnp.mean(cq * cq, axis=-1, keepdims=True) + RMS_EPS) * vec(6, 256)
    q = _segnorm(_dot(cq.astype(BF16), wuq_ref[...]), bd128, MLA_QK_DIM, vec(8, 512))
    mqn_ref[...] = q.astype(BF16)
    mqr_ref[...] = _rope(q, cm, sm, MLA_ROPE_DIM // 4).astype(BF16)
    ckv = p[:, C_ML_CKV:C_ML_CKV + 128]
    ckv = (ckv * lax.rsqrt(jnp.mean(ckv * ckv, axis=-1, keepdims=True) + RMS_EPS) * vec(7, 128)).astype(BF16)
    k_rope = pltpu.roll(p[:, C_ML_ROPE:C_ML_ROPE + 128], MLA_NOPE_DIM, 1)
    k = _dot(ckv, wuk_ref[...]) + _tile_lanes(k_rope, 512)
    k = _segnorm(k, bd128, MLA_QK_DIM, vec(9, 512))
    mk_ref[...] = _rope(k, cm, sm, MLA_ROPE_DIM // 4).astype(BF16)
    mv_ref[...] = _dot(ckv, wuv_ref[...]).astype(BF16)


_PROJ_OUT_WIDTHS = (1024, 256, 256, 256, 256, 256, 128, 128, 256, 256, 256, 256, 512, 512, 512, 256)


def _proj_call(x, mod, g, w_in, vecs, bds, wuq, wuk, wuv, tables, layer, n_tiles, mod_idx, tab_idx):
    d = x.shape[1]
    tm = TOKEN_TILE
    const2 = lambda i: (0, 0)
    in_specs = [
        pl.BlockSpec((tm, d), lambda i: (i, 0)),
        pl.BlockSpec((None, N_MOD, d), lambda i: (mod_idx(i), 0, 0)),
        pl.BlockSpec((1, d), const2),
        pl.BlockSpec((None, d, IN_COLS_PAD), lambda i: (layer, 0, 0)),
        pl.BlockSpec((None, 16, 512), lambda i: (layer, 0, 0)),
        pl.BlockSpec((256, 256), const2),
        pl.BlockSpec((256, 256), const2),
        pl.BlockSpec((512, 512), const2),
        pl.BlockSpec((None, 256, 512), lambda i: (layer, 0, 0)),
        pl.BlockSpec((None, 128, 512), lambda i: (layer, 0, 0)),
        pl.BlockSpec((None, 128, 256), lambda i: (layer, 0, 0)),
    ] + [pl.BlockSpec((tm, LANES), lambda i: (tab_idx(i), 0))] * 6
    rows = n_tiles * tm
    return pl.pallas_call(
        _proj_kernel,
        grid=(n_tiles,),
        in_specs=in_specs,
        out_specs=[pl.BlockSpec((tm, w), lambda i: (i, 0)) for w in _PROJ_OUT_WIDTHS],
        out_shape=[jax.ShapeDtypeStruct((rows, w), BF16) for w in _PROJ_OUT_WIDTHS],
        compiler_params=_cparams(("arbitrary",)),
        name="mixer_projection",
    )(x, mod, g, w_in, vecs, *bds, wuq, wuk, wuv, *tables)


def _softmax_step(state, s, v_t, bounded):
    m, l, acc = state
    if bounded:
        p = jnp.exp2(s)
        return m, l + jnp.sum(p, axis=0, keepdims=True), acc + _dot(v_t, p.astype(BF16))
    m_new = jnp.maximum(m, jnp.max(s, axis=0, keepdims=True))
    alpha = jnp.exp2(m - m_new)
    p = jnp.exp2(s - m_new)
    l = alpha * l + jnp.sum(p, axis=0, keepdims=True)
    acc = alpha * acc + _dot(v_t, p.astype(BF16))
    return m_new, l, acc


def _init_state(tq):
    return (jnp.full((1, tq), NEG_BIG, F32), jnp.zeros((1, tq), F32), jnp.zeros((HEAD_DIM, tq), F32))


def _score_bound(g_q, g_k, dim):
    return 1.01 * dim * (dim ** -0.5 * LOG2E) * jnp.max(jnp.abs(g_q)) * jnp.max(jnp.abs(g_k))


def _transpose_bf16(eye, a):
    return _dot_nt(eye[0:a.shape[1], 0:a.shape[1]], a).astype(BF16)


def _masked_q_t(eye, q_ref, slab, lo, hi):
    q = q_ref[:, slab * LANES:(slab + 1) * LANES]
    if not (lo == 0 and hi == LANES):
        lane = lax.broadcasted_iota(jnp.int32, q.shape, 1)
        q = jnp.where((lane >= lo) & (lane < hi), q, jnp.zeros_like(q))
    return _transpose_bf16(eye, q)


def _head_rows(v_t, half):
    return v_t[half * HEAD_DIM:(half + 1) * HEAD_DIM, :]


def _store_heads(eye, o_ref, o_t_halves):
    o_t = jnp.concatenate(o_t_halves, axis=0).astype(BF16)
    o_ref[...] = _dot_nt(eye, o_t).astype(o_ref.dtype)


def _joint_kernel(*refs, subs, n_lat_tiles, seq, diff_lam_init, bounded):
    if diff_lam_init is None:
        qr_ref, qn_ref, kl_ref, kc_ref, vl_ref, vc_ref, eye_ref, o_ref, vt_scr, vct_scr = refs
    else:
        qr_ref, qn_ref, kl_ref, kc_ref, vl_ref, vc_ref, eye_ref, lam_ref, gsub_ref, o_ref, vt_scr, vct_scr = refs
    t = pl.program_id(2)
    tq = o_ref.shape[0]
    n_chunks = seq // KEY_CHUNK
    eye = eye_ref[...]
    init = tuple(_init_state(tq) for _ in subs)

    @pl.when(t == 0)
    def _():
        for c in range(n_chunks):
            vt_scr[c] = _transpose_bf16(eye, vl_ref[c * KEY_CHUNK:(c + 1) * KEY_CHUNK, :])
        vct_scr[...] = _transpose_bf16(eye, vc_ref[...])

    def latent_part(states):
        qs = [_masked_q_t(eye, qr_ref, slab, lo, hi) for (slab, lo, hi, _) in subs]

        def body(c, st):
            off = pl.multiple_of(c * KEY_CHUNK, KEY_CHUNK)
            v_t = vt_scr[c]
            scores = [_dot(kl_ref[pl.ds(off, KEY_CHUNK), slab * LANES:(slab + 1) * LANES], qs[i])
                      for i, (slab, _, _, _) in enumerate(subs)]
            return tuple(_softmax_step(st[i], scores[i], _head_rows(v_t, vh), bounded)
                         for i, (_, _, _, vh) in enumerate(subs))

        return lax.fori_loop(0, n_chunks, body, states)

    states = lax.cond(t < n_lat_tiles, latent_part, lambda st: st, init)
    vc_t = vct_scr[...]
    outs = []
    for i, (slab, lo, hi, vh) in enumerate(subs):
        kc = kc_ref[:, slab * LANES:(slab + 1) * LANES]
        s_ctx = _dot(kc, _masked_q_t(eye, qn_ref, slab, lo, hi))
        _, l, acc = _softmax_step(states[i], s_ctx, _head_rows(vc_t, vh), bounded)
        outs.append(acc / l)

    if diff_lam_init is None:
        by_half = {vh: o for (_, _, _, vh), o in zip(subs, outs)}
        _store_heads(eye, o_ref, [by_half[0], by_half[1]])
    else:
        lv = lam_ref[...]
        lam = (jnp.exp(jnp.sum(lv[0:1, :] * lv[1:2, :], axis=-1, keepdims=True))
               - jnp.exp(jnp.sum(lv[2:3, :] * lv[3:4, :], axis=-1, keepdims=True)) + diff_lam_init)
        halves = []
        for o1, o2 in ((outs[0], outs[1]), (outs[2], outs[3])):
            d = o1 - lam * o2
            ms = jnp.mean(d * d, axis=0, keepdims=True)
            halves.append((d * lax.rsqrt(ms + RMS_EPS) * gsub_ref[...]) * (1.0 - diff_lam_init))
        _store_heads(eye, o_ref, halves)


def _joint_call(operands, out_rows, *, subs, q_width, k_width, k_blocks, batch, seq, ctx_len,
                with_ctx_queries, name, bounded, diff_lam_init=None):
    qr, qn, k, v = operands[:4]
    tq = Q_TILE
    n_lat = seq // tq
    nqt = n_lat + (1 if with_ctx_queries else 0)
    ctx_row0 = batch * seq // ctx_len

    def qrow(b, t):
        return jnp.where(t < n_lat, b * n_lat + t, batch * n_lat + b)

    kcol = (lambda blk: blk) if k_blocks > 1 else (lambda blk: 0)
    in_specs = [
        pl.BlockSpec((tq, q_width), lambda b, blk, t: (qrow(b, t), blk)),
        pl.BlockSpec((tq, q_width), lambda b, blk, t: (qrow(b, t), blk)),
        pl.BlockSpec((seq, k_width), lambda b, blk, t: (b, kcol(blk))),
        pl.BlockSpec((ctx_len, k_width), lambda b, blk, t: (ctx_row0 + b, kcol(blk))),
        pl.BlockSpec((seq, LANES), lambda b, blk, t: (b, kcol(blk))),
        pl.BlockSpec((ctx_len, LANES), lambda b, blk, t: (ctx_row0 + b, kcol(blk))),
        pl.BlockSpec((tq, tq), lambda b, blk, t: (0, 0)),
    ]
    args = [qr, qn, k, k, v, v, jnp.eye(tq, dtype=BF16)]
    if diff_lam_init is not None:
        in_specs += [pl.BlockSpec((4, DIFF_QK_DIM), lambda b, blk, t: (0, 0)),
                     pl.BlockSpec((HEAD_DIM, tq), lambda b, blk, t: (0, 0))]
        args += list(operands[4:])
    return pl.pallas_call(
        functools.partial(_joint_kernel, subs=subs, n_lat_tiles=n_lat, seq=seq, diff_lam_init=diff_lam_init,
                          bounded=bounded),
        grid=(batch, 2, nqt),
        in_specs=in_specs,
        out_specs=pl.BlockSpec((tq, LANES), lambda b, blk, t: (qrow(b, t), blk)),
        out_shape=jax.ShapeDtypeStruct((out_rows, BRANCH_WIDTH), BF16),
        scratch_shapes=[pltpu.VMEM((seq // KEY_CHUNK, LANES, KEY_CHUNK), BF16), pltpu.VMEM((LANES, ctx_len), BF16)],
        compiler_params=_cparams(("arbitrary", "arbitrary", "arbitrary")),
        name=name,
    )(*args)


def _na_kernel(q_ref, kl_ref, kc_ref, vl_ref, vc_ref, ab_ref, eye_ref, o_ref, vt_scr, vct_scr, *, n_lat_tiles):
    t = pl.program_id(2)
    tq = o_ref.shape[0]
    eye = eye_ref[...]
    subs = ((0, HEAD_DIM), (HEAD_DIM, LANES))
    rows_per_tile = tq // GRID_W
    band_tiles = NA_KEY_ROWS // rows_per_tile

    @pl.when(t == 0)
    def _():
        for c in range(n_lat_tiles):
            vt_scr[c] = _transpose_bf16(eye, vl_ref[c * tq:(c + 1) * tq, :])
        vct_scr[...] = _transpose_bf16(eye, vc_ref[...])

    def attend(blocks):
        halves = []
        for i, (lo, hi) in enumerate(subs):
            q_t = _masked_q_t(eye, q_ref, 0, lo, hi)
            scores = []
            for k, _, bias in blocks:
                s = _dot(k, q_t)
                scores.append(s if bias is None else s + bias[i])
            m = functools.reduce(jnp.maximum, [jnp.max(s, axis=0, keepdims=True) for s in scores])
            l, acc = 0.0, 0.0
            for s, (_, v_t, _) in zip(scores, blocks):
                p = jnp.exp2(s - m)
                l = l + jnp.sum(p, axis=0, keepdims=True)
                acc = acc + _dot(_head_rows(v_t, i), p.astype(BF16))
            halves.append(acc / l)
        _store_heads(eye, o_ref, halves)

    @pl.when(t < n_lat_tiles)
    def _():
        tile0 = jnp.clip(t - NA_WIN_H // 2 // rows_per_tile, 0, n_lat_tiles - band_tiles)
        off = pl.multiple_of(tile0 * tq, tq)
        k = kl_ref[pl.ds(off, band_tiles * tq), :]
        v_t = jnp.concatenate([vt_scr[tile0 + j] for j in range(band_tiles)], axis=1)
        attend([(k, v_t, ab_ref), (kc_ref[...], vct_scr[...], None)])

    @pl.when(t >= n_lat_tiles)
    def _():
        attend([(kc_ref[...], vct_scr[...], None)])


def _na_call(q, k, v, ab, out_rows, *, batch, seq, ctx_len, with_ctx_queries):
    tq = Q_TILE
    n_lat = seq // tq
    nqt = n_lat + (1 if with_ctx_queries else 0)
    ctx_row0 = batch * seq // ctx_len
    n_keys = NA_KEY_ROWS * GRID_W

    def qrow(b, t):
        return jnp.where(t < n_lat, b * n_lat + t, batch * n_lat + b)

    def pattern(t):
        return jnp.where(t == 0, 0, jnp.where(t >= n_lat - 1, 2, 1))

    return pl.pallas_call(
        functools.partial(_na_kernel, n_lat_tiles=n_lat),
        grid=(batch, 2, nqt),
        in_specs=[
            pl.BlockSpec((tq, LANES), lambda b, blk, t: (qrow(b, t), blk)),
            pl.BlockSpec((seq, LANES), lambda b, blk, t: (b, blk)),
            pl.BlockSpec((ctx_len, LANES), lambda b, blk, t: (ctx_row0 + b, blk)),
            pl.BlockSpec((seq, LANES), lambda b, blk, t: (b, blk)),
            pl.BlockSpec((ctx_len, LANES), lambda b, blk, t: (ctx_row0 + b, blk)),
            pl.BlockSpec((None, None, 2, n_keys, tq), lambda b, blk, t: (pattern(t), blk, 0, 0, 0)),
            pl.BlockSpec((tq, tq), lambda b, blk, t: (0, 0)),
        ],
        out_specs=pl.BlockSpec((tq, LANES), lambda b, blk, t: (qrow(b, t), blk)),
        out_shape=jax.ShapeDtypeStruct((out_rows, BRANCH_WIDTH), BF16),
        scratch_shapes=[pltpu.VMEM((n_lat, LANES, tq), BF16), pltpu.VMEM((LANES, ctx_len), BF16)],
        compiler_params=_cparams(("arbitrary", "arbitrary", "arbitrary")),
        name="neighbourhood_attention",
    )(q, k, k, v, v, ab, jnp.eye(tq, dtype=BF16))


def _merge_kernel(x_ref, mod_ref, n_ref, o0_ref, o1_ref, o2_ref, o3_ref, wg_ref, bg_ref, wb_ref, wo_ref,
                  out_ref, y_scr):
    j = pl.program_id(1)
    gate = jax.nn.sigmoid(_dot(n_ref[...], wg_ref[...]) + bg_ref[...])
    for jj, o_ref in enumerate((o0_ref, o1_ref, o2_ref, o3_ref)):
        @pl.when(j == jj)
        def _(o_ref=o_ref, jj=jj):
            contrib = gate * _dot(o_ref[...], wb_ref[...])
            if jj == 0:
                y_scr[...] = contrib
            else:
                y_scr[...] += contrib

    @pl.when(j == N_BRANCH - 1)
    def _():
        out_ref[...] = x_ref[...] + mod_ref[5:6, :] * _dot(y_scr[...].astype(BF16), wo_ref[...])


def _merge_call(x, mod, n, outs, w_gate, b_gate, w_branch, w_out, layer, n_tiles, mod_idx):
    d = x.shape[1]
    tm = TOKEN_TILE
    row = lambda i, j: (i, 0)
    return pl.pallas_call(
        _merge_kernel,
        grid=(n_tiles, N_BRANCH),
        in_specs=[
            pl.BlockSpec((tm, d), row),
            pl.BlockSpec((None, N_MOD, d), lambda i, j: (mod_idx(i), 0, 0)),
            pl.BlockSpec((tm, d), row),
        ] + [pl.BlockSpec((tm, BRANCH_WIDTH), row)] * 4 + [
            pl.BlockSpec((None, None, d, d), lambda i, j: (layer, j, 0, 0)),
            pl.BlockSpec((None, None, 1, d), lambda i, j: (layer, j, 0, 0)),
            pl.BlockSpec((None, None, BRANCH_WIDTH, d), lambda i, j: (layer, j, 0, 0)),
            pl.BlockSpec((None, d, d), lambda i, j: (layer, 0, 0)),
        ],
        out_specs=pl.BlockSpec((tm, d), row),
        out_shape=jax.ShapeDtypeStruct((n_tiles * tm, d), F32),
        scratch_shapes=[pltpu.VMEM((tm, d), F32)],
        compiler_params=_cparams(("arbitrary", "arbitrary")),
        name="branch_merge",
    )(x, mod, n, *outs, w_gate, b_gate, w_branch, w_out)


def _rope_tables(seq, rot_dim, pad_rows):
    t = jnp.arange(seq, dtype=jnp.int32)
    row = (t // GRID_W).astype(F32)
    col = (t % GRID_W).astype(F32)
    n_freq = rot_dim // 4
    inv = ROPE_THETA ** (-jnp.arange(n_freq, dtype=F32) / n_freq)
    ar = row[:, None] * inv[None, :]
    ac = col[:, None] * inv[None, :]
    ang = jnp.concatenate([ar, ar, ac, ac], axis=-1)
    sign = jnp.tile(jnp.concatenate([-jnp.ones((n_freq,), F32), jnp.ones((n_freq,), F32)]), 2)
    cos = jnp.concatenate([jnp.cos(ang), jnp.ones((pad_rows, rot_dim), F32)], axis=0)
    sin = jnp.concatenate([jnp.sin(ang) * sign[None, :], jnp.zeros((pad_rows, rot_dim), F32)], axis=0)
    return cos, sin


def _all_rope_tables(seq, pad_rows):
    c64, s64 = _rope_tables(seq, HEAD_DIM, pad_rows)
    c32, s32 = _rope_tables(seq, DIFF_QK_DIM, pad_rows)
    rows = seq + pad_rows
    ones = lambda w: jnp.ones((rows, w), F32)
    zeros = lambda w: jnp.zeros((rows, w), F32)
    pad = LANES - MLA_QK_DIM
    cm = jnp.concatenate([ones(MLA_NOPE_DIM), c32, ones(pad)], axis=-1)
    sm = jnp.concatenate([zeros(MLA_NOPE_DIM), s32, zeros(pad)], axis=-1)
    return (jnp.tile(c64, (1, 2)), jnp.tile(s64, (1, 2)), jnp.tile(c32, (1, 4)), jnp.tile(s32, (1, 4)), cm, sm)


def _block_diag_ones(width, seg):
    idx = np.arange(width) // seg
    return jnp.asarray(idx[:, None] == idx[None, :], dtype=BF16)


def _na_bias_tables(rpb, grid_rows):
    rows_per_tile = Q_TILE // GRID_W
    wh = min(NA_WIN_H, grid_rows)
    kinds = ((0, 0), (rows_per_tile, 0), (grid_rows - rows_per_tile, grid_rows - NA_KEY_ROWS))
    col = np.arange(GRID_W)
    c0 = np.clip(col - NA_WIN_W // 2, 0, GRID_W - NA_WIN_W)
    col_ok = (col[:, None] >= c0[None, :]) & (col[:, None] < c0[None, :] + NA_WIN_W)
    d_col = np.clip(col[:, None] - col[None, :], -(NA_WIN_W - 1), NA_WIN_W - 1) + (NA_WIN_W - 1)
    col_hot = (d_col[..., None] == np.arange(2 * NA_WIN_W - 1)).astype(np.float32)
    row_hot, valid = [], []
    for q_row0, k_row0 in kinds:
        r = q_row0 + np.arange(rows_per_tile)
        kr = k_row0 + np.arange(NA_KEY_ROWS)
        r0 = np.clip(r - wh // 2, 0, grid_rows - wh)
        row_ok = (kr[:, None] >= r0[None, :]) & (kr[:, None] < r0[None, :] + wh)
        d_row = kr[:, None] - r[None, :] + (NA_WIN_H - 1)
        row_hot.append((d_row[..., None] == np.arange(2 * NA_WIN_H - 1)).astype(np.float32))
        valid.append(row_ok[:, None, :, None] & col_ok[None, :, None, :])
    bias = jnp.einsum("nkqa,lhab,dcb->lnhkdqc", jnp.asarray(np.stack(row_hot)), rpb.astype(F32),
                      jnp.asarray(col_hot), precision=lax.Precision.HIGHEST)
    ab = jnp.where(jnp.asarray(np.stack(valid))[None, :, None], bias * LOG2E, NEG_BIG)
    depth = rpb.shape[0]
    return ab.reshape(depth, 3, 2, 2, NA_KEY_ROWS * GRID_W, Q_TILE)


def _pad_heads(w, head_w, to_w):
    lead = w.shape[:-1]
    w = w.reshape(lead + (BRANCH_HEADS, head_w))
    w = jnp.pad(w, [(0, 0)] * len(lead) + [(0, 0), (0, to_w - head_w)])
    return w.reshape(lead + (BRANCH_HEADS * to_w,))


def _gqa_head_order(a, axis):
    parts = jnp.split(a, BRANCH_HEADS, axis=axis)
    return jnp.concatenate([parts[0], parts[2], parts[1], parts[3]], axis=axis)


def kernel(x, c, ctx, c_ctx, w_ada, b_ada, norm_g, ffn_w1, ffn_w3, ffn_w2, w_in, na_g_q, na_g_k, na_rpb, gqa_g_q, gqa_g_k, diff_g_q, diff_g_k, diff_lambda, diff_g_sub, mla_g_cq, mla_g_ckv, mla_w_uq, mla_w_ukv, mla_g_q, mla_g_k, w_gate, b_gate, w_branch, w_out):
    batch, seq, d = x.shape
    ctx_len = ctx.shape[1]
    depth = w_ada.shape[0]
    tm = TOKEN_TILE
    assert d == D_MODEL and ctx_len == Q_TILE and seq % tm == 0 and (batch * ctx_len) % tm == 0
    assert seq % KEY_CHUNK == 0 and seq // GRID_W >= NA_KEY_ROWS and batch < 16
    n_lat_rows = batch * seq
    lat_tiles = n_lat_rows // tm
    all_tiles = lat_tiles + batch * ctx_len // tm
    tiles_per_batch = seq // tm

    def mod_idx(i):
        return jnp.where(i < lat_tiles, i // tiles_per_batch, batch)

    def tab_idx(i):
        return jnp.where(i < lat_tiles, i % tiles_per_batch, tiles_per_batch)

    c16 = jnp.concatenate([c, c_ctx[None, :], jnp.zeros((16 - batch - 1, d), F32)], axis=0)
    mods = _mod_call(c16, w_ada, b_ada).reshape(depth, 16, N_MOD, d)
    w1, w3, w2 = ffn_w1.astype(BF16), ffn_w3.astype(BF16), ffn_w2.astype(BF16)
    na_w, gq_w, df_w, ml_w = jnp.split(w_in, [768, 1280, 2048], axis=-1)
    w_in_p = jnp.concatenate([
        na_w, _gqa_head_order(gq_w[..., :256], -1), gq_w[..., 256:], df_w, ml_w,
        jnp.zeros((depth, d, IN_COLS_PAD - w_in.shape[-1]), F32)], axis=-1).astype(BF16)
    wuq = _pad_heads(mla_w_uq, MLA_QK_DIM, LANES).astype(BF16)
    wukv = mla_w_ukv.reshape(depth, MLA_KV_RANK, BRANCH_HEADS, MLA_NOPE_DIM + HEAD_DIM)
    wuk = _pad_heads(wukv[..., :MLA_NOPE_DIM].reshape(depth, MLA_KV_RANK, -1), MLA_NOPE_DIM, LANES).astype(BF16)
    wuv = wukv[..., MLA_NOPE_DIM:].reshape(depth, MLA_KV_RANK, BRANCH_WIDTH).astype(BF16)

    def vec_row(v, reps=1, scale=1.0):
        v = jnp.tile(v, (1, reps)) * scale
        return jnp.pad(v, ((0, 0), (0, 512 - v.shape[-1])))

    pad_g = lambda g: jnp.pad(g, ((0, 0), (0, LANES - MLA_QK_DIM)))
    vec_rows = [
        vec_row(na_g_q, 4, HEAD_DIM ** -0.5 * LOG2E), vec_row(na_g_k, 4),
        vec_row(gqa_g_q, 4, HEAD_DIM ** -0.5 * LOG2E), vec_row(gqa_g_k, 2),
        vec_row(diff_g_q, 8, DIFF_QK_DIM ** -0.5 * LOG2E), vec_row(diff_g_k, 8),
        vec_row(mla_g_cq), vec_row(mla_g_ckv),
        vec_row(pad_g(mla_g_q), 4, MLA_QK_DIM ** -0.5 * LOG2E), vec_row(pad_g(mla_g_k), 4),
    ]
    vecs = jnp.stack(vec_rows + [jnp.zeros_like(vec_rows[0])] * (16 - len(vec_rows)), axis=1)
    bds = (_block_diag_ones(256, HEAD_DIM), _block_diag_ones(256, DIFF_QK_DIM), _block_diag_ones(512, LANES))
    tables = _all_rope_tables(seq, tm)
    na_ab = _na_bias_tables(na_rpb, seq // GRID_W)
    gsub = jnp.broadcast_to(diff_g_sub[:, :, None], (depth, HEAD_DIM, Q_TILE))
    wg, wo = w_gate.astype(BF16), w_out.astype(BF16)
    wb = jnp.concatenate([w_branch[:, 0:1], _gqa_head_order(w_branch[:, 1:2], 2), w_branch[:, 2:]], axis=1).astype(BF16)
    bg = b_gate.reshape(depth, N_BRANCH, 1, d)

    xa = jnp.concatenate([x.reshape(n_lat_rows, d), ctx.reshape(batch * ctx_len, d)], axis=0)

    gqa_subs = ((0, 0, 64, 0), (0, 64, 128, 1))
    diff_subs = ((0, 0, 32, 0), (0, 32, 64, 0), (0, 64, 96, 1), (0, 96, 128, 1))
    mla_subs = ((0, 0, 128, 0), (1, 0, 128, 1))

    for i in range(depth):
        need_ctx = i < depth - 1
        mod = mods[i]
        g = norm_g[i].reshape(3, 1, d)
        xa = _ffn_call(xa, mod, g[0], w1, w3, w2, i, 0, 0, all_tiles, mod_idx)
        (n, naq, nak, nav, gqr, gqn, gk, gv, dqr, dqn, dk, dv, mqr, mqn, mk, mv) = _proj_call(
            xa, mod, g[1], w_in_p, vecs, bds, wuq, wuk, wuv, tables, i, all_tiles, mod_idx, tab_idx)
        out_rows = xa.shape[0] if need_ctx else n_lat_rows
        common = dict(batch=batch, seq=seq, ctx_len=ctx_len, with_ctx_queries=need_ctx)
        lam_init = 0.8 - 0.6 * math.exp(-0.3 * i)
        def joint(bound, *operands, **kw):
            return lax.cond(bound <= SCORE_BOUND,
                            lambda *a: _joint_call(a, out_rows, bounded=True, **kw, **common),
                            lambda *a: _joint_call(a, out_rows, bounded=False, **kw, **common), *operands)

        outs = [
            _na_call(naq, nak, nav, na_ab[i], out_rows, **common),
            joint(_score_bound(gqa_g_q[i], gqa_g_k[i], HEAD_DIM), gqr, gqn, gk, gv,
                  subs=gqa_subs, q_width=128, k_width=128, k_blocks=1, name="gqa_attention"),
            joint(_score_bound(diff_g_q[i], diff_g_k[i], DIFF_QK_DIM), dqr, dqn, dk, dv, diff_lambda[i], gsub[i],
                  subs=diff_subs, q_width=128, k_width=128, k_blocks=2, name="diff_attention", diff_lam_init=lam_init),
            joint(_score_bound(mla_g_q[i], mla_g_k[i], MLA_QK_DIM), mqr, mqn, mk, mv,
                  subs=mla_subs, q_width=256, k_width=256, k_blocks=2, name="mla_attention"),
        ]
        n_tiles = all_tiles if need_ctx else lat_tiles
        xa = _merge_call(xa, mod, n, outs, wg, bg, wb, wo, i, n_tiles, mod_idx)
        xa = _ffn_call(xa, mod, g[2], w1, w3, w2, i, 1, 6, n_tiles, mod_idx)
    return xa.reshape(batch, seq, d)
```

```python
import functools
import math

import jax
import jax.numpy as jnp
import numpy as np
from jax import lax
from jax.experimental import pallas as pl
from jax.experimental.pallas import tpu as pltpu

F32 = jnp.float32
BF16 = jnp.bfloat16

D_MODEL = 1024
GRID_W = 64
ROPE_THETA = 10000.0
RMS_EPS = 1e-6
HEAD_DIM = 64
N_BRANCH = 4
BRANCH_HEADS = 4
BRANCH_WIDTH = 256
NA_WIN_H = 8
NA_WIN_W = 16
DIFF_QK_DIM = 32
MLA_Q_RANK = 256
MLA_KV_RANK = 128
MLA_NOPE_DIM = 64
MLA_ROPE_DIM = 32
MLA_QK_DIM = 96
D_FF = 2816
N_MOD = 9
LOG2E = 1.4426950408889634

LANES = 128
TOKEN_TILE = 512
Q_TILE = 256
KEY_CHUNK = 512
FF_SLAB = 256
MERGE_SLAB = 256
NA_KEY_ROWS = 12
NEG_BIG = -1e30
SCORE_BOUND = 40.0
VMEM_LIMIT = 56 * 1024 * 1024

C_NA_Q, C_NA_K, C_NA_V = 0, 256, 512
C_GQ_Q, C_GQ_K, C_GQ_V = 768, 1024, 1152
C_DF_Q, C_DF_K, C_DF_V = 1280, 1536, 1792
C_ML_CQ, C_ML_CKV, C_ML_ROPE = 2048, 2304, 2432
IN_COLS_PAD = 2560


def _cparams(sem):
    return pltpu.CompilerParams(dimension_semantics=sem, vmem_limit_bytes=VMEM_LIMIT)


def _dot(a, b):
    return jnp.dot(a, b, preferred_element_type=F32)


def _dot_nt(a, b):
    return lax.dot_general(a, b, (((1,), (1,)), ((), ())), preferred_element_type=F32)


def _silu(a):
    return a * jax.nn.sigmoid(a)


def _modulated_norm(x, g, shift, scale):
    ms = jnp.mean(x * x, axis=-1, keepdims=True)
    return (x * lax.rsqrt(ms + RMS_EPS) * g) * (1.0 + scale) + shift


def _mod_kernel(c_ref, w_ref, b_ref, o_ref):
    c = c_ref[...]
    o_ref[...] = _dot(_silu(c).astype(BF16), w_ref[...].astype(BF16)) + b_ref[...]


def _mod_call(c16, w_ada, b_ada):
    depth, d, nd = w_ada.shape
    tn = 1024
    return pl.pallas_call(
        _mod_kernel,
        grid=(depth, nd // tn),
        in_specs=[
            pl.BlockSpec((16, d), lambda l, j: (0, 0)),
            pl.BlockSpec((None, d, tn), lambda l, j: (l, 0, j)),
            pl.BlockSpec((None, 1, tn), lambda l, j: (l, 0, j)),
        ],
        out_specs=pl.BlockSpec((None, 16, tn), lambda l, j: (l, 0, j)),
        out_shape=jax.ShapeDtypeStruct((depth, 16, nd), F32),
        compiler_params=_cparams(("arbitrary", "arbitrary")),
        name="adaln_mod",
    )(c16, w_ada, b_ada.reshape(depth, 1, nd))


def _ffn_kernel(x_ref, mod_ref, g_ref, w1_ref, w3_ref, w2_ref, o_ref, act_scr, *, row0):
    x = x_ref[...]
    h = _modulated_norm(x, g_ref[...], mod_ref[row0:row0 + 1, :], mod_ref[row0 + 1:row0 + 2, :]).astype(BF16)
    for f0 in range(0, D_FF, FF_SLAB):
        a = _dot(h, w1_ref[:, f0:f0 + FF_SLAB])
        b = _dot(h, w3_ref[:, f0:f0 + FF_SLAB])
        act_scr[:, f0:f0 + FF_SLAB] = (_silu(a) * b).astype(BF16)
    o_ref[...] = x + (0.5 * mod_ref[row0 + 2:row0 + 3, :]) * _dot(act_scr[...], w2_ref[...])


def _resident(block_shape, index_map):
    return pl.BlockSpec(block_shape, index_map, pipeline_mode=pl.Buffered(1))


def _ffn_call(x, mod, g, w1, w3, w2, layer, half, row0, n_tiles, mod_idx):
    d = x.shape[1]
    tm = TOKEN_TILE
    return pl.pallas_call(
        functools.partial(_ffn_kernel, row0=row0),
        grid=(n_tiles,),
        in_specs=[
            pl.BlockSpec((tm, d), lambda i: (i, 0)),
            pl.BlockSpec((None, N_MOD, d), lambda i: (mod_idx(i), 0, 0)),
            _resident((1, d), lambda i: (0, 0)),
            _resident((None, None, d, D_FF), lambda i: (layer, half, 0, 0)),
            _resident((None, None, d, D_FF), lambda i: (layer, half, 0, 0)),
            _resident((None, None, D_FF, d), lambda i: (layer, half, 0, 0)),
        ],
        out_specs=pl.BlockSpec((tm, d), lambda i: (i, 0)),
        out_shape=jax.ShapeDtypeStruct((n_tiles * tm, d), F32),
        scratch_shapes=[pltpu.VMEM((tm, D_FF), BF16)],
        compiler_params=_cparams(("arbitrary",)),
        name="ffn_half_step",
    )(x, mod, g, w1, w3, w2)


def _segnorm(x, bd, seg_len, gain):
    x2 = x * x
    hi = x2.astype(BF16)
    lo = (x2 - hi.astype(F32)).astype(BF16)
    ss = _dot(hi, bd) + _dot(lo, bd)
    return x * lax.rsqrt(ss * (1.0 / seg_len) + RMS_EPS) * gain


def _tile_lanes(t, width):
    reps = width // t.shape[-1]
    return t if reps == 1 else jnp.concatenate([t] * reps, axis=-1)


def _rope(x, cos, sin_signed, quarter):
    width = x.shape[-1]
    lane = lax.broadcasted_iota(jnp.int32, x.shape, 1)
    first = (lane % (2 * quarter)) < quarter
    rot = jnp.where(first, pltpu.roll(x, width - quarter, 1), pltpu.roll(x, quarter, 1))
    return x * _tile_lanes(cos, width) + rot * _tile_lanes(sin_signed, width)


def _proj_kernel(x_ref, mod_ref, g_ref, w_ref, vec_ref, bd64_ref, bd32_ref, bd128_ref,
                 wuq_ref, wuk_ref, wuv_ref, c64_ref, s64_ref, c32_ref, s32_ref, cm_ref, sm_ref,
                 n_ref, naq_ref, nak_ref, nav_ref,
                 gqr_ref, gqn_ref, gk_ref, gv_ref,
                 dqr_ref, dqn_ref, dk_ref, dv_ref,
                 mqr_ref, mqn_ref, mk_ref, mv_ref):
    n = _modulated_norm(x_ref[...], g_ref[...], mod_ref[3:4, :], mod_ref[4:5, :]).astype(BF16)
    n_ref[...] = n
    p = _dot(n, w_ref[...])
    bd64, bd32, bd128 = bd64_ref[...], bd32_ref[...], bd128_ref[...]
    c64, s64 = c64_ref[...], s64_ref[...]
    c32, s32 = c32_ref[...], s32_ref[...]
    cm, sm = cm_ref[...], sm_ref[...]

    def vec(row, width):
        return vec_ref[row:row + 1, 0:width]

    naq_ref[...] = _segnorm(p[:, C_NA_Q:C_NA_Q + 256], bd64, HEAD_DIM, vec(0, 256)).astype(BF16)
    nak_ref[...] = _segnorm(p[:, C_NA_K:C_NA_K + 256], bd64, HEAD_DIM, vec(1, 256)).astype(BF16)
    nav_ref[...] = p[:, C_NA_V:C_NA_V + 256].astype(BF16)

    q = _segnorm(p[:, C_GQ_Q:C_GQ_Q + 256], bd64, HEAD_DIM, vec(2, 256))
    gqn_ref[...] = q.astype(BF16)
    gqr_ref[...] = _rope(q, c64, s64, HEAD_DIM // 4).astype(BF16)
    k = _segnorm(p[:, C_GQ_K:C_GQ_K + 128], bd64[0:128, 0:128], HEAD_DIM, vec(3, 128))
    gk_ref[...] = _rope(k, c64, s64, HEAD_DIM // 4).astype(BF16)
    gv_ref[...] = p[:, C_GQ_V:C_GQ_V + 128].astype(BF16)

    q = _segnorm(p[:, C_DF_Q:C_DF_Q + 256], bd32, DIFF_QK_DIM, vec(4, 256))
    dqn_ref[...] = q.astype(BF16)
    dqr_ref[...] = _rope(q, c32, s32, DIFF_QK_DIM // 4).astype(BF16)
    k = _segnorm(p[:, C_DF_K:C_DF_K + 256], bd32, DIFF_QK_DIM, vec(5, 256))
    dk_ref[...] = _rope(k, c32, s32, DIFF_QK_DIM // 4).astype(BF16)
    dv_ref[...] = p[:, C_DF_V:C_DF_V + 256].astype(BF16)

    cq = p[:, C_ML_CQ:C_ML_CQ + 256]
    cq = cq * lax.rsqrt(jnp.mean(cq * cq, axis=-1, keepdims=True) + RMS_EPS) * vec(6, 256)
    q = _segnorm(_dot(cq.astype(BF16), wuq_ref[...]), bd128, MLA_QK_DIM, vec(8, 512))
    mqn_ref[...] = q.astype(BF16)
    mqr_ref[...] = _rope(q, cm, sm, MLA_ROPE_DIM // 4).astype(BF16)
    ckv = p[:, C_ML_CKV:C_ML_CKV + 128]
    ckv = (ckv * lax.rsqrt(jnp.mean(ckv * ckv, axis=-1, keepdims=True) + RMS_EPS) * vec(7, 128)).astype(BF16)
    k_rope = pltpu.roll(p[:, C_ML_ROPE:C_ML_ROPE + 128], MLA_NOPE_DIM, 1)
    k = _dot(ckv, wuk_ref[...]) + _tile_lanes(k_rope, 512)
    k = _segnorm(k, bd128, MLA_QK_DIM, vec(9, 512))
    mk_ref[...] = _rope(k, cm, sm, MLA_ROPE_DIM // 4).astype(BF16)
    mv_ref[...] = _dot(ckv, wuv_ref[...]).astype(BF16)


_PROJ_OUT_WIDTHS = (1024, 256, 256, 256, 256, 256, 128, 128, 256, 256, 256, 256, 512, 512, 512, 256)


def _proj_call(x, mod, g, w_in, vecs, bds, wuq, wuk, wuv, tables, layer, n_tiles, mod_idx, tab_idx):
    d = x.shape[1]
    tm = TOKEN_TILE
    const2 = lambda i: (0, 0)
    in_specs = [
        pl.BlockSpec((tm, d), lambda i: (i, 0)),
        pl.BlockSpec((None, N_MOD, d), lambda i: (mod_idx(i), 0, 0)),
        pl.BlockSpec((1, d), const2),
        pl.BlockSpec((None, d, IN_COLS_PAD), lambda i: (layer, 0, 0)),
        pl.BlockSpec((None, 16, 512), lambda i: (layer, 0, 0)),
        pl.BlockSpec((256, 256), const2),
        pl.BlockSpec((256, 256), const2),
        pl.BlockSpec((512, 512), const2),
        pl.BlockSpec((None, 256, 512), lambda i: (layer, 0, 0)),
        pl.BlockSpec((None, 128, 512), lambda i: (layer, 0, 0)),
        pl.BlockSpec((None, 128, 256), lambda i: (layer, 0, 0)),
    ] + [pl.BlockSpec((tm, LANES), lambda i: (tab_idx(i), 0))] * 6
    rows = n_tiles * tm
    return pl.pallas_call(
        _proj_kernel,
        grid=(n_tiles,),
        in_specs=in_specs,
        out_specs=[pl.BlockSpec((tm, w), lambda i: (i, 0)) for w in _PROJ_OUT_WIDTHS],
        out_shape=[jax.ShapeDtypeStruct((rows, w), BF16) for w in _PROJ_OUT_WIDTHS],
        compiler_params=_cparams(("arbitrary",)),
        name="mixer_projection",
    )(x, mod, g, w_in, vecs, *bds, wuq, wuk, wuv, *tables)


def _softmax_step(state, s, v_t, bounded):
    m, l, acc = state
    if bounded:
        p = jnp.exp2(s)
        return m, l + jnp.sum(p, axis=0, keepdims=True), acc + _dot(v_t, p.astype(BF16))
    m_new = jnp.maximum(m, jnp.max(s, axis=0, keepdims=True))
    alpha = jnp.exp2(m - m_new)
    p = jnp.exp2(s - m_new)
    l = alpha * l + jnp.sum(p, axis=0, keepdims=True)
    acc = alpha * acc + _dot(v_t, p.astype(BF16))
    return m_new, l, acc


def _init_state(tq):
    return (jnp.full((1, tq), NEG_BIG, F32), jnp.zeros((1, tq), F32), jnp.zeros((HEAD_DIM, tq), F32))


def _score_bound(g_q, g_k, dim):
    return 1.01 * dim * (dim ** -0.5 * LOG2E) * jnp.max(jnp.abs(g_q)) * jnp.max(jnp.abs(g_k))


def _transpose_bf16(eye, a):
    return _dot_nt(eye[0:a.shape[1], 0:a.shape[1]], a).astype(BF16)


def _q_slabs_t(eye, q_ref):
    out = []
    for j0 in range(0, q_ref.shape[1], 2 * LANES):
        w = min(2 * LANES, q_ref.shape[1] - j0)
        q_t = _dot_nt(eye[0:w, 0:w], q_ref[:, j0:j0 + w]).astype(BF16)
        out += [q_t[r:r + LANES, :] for r in range(0, w, LANES)]
    return out


def _keep_rows(q_t, lo, hi):
    if lo == 0 and hi == q_t.shape[0]:
        return q_t
    row = lax.broadcasted_iota(jnp.int32, q_t.shape, 0)
    return jnp.where((row >= lo) & (row < hi), q_t, jnp.zeros_like(q_t))


def _head_rows(v_t, head):
    return v_t[head * HEAD_DIM:(head + 1) * HEAD_DIM, :]


def _store_heads(eye, o_ref, o_t_heads):
    for p in range(0, len(o_t_heads), 4):
        o_t = jnp.concatenate(o_t_heads[p:p + 4], axis=0).astype(BF16)
        o_ref[:, p * HEAD_DIM:p * HEAD_DIM + o_t.shape[0]] = _dot_nt(eye, o_t).astype(o_ref.dtype)


def _stage_values_t(eye, v_ref, vt_scr, n_chunks, chunk):
    for c in range(n_chunks):
        for j in range(v_ref.shape[1] // LANES):
            vt_scr[c, j * LANES:(j + 1) * LANES, :] = _transpose_bf16(
                eye, v_ref[c * chunk:(c + 1) * chunk, j * LANES:(j + 1) * LANES])


def _joint_kernel(*refs, subs, n_lat_tiles, seq, diff_lam_init, bounded):
    if diff_lam_init is None:
        qr_ref, qn_ref, kl_ref, kc_ref, vl_ref, vc_ref, eye_ref, o_ref, vt_scr, vct_scr = refs
    else:
        qr_ref, qn_ref, kl_ref, kc_ref, vl_ref, vc_ref, eye_ref, lam_ref, gsub_ref, o_ref, vt_scr, vct_scr = refs
    t = pl.program_id(2)
    tq = o_ref.shape[0]
    n_chunks = seq // KEY_CHUNK
    eye = eye_ref[...]
    init = tuple(_init_state(tq) for _ in subs)

    @pl.when(t == 0)
    def _():
        _stage_values_t(eye, vl_ref, vt_scr, n_chunks, KEY_CHUNK)
        _stage_values_t(eye, vc_ref, vct_scr, 1, vc_ref.shape[0])

    def slab(ref, rows, j):
        return ref[rows, j * LANES:(j + 1) * LANES]

    qn_t, qr_t = _q_slabs_t(eye, qn_ref), _q_slabs_t(eye, qr_ref)
    ctx_scores = [_dot(slab(kc_ref, slice(None), ks), _keep_rows(qn_t[qs_], lo, hi))
                  for (qs_, lo, hi, ks, _) in subs]

    def consume(states, scores, v_t):
        return tuple(_softmax_step(states[i], scores[i], _head_rows(v_t, vh), bounded)
                     for i, (_, _, _, _, vh) in enumerate(subs))

    def latent_part(states):
        q_t = [_keep_rows(qr_t[qs_], lo, hi) for (qs_, lo, hi, _, _) in subs]

        def scores_at(c):
            rows = pl.ds(pl.multiple_of(c * KEY_CHUNK, KEY_CHUNK), KEY_CHUNK)
            return tuple(_dot(slab(kl_ref, rows, ks), q_t[i]) for i, (_, _, _, ks, _) in enumerate(subs))

        return lax.fori_loop(0, n_chunks, lambda c, st: consume(st, scores_at(c), vt_scr[c]), states)

    states = lax.cond(t < n_lat_tiles, latent_part, lambda st: st, init)
    states = consume(states, ctx_scores, vct_scr[0])
    outs = [acc / l for (_, l, acc) in states]

    if diff_lam_init is None:
        _store_heads(eye, o_ref, outs)
    else:
        lv = lam_ref[...]
        lam = (jnp.exp(jnp.sum(lv[0:1, :] * lv[1:2, :], axis=-1, keepdims=True))
               - jnp.exp(jnp.sum(lv[2:3, :] * lv[3:4, :], axis=-1, keepdims=True)) + diff_lam_init)
        heads = []
        for o1, o2 in zip(outs[0::2], outs[1::2]):
            d = o1 - lam * o2
            ms = jnp.mean(d * d, axis=0, keepdims=True)
            heads.append((d * lax.rsqrt(ms + RMS_EPS) * gsub_ref[...]) * (1.0 - diff_lam_init))
        _store_heads(eye, o_ref, heads)


def _joint_call(operands, out_rows, *, subs, n_blocks, batch, seq, ctx_len, with_ctx_queries, name, bounded,
                diff_lam_init=None):
    qr, qn, k, v = operands[:4]
    tq = Q_TILE
    n_lat = seq // tq
    nqt = n_lat + (1 if with_ctx_queries else 0)
    ctx_row0 = batch * seq // ctx_len
    q_width, k_width, v_width = qr.shape[1] // n_blocks, k.shape[1] // n_blocks, v.shape[1] // n_blocks
    out_width = BRANCH_WIDTH // n_blocks
    n_chunks = seq // KEY_CHUNK

    def qrow(b, t):
        return jnp.where(t < n_lat, b * n_lat + t, batch * n_lat + b)

    in_specs = [
        pl.BlockSpec((tq, q_width), lambda b, blk, t: (qrow(b, t), blk)),
        pl.BlockSpec((tq, q_width), lambda b, blk, t: (qrow(b, t), blk)),
        pl.BlockSpec((seq, k_width), lambda b, blk, t: (b, blk)),
        pl.BlockSpec((ctx_len, k_width), lambda b, blk, t: (ctx_row0 + b, blk)),
        pl.BlockSpec((seq, v_width), lambda b, blk, t: (b, blk)),
        pl.BlockSpec((ctx_len, v_width), lambda b, blk, t: (ctx_row0 + b, blk)),
        pl.BlockSpec((tq, tq), lambda b, blk, t: (0, 0)),
    ]
    args = [qr, qn, k, k, v, v, jnp.eye(tq, dtype=BF16)]
    if diff_lam_init is not None:
        in_specs += [pl.BlockSpec((4, DIFF_QK_DIM), lambda b, blk, t: (0, 0)),
                     pl.BlockSpec((HEAD_DIM, tq), lambda b, blk, t: (0, 0))]
        args += list(operands[4:])
    return pl.pallas_call(
        functools.partial(_joint_kernel, subs=subs, n_lat_tiles=n_lat, seq=seq, diff_lam_init=diff_lam_init,
                          bounded=bounded),
        grid=(batch, n_blocks, nqt),
        in_specs=in_specs,
        out_specs=pl.BlockSpec((tq, out_width), lambda b, blk, t: (qrow(b, t), blk)),
        out_shape=jax.ShapeDtypeStruct((out_rows, BRANCH_WIDTH), BF16),
        scratch_shapes=[pltpu.VMEM((n_chunks, v_width, KEY_CHUNK), BF16), pltpu.VMEM((1, v_width, ctx_len), BF16)],
        compiler_params=_cparams(("arbitrary", "arbitrary", "arbitrary")),
        name=name,
    )(*args)


def _na_kernel(q_ref, kl_ref, kc_ref, vl_ref, vc_ref, ab_ref, eye_ref, o_ref, vt_scr, vct_scr, *, n_lat_tiles):
    t = pl.program_id(2)
    tq = o_ref.shape[0]
    eye = eye_ref[...]
    rows_per_tile = tq // GRID_W
    band_tiles = NA_KEY_ROWS // rows_per_tile
    heads = range(BRANCH_HEADS)

    @pl.when(t == 0)
    def _():
        _stage_values_t(eye, vl_ref, vt_scr, n_lat_tiles, tq)
        _stage_values_t(eye, vc_ref, vct_scr, 1, vc_ref.shape[0])

    def attend(blocks):
        q_slabs = _q_slabs_t(eye, q_ref)
        q_t = [_keep_rows(q_slabs[h // 2], (h % 2) * HEAD_DIM, (h % 2 + 1) * HEAD_DIM) for h in heads]
        scores = []
        for h in heads:
            per_block = []
            for k_ref, rows, _, bias in blocks:
                s = _dot(k_ref[rows, (h // 2) * LANES:(h // 2 + 1) * LANES], q_t[h])
                per_block.append(s if bias is None else s + bias[h])
            scores.append(per_block)
        outs = []
        for h in heads:
            m = functools.reduce(jnp.maximum, [jnp.max(s, axis=0, keepdims=True) for s in scores[h]])
            l, acc = 0.0, 0.0
            for s, (_, _, v_t, _) in zip(scores[h], blocks):
                p = jnp.exp2(s - m)
                l = l + jnp.sum(p, axis=0, keepdims=True)
                acc = acc + _dot(_head_rows(v_t, h), p.astype(BF16))
            outs.append(acc / l)
        _store_heads(eye, o_ref, outs)

    ctx_block = (kc_ref, slice(None), vct_scr[0], None)

    @pl.when(t < n_lat_tiles)
    def _():
        tile0 = jnp.clip(t - NA_WIN_H // 2 // rows_per_tile, 0, n_lat_tiles - band_tiles)
        rows = pl.ds(pl.multiple_of(tile0 * tq, tq), band_tiles * tq)
        v_t = jnp.concatenate([vt_scr[tile0 + j] for j in range(band_tiles)], axis=1)
        attend([(kl_ref, rows, v_t, ab_ref), ctx_block])

    @pl.when(t >= n_lat_tiles)
    def _():
        attend([ctx_block])


def _na_call(q, k, v, ab, out_rows, *, batch, seq, ctx_len, with_ctx_queries):
    tq = Q_TILE
    n_lat = seq // tq
    nqt = n_lat + (1 if with_ctx_queries else 0)
    ctx_row0 = batch * seq // ctx_len
    n_keys = NA_KEY_ROWS * GRID_W
    width = BRANCH_WIDTH

    def qrow(b, t):
        return jnp.where(t < n_lat, b * n_lat + t, batch * n_lat + b)

    def pattern(t):
        return jnp.where(t == 0, 0, jnp.where(t >= n_lat - 1, 2, 1))

    return pl.pallas_call(
        functools.partial(_na_kernel, n_lat_tiles=n_lat),
        grid=(batch, 1, nqt),
        in_specs=[
            pl.BlockSpec((tq, width), lambda b, blk, t: (qrow(b, t), 0)),
            pl.BlockSpec((seq, width), lambda b, blk, t: (b, 0)),
            pl.BlockSpec((ctx_len, width), lambda b, blk, t: (ctx_row0 + b, 0)),
            pl.BlockSpec((seq, width), lambda b, blk, t: (b, 0)),
            pl.BlockSpec((ctx_len, width), lambda b, blk, t: (ctx_row0 + b, 0)),
            pl.BlockSpec((None, BRANCH_HEADS, n_keys, tq), lambda b, blk, t: (pattern(t), 0, 0, 0)),
            pl.BlockSpec((tq, tq), lambda b, blk, t: (0, 0)),
        ],
        out_specs=pl.BlockSpec((tq, width), lambda b, blk, t: (qrow(b, t), 0)),
        out_shape=jax.ShapeDtypeStruct((out_rows, width), BF16),
        scratch_shapes=[pltpu.VMEM((n_lat, width, tq), BF16), pltpu.VMEM((1, width, ctx_len), BF16)],
        compiler_params=_cparams(("arbitrary", "arbitrary", "arbitrary")),
        name="neighbourhood_attention",
    )(q, k, k, v, v, ab, jnp.eye(tq, dtype=BF16))


def _merge_kernel(x_ref, mod_ref, n_ref, o0_ref, o1_ref, o2_ref, o3_ref, wg_ref, bg_ref, wb_ref, wo_ref,
                  out_ref, y_scr):
    n = n_ref[...]
    branch = [o_ref[...] for o_ref in (o0_ref, o1_ref, o2_ref, o3_ref)]
    for c0 in range(0, n.shape[1], MERGE_SLAB):
        cols = slice(c0, c0 + MERGE_SLAB)
        y = None
        for j in range(N_BRANCH):
            gate = jax.nn.sigmoid(_dot(n, wg_ref[j, :, cols]) + bg_ref[j, :, cols])
            term = gate * _dot(branch[j], wb_ref[j, :, cols])
            y = term if y is None else y + term
        y_scr[:, cols] = y.astype(BF16)
    out_ref[...] = x_ref[...] + mod_ref[5:6, :] * _dot(y_scr[...], wo_ref[...])


def _merge_call(x, mod, n, outs, w_gate, b_gate, w_branch, w_out, layer, n_tiles, mod_idx):
    d = x.shape[1]
    tm = TOKEN_TILE
    row = lambda i: (i, 0)
    return pl.pallas_call(
        _merge_kernel,
        grid=(n_tiles,),
        in_specs=[
            pl.BlockSpec((tm, d), row),
            pl.BlockSpec((None, N_MOD, d), lambda i: (mod_idx(i), 0, 0)),
            pl.BlockSpec((tm, d), row),
        ] + [pl.BlockSpec((tm, BRANCH_WIDTH), row)] * 4 + [
            _resident((None, N_BRANCH, d, d), lambda i: (layer, 0, 0, 0)),
            _resident((None, N_BRANCH, 1, d), lambda i: (layer, 0, 0, 0)),
            _resident((None, N_BRANCH, BRANCH_WIDTH, d), lambda i: (layer, 0, 0, 0)),
            _resident((None, d, d), lambda i: (layer, 0, 0)),
        ],
        out_specs=pl.BlockSpec((tm, d), row),
        out_shape=jax.ShapeDtypeStruct((n_tiles * tm, d), F32),
        scratch_shapes=[pltpu.VMEM((tm, d), BF16)],
        compiler_params=_cparams(("arbitrary",)),
        name="branch_merge",
    )(x, mod, n, *outs, w_gate, b_gate, w_branch, w_out)


def _rope_tables(seq, rot_dim, pad_rows):
    t = jnp.arange(seq, dtype=jnp.int32)
    row = (t // GRID_W).astype(F32)
    col = (t % GRID_W).astype(F32)
    n_freq = rot_dim // 4
    inv = ROPE_THETA ** (-jnp.arange(n_freq, dtype=F32) / n_freq)
    ar = row[:, None] * inv[None, :]
    ac = col[:, None] * inv[None, :]
    ang = jnp.concatenate([ar, ar, ac, ac], axis=-1)
    sign = jnp.tile(jnp.concatenate([-jnp.ones((n_freq,), F32), jnp.ones((n_freq,), F32)]), 2)
    cos = jnp.concatenate([jnp.cos(ang), jnp.ones((pad_rows, rot_dim), F32)], axis=0)
    sin = jnp.concatenate([jnp.sin(ang) * sign[None, :], jnp.zeros((pad_rows, rot_dim), F32)], axis=0)
    return cos, sin


def _all_rope_tables(seq, pad_rows):
    c64, s64 = _rope_tables(seq, HEAD_DIM, pad_rows)
    c32, s32 = _rope_tables(seq, DIFF_QK_DIM, pad_rows)
    rows = seq + pad_rows
    ones = lambda w: jnp.ones((rows, w), F32)
    zeros = lambda w: jnp.zeros((rows, w), F32)
    pad = LANES - MLA_QK_DIM
    cm = jnp.concatenate([ones(MLA_NOPE_DIM), c32, ones(pad)], axis=-1)
    sm = jnp.concatenate([zeros(MLA_NOPE_DIM), s32, zeros(pad)], axis=-1)
    return (jnp.tile(c64, (1, 2)), jnp.tile(s64, (1, 2)), jnp.tile(c32, (1, 4)), jnp.tile(s32, (1, 4)), cm, sm)


def _block_diag_ones(width, seg):
    idx = np.arange(width) // seg
    return jnp.asarray(idx[:, None] == idx[None, :], dtype=BF16)


def _na_bias_tables(rpb, grid_rows):
    rows_per_tile = Q_TILE // GRID_W
    wh = min(NA_WIN_H, grid_rows)
    kinds = ((0, 0), (rows_per_tile, 0), (grid_rows - rows_per_tile, grid_rows - NA_KEY_ROWS))
    col = np.arange(GRID_W)
    c0 = np.clip(col - NA_WIN_W // 2, 0, GRID_W - NA_WIN_W)
    col_ok = (col[:, None] >= c0[None, :]) & (col[:, None] < c0[None, :] + NA_WIN_W)
    d_col = np.clip(col[:, None] - col[None, :], -(NA_WIN_W - 1), NA_WIN_W - 1) + (NA_WIN_W - 1)
    col_hot = (d_col[..., None] == np.arange(2 * NA_WIN_W - 1)).astype(np.float32)
    row_hot, valid = [], []
    for q_row0, k_row0 in kinds:
        r = q_row0 + np.arange(rows_per_tile)
        kr = k_row0 + np.arange(NA_KEY_ROWS)
        r0 = np.clip(r - wh // 2, 0, grid_rows - wh)
        row_ok = (kr[:, None] >= r0[None, :]) & (kr[:, None] < r0[None, :] + wh)
        d_row = kr[:, None] - r[None, :] + (NA_WIN_H - 1)
        row_hot.append((d_row[..., None] == np.arange(2 * NA_WIN_H - 1)).astype(np.float32))
        valid.append(row_ok[:, None, :, None] & col_ok[None, :, None, :])
    bias = jnp.einsum("nkqa,lhab,dcb->lnhkdqc", jnp.asarray(np.stack(row_hot)), rpb.astype(F32),
                      jnp.asarray(col_hot), precision=lax.Precision.HIGHEST)
    ab = jnp.where(jnp.asarray(np.stack(valid))[None, :, None], bias * LOG2E, NEG_BIG)
    depth = rpb.shape[0]
    return ab.reshape(depth, 3, BRANCH_HEADS, NA_KEY_ROWS * GRID_W, Q_TILE)


def _pad_heads(w, head_w, to_w):
    lead = w.shape[:-1]
    w = w.reshape(lead + (BRANCH_HEADS, head_w))
    w = jnp.pad(w, [(0, 0)] * len(lead) + [(0, 0), (0, to_w - head_w)])
    return w.reshape(lead + (BRANCH_HEADS * to_w,))


def _gqa_head_order(a, axis):
    parts = jnp.split(a, BRANCH_HEADS, axis=axis)
    return jnp.concatenate([parts[0], parts[2], parts[1], parts[3]], axis=axis)


def kernel(x, c, ctx, c_ctx, w_ada, b_ada, norm_g, ffn_w1, ffn_w3, ffn_w2, w_in, na_g_q, na_g_k, na_rpb, gqa_g_q, gqa_g_k, diff_g_q, diff_g_k, diff_lambda, diff_g_sub, mla_g_cq, mla_g_ckv, mla_w_uq, mla_w_ukv, mla_g_q, mla_g_k, w_gate, b_gate, w_branch, w_out):
    batch, seq, d = x.shape
    ctx_len = ctx.shape[1]
    depth = w_ada.shape[0]
    tm = TOKEN_TILE
    assert d == D_MODEL and ctx_len == Q_TILE and seq % tm == 0 and (batch * ctx_len) % tm == 0
    assert seq % KEY_CHUNK == 0 and seq // GRID_W >= NA_KEY_ROWS and batch < 16
    n_lat_rows = batch * seq
    lat_tiles = n_lat_rows // tm
    all_tiles = lat_tiles + batch * ctx_len // tm
    tiles_per_batch = seq // tm

    def mod_idx(i):
        return jnp.where(i < lat_tiles, i // tiles_per_batch, batch)

    def tab_idx(i):
        return jnp.where(i < lat_tiles, i % tiles_per_batch, tiles_per_batch)

    c16 = jnp.concatenate([c, c_ctx[None, :], jnp.zeros((16 - batch - 1, d), F32)], axis=0)
    mods = _mod_call(c16, w_ada, b_ada).reshape(depth, 16, N_MOD, d)
    w1, w3, w2 = ffn_w1.astype(BF16), ffn_w3.astype(BF16), ffn_w2.astype(BF16)
    na_w, gq_w, df_w, ml_w = jnp.split(w_in, [768, 1280, 2048], axis=-1)
    w_in_p = jnp.concatenate([
        na_w, _gqa_head_order(gq_w[..., :256], -1), gq_w[..., 256:], df_w, ml_w,
        jnp.zeros((depth, d, IN_COLS_PAD - w_in.shape[-1]), F32)], axis=-1).astype(BF16)
    wuq = _pad_heads(mla_w_uq, MLA_QK_DIM, LANES).astype(BF16)
    wukv = mla_w_ukv.reshape(depth, MLA_KV_RANK, BRANCH_HEADS, MLA_NOPE_DIM + HEAD_DIM)
    wuk = _pad_heads(wukv[..., :MLA_NOPE_DIM].reshape(depth, MLA_KV_RANK, -1), MLA_NOPE_DIM, LANES).astype(BF16)
    wuv = wukv[..., MLA_NOPE_DIM:].reshape(depth, MLA_KV_RANK, BRANCH_WIDTH).astype(BF16)

    def vec_row(v, reps=1, scale=1.0):
        v = jnp.tile(v, (1, reps)) * scale
        return jnp.pad(v, ((0, 0), (0, 512 - v.shape[-1])))

    pad_g = lambda g: jnp.pad(g, ((0, 0), (0, LANES - MLA_QK_DIM)))
    vec_rows = [
        vec_row(na_g_q, 4, HEAD_DIM ** -0.5 * LOG2E), vec_row(na_g_k, 4),
        vec_row(gqa_g_q, 4, HEAD_DIM ** -0.5 * LOG2E), vec_row(gqa_g_k, 2),
        vec_row(diff_g_q, 8, DIFF_QK_DIM ** -0.5 * LOG2E), vec_row(diff_g_k, 8),
        vec_row(mla_g_cq), vec_row(mla_g_ckv),
        vec_row(pad_g(mla_g_q), 4, MLA_QK_DIM ** -0.5 * LOG2E), vec_row(pad_g(mla_g_k), 4),
    ]
    vecs = jnp.stack(vec_rows + [jnp.zeros_like(vec_rows[0])] * (16 - len(vec_rows)), axis=1)
    bds = (_block_diag_ones(256, HEAD_DIM), _block_diag_ones(256, DIFF_QK_DIM), _block_diag_ones(512, LANES))
    tables = _all_rope_tables(seq, tm)
    na_ab = _na_bias_tables(na_rpb, seq // GRID_W)
    gsub = jnp.broadcast_to(diff_g_sub[:, :, None], (depth, HEAD_DIM, Q_TILE))
    wg, wo = w_gate.astype(BF16), w_out.astype(BF16)
    wb = jnp.concatenate([w_branch[:, 0:1], _gqa_head_order(w_branch[:, 1:2], 2), w_branch[:, 2:]], axis=1).astype(BF16)
    bg = b_gate.reshape(depth, N_BRANCH, 1, d)

    xa = jnp.concatenate([x.reshape(n_lat_rows, d), ctx.reshape(batch * ctx_len, d)], axis=0)

    gqa_subs = ((0, 0, 64, 0, 0), (0, 64, 128, 0, 1), (1, 0, 64, 0, 0), (1, 64, 128, 0, 1))
    diff_subs = ((0, 0, 32, 0, 0), (0, 32, 64, 0, 0), (0, 64, 96, 0, 1), (0, 96, 128, 0, 1))
    mla_subs = tuple((h, 0, LANES, h, h) for h in range(BRANCH_HEADS))

    for i in range(depth):
        need_ctx = i < depth - 1
        mod = mods[i]
        g = norm_g[i].reshape(3, 1, d)
        xa = _ffn_call(xa, mod, g[0], w1, w3, w2, i, 0, 0, all_tiles, mod_idx)
        (n, naq, nak, nav, gqr, gqn, gk, gv, dqr, dqn, dk, dv, mqr, mqn, mk, mv) = _proj_call(
            xa, mod, g[1], w_in_p, vecs, bds, wuq, wuk, wuv, tables, i, all_tiles, mod_idx, tab_idx)
        out_rows = xa.shape[0] if need_ctx else n_lat_rows
        common = dict(batch=batch, seq=seq, ctx_len=ctx_len, with_ctx_queries=need_ctx)
        lam_init = 0.8 - 0.6 * math.exp(-0.3 * i)
        def joint(bound, *operands, **kw):
            return lax.cond(bound <= SCORE_BOUND,
                            lambda *a: _joint_call(a, out_rows, bounded=True, **kw, **common),
                            lambda *a: _joint_call(a, out_rows, bounded=False, **kw, **common), *operands)

        outs = [
            _na_call(naq, nak, nav, na_ab[i], out_rows, **common),
            joint(_score_bound(gqa_g_q[i], gqa_g_k[i], HEAD_DIM), gqr, gqn, gk, gv,
                  subs=gqa_subs, n_blocks=1, name="gqa_attention"),
            joint(_score_bound(diff_g_q[i], diff_g_k[i], DIFF_QK_DIM), dqr, dqn, dk, dv, diff_lambda[i], gsub[i],
                  subs=diff_subs, n_blocks=2, name="diff_attention", diff_lam_init=lam_init),
            joint(_score_bound(mla_g_q[i], mla_g_k[i], MLA_QK_DIM), mqr, mqn, mk, mv,
                  subs=mla_subs, n_blocks=1, name="mla_attention"),
        ]
        n_tiles = all_tiles if need_ctx else lat_tiles
        xa = _merge_call(xa, mod, n, outs, wg, bg, wb, wo, i, n_tiles, mod_idx)
        xa = _ffn_call(xa, mod, g[2], w1, w3, w2, i, 1, 6, n_tiles, mod_idx)
    return xa.reshape(batch, seq, d)
```

```python
import functools
import math

import jax
import jax.numpy as jnp
import numpy as np
from jax import lax
from jax.experimental import pallas as pl
from jax.experimental.pallas import tpu as pltpu

F32 = jnp.float32
BF16 = jnp.bfloat16

D_MODEL = 1024
GRID_W = 64
ROPE_THETA = 10000.0
RMS_EPS = 1e-6
HEAD_DIM = 64
N_BRANCH = 4
BRANCH_HEADS = 4
BRANCH_WIDTH = 256
NA_WIN_H = 8
NA_WIN_W = 16
DIFF_QK_DIM = 32
MLA_Q_RANK = 256
MLA_KV_RANK = 128
MLA_NOPE_DIM = 64
MLA_ROPE_DIM = 32
MLA_QK_DIM = 96
D_FF = 2816
N_MOD = 9
LOG2E = 1.4426950408889634

LANES = 128
TOKEN_TILE = 512
Q_TILE = 256
KEY_CHUNK = 4096
FF_SLAB = 256
MERGE_SLAB = 256
NA_KEY_ROWS = 12
NEG_BIG = -1e30
SCORE_BOUND = 40.0
VMEM_LIMIT = 56 * 1024 * 1024

C_NA_Q, C_NA_K, C_NA_V = 0, 256, 512
C_GQ_Q, C_GQ_K, C_GQ_V = 768, 1024, 1152
C_DF_Q, C_DF_K, C_DF_V = 1280, 1536, 1792
C_ML_CQ, C_ML_CKV, C_ML_ROPE = 2048, 2304, 2432
IN_COLS_PAD = 2560


def _cparams(sem):
    return pltpu.CompilerParams(dimension_semantics=sem, vmem_limit_bytes=VMEM_LIMIT)


def _dot(a, b):
    return jnp.dot(a, b, preferred_element_type=F32)


def _dot_nt(a, b):
    return lax.dot_general(a, b, (((1,), (1,)), ((), ())), preferred_element_type=F32)


def _silu(a):
    return a * jax.nn.sigmoid(a)


def _modulated_norm(x, g, shift, scale):
    ms = jnp.mean(x * x, axis=-1, keepdims=True)
    return (x * lax.rsqrt(ms + RMS_EPS) * g) * (1.0 + scale) + shift


def _mod_kernel(c_ref, w_ref, b_ref, o_ref):
    c = c_ref[...]
    o_ref[...] = _dot(_silu(c).astype(BF16), w_ref[...].astype(BF16)) + b_ref[...]


def _mod_call(c16, w_ada, b_ada):
    depth, d, nd = w_ada.shape
    tn = 1024
    return pl.pallas_call(
        _mod_kernel,
        grid=(depth, nd // tn),
        in_specs=[
            pl.BlockSpec((16, d), lambda l, j: (0, 0)),
            pl.BlockSpec((None, d, tn), lambda l, j: (l, 0, j)),
            pl.BlockSpec((None, 1, tn), lambda l, j: (l, 0, j)),
        ],
        out_specs=pl.BlockSpec((None, 16, tn), lambda l, j: (l, 0, j)),
        out_shape=jax.ShapeDtypeStruct((depth, 16, nd), F32),
        compiler_params=_cparams(("arbitrary", "arbitrary")),
        name="adaln_mod",
    )(c16, w_ada, b_ada.reshape(depth, 1, nd))


def _ffn_kernel(x_ref, mod_ref, g_ref, w1_ref, w3_ref, w2_ref, o_ref, act_scr, *, row0):
    x = x_ref[...]
    h = _modulated_norm(x, g_ref[...], mod_ref[row0:row0 + 1, :], mod_ref[row0 + 1:row0 + 2, :]).astype(BF16)
    for f0 in range(0, D_FF, FF_SLAB):
        a = _dot(h, w1_ref[:, f0:f0 + FF_SLAB])
        b = _dot(h, w3_ref[:, f0:f0 + FF_SLAB])
        act_scr[:, f0:f0 + FF_SLAB] = (_silu(a) * b).astype(BF16)
    o_ref[...] = x + (0.5 * mod_ref[row0 + 2:row0 + 3, :]) * _dot(act_scr[...], w2_ref[...])


def _resident(block_shape, index_map):
    return pl.BlockSpec(block_shape, index_map, pipeline_mode=pl.Buffered(1))


def _ffn_call(x, mod, g, w1, w3, w2, layer, half, row0, n_tiles, mod_idx):
    d = x.shape[1]
    tm = TOKEN_TILE
    return pl.pallas_call(
        functools.partial(_ffn_kernel, row0=row0),
        grid=(n_tiles,),
        in_specs=[
            pl.BlockSpec((tm, d), lambda i: (i, 0)),
            pl.BlockSpec((None, N_MOD, d), lambda i: (mod_idx(i), 0, 0)),
            _resident((1, d), lambda i: (0, 0)),
            _resident((None, None, d, D_FF), lambda i: (layer, half, 0, 0)),
            _resident((None, None, d, D_FF), lambda i: (layer, half, 0, 0)),
            _resident((None, None, D_FF, d), lambda i: (layer, half, 0, 0)),
        ],
        out_specs=pl.BlockSpec((tm, d), lambda i: (i, 0)),
        out_shape=jax.ShapeDtypeStruct((n_tiles * tm, d), F32),
        scratch_shapes=[pltpu.VMEM((tm, D_FF), BF16)],
        compiler_params=_cparams(("arbitrary",)),
        name="ffn_half_step",
    )(x, mod, g, w1, w3, w2)


def _segnorm(x, bd, seg_len, gain):
    x2 = x * x
    hi = x2.astype(BF16)
    lo = (x2 - hi.astype(F32)).astype(BF16)
    groups = []
    for c0 in range(0, x.shape[1], bd.shape[0]):
        w = min(bd.shape[0], x.shape[1] - c0)
        groups.append(_dot(hi[:, c0:c0 + w], bd[0:w, 0:w]) + _dot(lo[:, c0:c0 + w], bd[0:w, 0:w]))
    ss = groups[0] if len(groups) == 1 else jnp.concatenate(groups, axis=-1)
    return x * lax.rsqrt(ss * (1.0 / seg_len) + RMS_EPS) * gain


def _tile_lanes(t, width):
    reps = width // t.shape[-1]
    return t if reps == 1 else jnp.concatenate([t] * reps, axis=-1)


def _rope(x, cos, sin_signed, quarter):
    width = x.shape[-1]
    lane = lax.broadcasted_iota(jnp.int32, x.shape, 1)
    first = (lane % (2 * quarter)) < quarter
    rot = jnp.where(first, pltpu.roll(x, width - quarter, 1), pltpu.roll(x, quarter, 1))
    return x * _tile_lanes(cos, width) + rot * _tile_lanes(sin_signed, width)


def _proj_kernel(x_ref, mod_ref, g_ref, w_ref, vec_ref, bd64_ref, bd32_ref, bd128_ref,
                 wuq_ref, wuk_ref, wuv_ref, c64_ref, s64_ref, c32_ref, s32_ref, cm_ref, sm_ref,
                 n_ref, naq_ref, nak_ref, nav_ref,
                 gqr_ref, gqn_ref, gk_ref, gv_ref,
                 dqr_ref, dqn_ref, dk_ref, dv_ref,
                 mqr_ref, mqn_ref, mk_ref, mv_ref):
    n = _modulated_norm(x_ref[...], g_ref[...], mod_ref[3:4, :], mod_ref[4:5, :]).astype(BF16)
    n_ref[...] = n
    p = _dot(n, w_ref[...])
    bd64, bd32, bd128 = bd64_ref[...], bd32_ref[...], bd128_ref[...]
    c64, s64 = c64_ref[...], s64_ref[...]
    c32, s32 = c32_ref[...], s32_ref[...]
    cm, sm = cm_ref[...], sm_ref[...]

    def vec(row, width):
        return vec_ref[row:row + 1, 0:width]

    naq_ref[...] = _segnorm(p[:, C_NA_Q:C_NA_Q + 256], bd64, HEAD_DIM, vec(0, 256)).astype(BF16)
    nak_ref[...] = _segnorm(p[:, C_NA_K:C_NA_K + 256], bd64, HEAD_DIM, vec(1, 256)).astype(BF16)
    nav_ref[...] = p[:, C_NA_V:C_NA_V + 256].astype(BF16)

    q = _segnorm(p[:, C_GQ_Q:C_GQ_Q + 256], bd64, HEAD_DIM, vec(2, 256))
    gqn_ref[...] = q.astype(BF16)
    gqr_ref[...] = _rope(q, c64, s64, HEAD_DIM // 4).astype(BF16)
    k = _segnorm(p[:, C_GQ_K:C_GQ_K + 128], bd64, HEAD_DIM, vec(3, 128))
    gk_ref[...] = _rope(k, c64, s64, HEAD_DIM // 4).astype(BF16)
    gv_ref[...] = p[:, C_GQ_V:C_GQ_V + 128].astype(BF16)

    q = _segnorm(p[:, C_DF_Q:C_DF_Q + 256], bd32, DIFF_QK_DIM, vec(4, 256))
    dqn_ref[...] = q.astype(BF16)
    dqr_ref[...] = _rope(q, c32, s32, DIFF_QK_DIM // 4).astype(BF16)
    k = _segnorm(p[:, C_DF_K:C_DF_K + 256], bd32, DIFF_QK_DIM, vec(5, 256))
    dk_ref[...] = _rope(k, c32, s32, DIFF_QK_DIM // 4).astype(BF16)
    dv_ref[...] = p[:, C_DF_V:C_DF_V + 256].astype(BF16)

    cq = p[:, C_ML_CQ:C_ML_CQ + 256]
    cq = cq * lax.rsqrt(jnp.mean(cq * cq, axis=-1, keepdims=True) + RMS_EPS) * vec(6, 256)
    q = _segnorm(_dot(cq.astype(BF16), wuq_ref[...]), bd128, MLA_QK_DIM, vec(8, 512))
    mqn_ref[...] = q.astype(BF16)
    mqr_ref[...] = _rope(q, cm, sm, MLA_ROPE_DIM // 4).astype(BF16)
    ckv = p[:, C_ML_CKV:C_ML_CKV + 128]
    ckv = (ckv * lax.rsqrt(jnp.mean(ckv * ckv, axis=-1, keepdims=True) + RMS_EPS) * vec(7, 128)).astype(BF16)
    k_rope = pltpu.roll(p[:, C_ML_ROPE:C_ML_ROPE + 128], MLA_NOPE_DIM, 1)
    k = _dot(ckv, wuk_ref[...]) + _tile_lanes(k_rope, 512)
    k = _segnorm(k, bd128, MLA_QK_DIM, vec(9, 512))
    mk_ref[...] = _rope(k, cm, sm, MLA_ROPE_DIM // 4).astype(BF16)
    mv_ref[...] = _dot(ckv, wuv_ref[...]).astype(BF16)


_PROJ_OUT_WIDTHS = (1024, 256, 256, 256, 256, 256, 128, 128, 256, 256, 256, 256, 512, 512, 512, 256)


def _proj_call(x, mod, g, w_in, vecs, bds, wuq, wuk, wuv, tables, layer, n_tiles, mod_idx, tab_idx):
    d = x.shape[1]
    tm = TOKEN_TILE
    const2 = lambda i: (0, 0)
    in_specs = [
        pl.BlockSpec((tm, d), lambda i: (i, 0)),
        pl.BlockSpec((None, N_MOD, d), lambda i: (mod_idx(i), 0, 0)),
        pl.BlockSpec((1, d), const2),
        pl.BlockSpec((None, d, IN_COLS_PAD), lambda i: (layer, 0, 0)),
        pl.BlockSpec((None, 16, 512), lambda i: (layer, 0, 0)),
        pl.BlockSpec((256, 256), const2),
        pl.BlockSpec((256, 256), const2),
        pl.BlockSpec((256, 256), const2),
        pl.BlockSpec((None, 256, 512), lambda i: (layer, 0, 0)),
        pl.BlockSpec((None, 128, 512), lambda i: (layer, 0, 0)),
        pl.BlockSpec((None, 128, 256), lambda i: (layer, 0, 0)),
    ] + [pl.BlockSpec((tm, LANES), lambda i: (tab_idx(i), 0))] * 6
    rows = n_tiles * tm
    return pl.pallas_call(
        _proj_kernel,
        grid=(n_tiles,),
        in_specs=in_specs,
        out_specs=[pl.BlockSpec((tm, w), lambda i: (i, 0)) for w in _PROJ_OUT_WIDTHS],
        out_shape=[jax.ShapeDtypeStruct((rows, w), BF16) for w in _PROJ_OUT_WIDTHS],
        compiler_params=_cparams(("arbitrary",)),
        name="mixer_projection",
    )(x, mod, g, w_in, vecs, *bds, wuq, wuk, wuv, *tables)


def _attend(blocks, subs, bounded):
    scores = []
    for i, sub in enumerate(subs):
        ks = sub[3]
        per_block = []
        for k_ref, rows, _, q_t, bias in blocks:
            s = _dot(k_ref[rows, ks * LANES:(ks + 1) * LANES], q_t[i])
            per_block.append(s if bias is None else s + bias[i])
        scores.append(per_block)
    outs = []
    for i, sub in enumerate(subs):
        if not bounded:
            m = functools.reduce(jnp.maximum, [jnp.max(s, axis=0, keepdims=True) for s in scores[i]])
        l = acc = None
        for s, (_, _, v_t, _, _) in zip(scores[i], blocks):
            p = jnp.exp2(s if bounded else s - m)
            l_blk = jnp.sum(p, axis=0, keepdims=True)
            acc_blk = _dot(_head_rows(v_t, sub[4]), p.astype(BF16))
            l, acc = (l_blk, acc_blk) if l is None else (l + l_blk, acc + acc_blk)
        outs.append(acc / l)
    return outs


def _score_bound(g_q, g_k, dim):
    return 1.01 * dim * (dim ** -0.5 * LOG2E) * jnp.max(jnp.abs(g_q)) * jnp.max(jnp.abs(g_k))


def _transpose_bf16(eye, a):
    return _dot_nt(eye[0:a.shape[1], 0:a.shape[1]], a).astype(BF16)


def _q_slabs_t(eye, q_ref):
    out = []
    for j0 in range(0, q_ref.shape[1], 2 * LANES):
        w = min(2 * LANES, q_ref.shape[1] - j0)
        q_t = _dot_nt(eye[0:w, 0:w], q_ref[:, j0:j0 + w]).astype(BF16)
        out += [q_t[r:r + LANES, :] for r in range(0, w, LANES)]
    return out


def _keep_rows(q_t, lo, hi):
    if lo == 0 and hi == q_t.shape[0]:
        return q_t
    row = lax.broadcasted_iota(jnp.int32, q_t.shape, 0)
    return jnp.where((row >= lo) & (row < hi), q_t, jnp.zeros_like(q_t))


def _head_rows(v_t, head):
    return v_t[head * HEAD_DIM:(head + 1) * HEAD_DIM, :]


def _store_heads(eye, o_ref, o_t_heads):
    for p in range(0, len(o_t_heads), 4):
        o_t = jnp.concatenate(o_t_heads[p:p + 4], axis=0).astype(BF16)
        o_ref[:, p * HEAD_DIM:p * HEAD_DIM + o_t.shape[0]] = _dot_nt(eye, o_t).astype(o_ref.dtype)


def _stage_values_t(eye, v_ref, vt_scr, n_chunks, chunk):
    for c in range(n_chunks):
        for j in range(v_ref.shape[1] // LANES):
            vt_scr[c, j * LANES:(j + 1) * LANES, :] = _transpose_bf16(
                eye, v_ref[c * chunk:(c + 1) * chunk, j * LANES:(j + 1) * LANES])


def _joint_kernel(*refs, subs, n_lat_tiles, seq, chunk, with_ctx_queries, diff_lam_init, bounded):
    if diff_lam_init is None:
        qr_ref, qn_ref, kl_ref, kc_ref, vl_ref, vc_ref, eye_ref, o_ref, vt_scr, vct_scr = refs
    else:
        qr_ref, qn_ref, kl_ref, kc_ref, vl_ref, vc_ref, eye_ref, lam_ref, gsub_ref, o_ref, vt_scr, vct_scr = refs
    t = pl.program_id(2)
    n_chunks = seq // chunk
    eye = eye_ref[...]

    @pl.when(t == 0)
    def _():
        _stage_values_t(eye, vl_ref, vt_scr, n_chunks, chunk)
        _stage_values_t(eye, vc_ref, vct_scr, 1, vc_ref.shape[0])

    def operands(q_ref):
        slabs = _q_slabs_t(eye, q_ref)
        return [_keep_rows(slabs[qs], lo, hi) for (qs, lo, hi, _, _) in subs]

    def finish(outs):
        if diff_lam_init is None:
            _store_heads(eye, o_ref, outs)
            return
        lv = lam_ref[...]
        lam = (jnp.exp(jnp.sum(lv[0:1, :] * lv[1:2, :], axis=-1, keepdims=True))
               - jnp.exp(jnp.sum(lv[2:3, :] * lv[3:4, :], axis=-1, keepdims=True)) + diff_lam_init)
        heads = []
        for o1, o2 in zip(outs[0::2], outs[1::2]):
            d = o1 - lam * o2
            ms = jnp.mean(d * d, axis=0, keepdims=True)
            heads.append((d * lax.rsqrt(ms + RMS_EPS) * gsub_ref[...]) * (1.0 - diff_lam_init))
        _store_heads(eye, o_ref, heads)

    def ctx_block():
        return (kc_ref, slice(None), vct_scr[0], operands(qn_ref), None)

    @pl.when(t < n_lat_tiles)
    def _():
        q_t = operands(qr_ref)
        latent = [(kl_ref, slice(c * chunk, (c + 1) * chunk), vt_scr[c], q_t, None) for c in range(n_chunks)]
        finish(_attend(latent + [ctx_block()], subs, bounded))

    if with_ctx_queries:
        @pl.when(t >= n_lat_tiles)
        def _():
            finish(_attend([ctx_block()], subs, bounded))


def _joint_call(operands, out_rows, *, subs, n_blocks, batch, seq, ctx_len, with_ctx_queries, name, bounded,
                diff_lam_init=None):
    qr, qn, k, v = operands[:4]
    tq = Q_TILE
    n_lat = seq // tq
    nqt = n_lat + (1 if with_ctx_queries else 0)
    ctx_row0 = batch * seq // ctx_len
    q_width, k_width, v_width = qr.shape[1] // n_blocks, k.shape[1] // n_blocks, v.shape[1] // n_blocks
    out_width = BRANCH_WIDTH // n_blocks
    chunk = min(KEY_CHUNK, seq)
    n_chunks = seq // chunk

    def qrow(b, t):
        return jnp.where(t < n_lat, b * n_lat + t, batch * n_lat + b)

    in_specs = [
        pl.BlockSpec((tq, q_width), lambda b, blk, t: (qrow(b, t), blk)),
        pl.BlockSpec((tq, q_width), lambda b, blk, t: (qrow(b, t), blk)),
        pl.BlockSpec((seq, k_width), lambda b, blk, t: (b, blk)),
        pl.BlockSpec((ctx_len, k_width), lambda b, blk, t: (ctx_row0 + b, blk)),
        pl.BlockSpec((seq, v_width), lambda b, blk, t: (b, blk)),
        pl.BlockSpec((ctx_len, v_width), lambda b, blk, t: (ctx_row0 + b, blk)),
        pl.BlockSpec((tq, tq), lambda b, blk, t: (0, 0)),
    ]
    args = [qr, qn, k, k, v, v, jnp.eye(tq, dtype=BF16)]
    if diff_lam_init is not None:
        in_specs += [pl.BlockSpec((4, DIFF_QK_DIM), lambda b, blk, t: (0, 0)),
                     pl.BlockSpec((HEAD_DIM, tq), lambda b, blk, t: (0, 0))]
        args += list(operands[4:])
    return pl.pallas_call(
        functools.partial(_joint_kernel, subs=subs, n_lat_tiles=n_lat, seq=seq, chunk=chunk,
                          with_ctx_queries=with_ctx_queries, diff_lam_init=diff_lam_init, bounded=bounded),
        grid=(batch, n_blocks, nqt),
        in_specs=in_specs,
        out_specs=pl.BlockSpec((tq, out_width), lambda b, blk, t: (qrow(b, t), blk)),
        out_shape=jax.ShapeDtypeStruct((out_rows, BRANCH_WIDTH), BF16),
        scratch_shapes=[pltpu.VMEM((n_chunks, v_width, chunk), BF16), pltpu.VMEM((1, v_width, ctx_len), BF16)],
        compiler_params=_cparams(("arbitrary", "arbitrary", "arbitrary")),
        name=name,
    )(*args)


def _na_kernel(q_ref, kl_ref, kc_ref, vl_ref, vc_ref, ab_ref, eye_ref, o_ref, vt_scr, vct_scr, *, n_lat_tiles):
    t = pl.program_id(2)
    tq = o_ref.shape[0]
    eye = eye_ref[...]
    rows_per_tile = tq // GRID_W
    band_tiles = NA_KEY_ROWS // rows_per_tile
    heads = range(BRANCH_HEADS)

    @pl.when(t == 0)
    def _():
        _stage_values_t(eye, vl_ref, vt_scr, n_lat_tiles, tq)
        _stage_values_t(eye, vc_ref, vct_scr, 1, vc_ref.shape[0])

    subs = tuple((h // 2, (h % 2) * HEAD_DIM, (h % 2 + 1) * HEAD_DIM, h // 2, h) for h in heads)

    def attend(blocks):
        slabs = _q_slabs_t(eye, q_ref)
        q_t = [_keep_rows(slabs[qs], lo, hi) for (qs, lo, hi, _, _) in subs]
        outs = _attend([(k_ref, rows, v_t, q_t, bias) for k_ref, rows, v_t, bias in blocks], subs, bounded=False)
        _store_heads(eye, o_ref, outs)

    ctx_block = (kc_ref, slice(None), vct_scr[0], None)

    @pl.when(t < n_lat_tiles)
    def _():
        tile0 = jnp.clip(t - NA_WIN_H // 2 // rows_per_tile, 0, n_lat_tiles - band_tiles)
        rows = pl.ds(pl.multiple_of(tile0 * tq, tq), band_tiles * tq)
        v_t = jnp.concatenate([vt_scr[tile0 + j] for j in range(band_tiles)], axis=1)
        attend([(kl_ref, rows, v_t, ab_ref), ctx_block])

    @pl.when(t >= n_lat_tiles)
    def _():
        attend([ctx_block])


def _na_call(q, k, v, ab, out_rows, *, batch, seq, ctx_len, with_ctx_queries):
    tq = Q_TILE
    n_lat = seq // tq
    nqt = n_lat + (1 if with_ctx_queries else 0)
    ctx_row0 = batch * seq // ctx_len
    n_keys = NA_KEY_ROWS * GRID_W
    width = BRANCH_WIDTH

    def qrow(b, t):
        return jnp.where(t < n_lat, b * n_lat + t, batch * n_lat + b)

    def pattern(t):
        return jnp.where(t == 0, 0, jnp.where(t >= n_lat - 1, 2, 1))

    return pl.pallas_call(
        functools.partial(_na_kernel, n_lat_tiles=n_lat),
        grid=(batch, 1, nqt),
        in_specs=[
            pl.BlockSpec((tq, width), lambda b, blk, t: (qrow(b, t), 0)),
            pl.BlockSpec((seq, width), lambda b, blk, t: (b, 0)),
            pl.BlockSpec((ctx_len, width), lambda b, blk, t: (ctx_row0 + b, 0)),
            pl.BlockSpec((seq, width), lambda b, blk, t: (b, 0)),
            pl.BlockSpec((ctx_len, width), lambda b, blk, t: (ctx_row0 + b, 0)),
            pl.BlockSpec((None, BRANCH_HEADS, n_keys, tq), lambda b, blk, t: (pattern(t), 0, 0, 0)),
            pl.BlockSpec((tq, tq), lambda b, blk, t: (0, 0)),
        ],
        out_specs=pl.BlockSpec((tq, width), lambda b, blk, t: (qrow(b, t), 0)),
        out_shape=jax.ShapeDtypeStruct((out_rows, width), BF16),
        scratch_shapes=[pltpu.VMEM((n_lat, width, tq), BF16), pltpu.VMEM((1, width, ctx_len), BF16)],
        compiler_params=_cparams(("arbitrary", "arbitrary", "arbitrary")),
        name="neighbourhood_attention",
    )(q, k, k, v, v, ab, jnp.eye(tq, dtype=BF16))


def _merge_kernel(x_ref, mod_ref, n_ref, o0_ref, o1_ref, o2_ref, o3_ref, wg_ref, bg_ref, wb_ref, wo_ref,
                  out_ref, y_scr):
    n = n_ref[...]
    branch = [o_ref[...] for o_ref in (o0_ref, o1_ref, o2_ref, o3_ref)]
    for c0 in range(0, n.shape[1], MERGE_SLAB):
        cols = slice(c0, c0 + MERGE_SLAB)
        y = None
        for j in range(N_BRANCH):
            gate = jax.nn.sigmoid(_dot(n, wg_ref[j, :, cols]) + bg_ref[j, :, cols])
            term = gate * _dot(branch[j], wb_ref[j, :, cols])
            y = term if y is None else y + term
        y_scr[:, cols] = y.astype(BF16)
    out_ref[...] = x_ref[...] + mod_ref[5:6, :] * _dot(y_scr[...], wo_ref[...])


def _merge_call(x, mod, n, outs, w_gate, b_gate, w_branch, w_out, layer, n_tiles, mod_idx):
    d = x.shape[1]
    tm = TOKEN_TILE
    row = lambda i: (i, 0)
    return pl.pallas_call(
        _merge_kernel,
        grid=(n_tiles,),
        in_specs=[
            pl.BlockSpec((tm, d), row),
            pl.BlockSpec((None, N_MOD, d), lambda i: (mod_idx(i), 0, 0)),
            pl.BlockSpec((tm, d), row),
        ] + [pl.BlockSpec((tm, BRANCH_WIDTH), row)] * 4 + [
            _resident((None, N_BRANCH, d, d), lambda i: (layer, 0, 0, 0)),
            _resident((None, N_BRANCH, 1, d), lambda i: (layer, 0, 0, 0)),
            _resident((None, N_BRANCH, BRANCH_WIDTH, d), lambda i: (layer, 0, 0, 0)),
            _resident((None, d, d), lambda i: (layer, 0, 0)),
        ],
        out_specs=pl.BlockSpec((tm, d), row),
        out_shape=jax.ShapeDtypeStruct((n_tiles * tm, d), F32),
        scratch_shapes=[pltpu.VMEM((tm, d), BF16)],
        compiler_params=_cparams(("arbitrary",)),
        name="branch_merge",
    )(x, mod, n, *outs, w_gate, b_gate, w_branch, w_out)


def _rope_tables(seq, rot_dim, pad_rows):
    t = jnp.arange(seq, dtype=jnp.int32)
    row = (t // GRID_W).astype(F32)
    col = (t % GRID_W).astype(F32)
    n_freq = rot_dim // 4
    inv = ROPE_THETA ** (-jnp.arange(n_freq, dtype=F32) / n_freq)
    ar = row[:, None] * inv[None, :]
    ac = col[:, None] * inv[None, :]
    ang = jnp.concatenate([ar, ar, ac, ac], axis=-1)
    sign = jnp.tile(jnp.concatenate([-jnp.ones((n_freq,), F32), jnp.ones((n_freq,), F32)]), 2)
    cos = jnp.concatenate([jnp.cos(ang), jnp.ones((pad_rows, rot_dim), F32)], axis=0)
    sin = jnp.concatenate([jnp.sin(ang) * sign[None, :], jnp.zeros((pad_rows, rot_dim), F32)], axis=0)
    return cos, sin


def _all_rope_tables(seq, pad_rows):
    c64, s64 = _rope_tables(seq, HEAD_DIM, pad_rows)
    c32, s32 = _rope_tables(seq, DIFF_QK_DIM, pad_rows)
    rows = seq + pad_rows
    ones = lambda w: jnp.ones((rows, w), F32)
    zeros = lambda w: jnp.zeros((rows, w), F32)
    pad = LANES - MLA_QK_DIM
    cm = jnp.concatenate([ones(MLA_NOPE_DIM), c32, ones(pad)], axis=-1)
    sm = jnp.concatenate([zeros(MLA_NOPE_DIM), s32, zeros(pad)], axis=-1)
    return (jnp.tile(c64, (1, 2)), jnp.tile(s64, (1, 2)), jnp.tile(c32, (1, 4)), jnp.tile(s32, (1, 4)), cm, sm)


def _block_diag_ones(width, seg):
    idx = np.arange(width) // seg
    return jnp.asarray(idx[:, None] == idx[None, :], dtype=BF16)


def _na_bias_tables(rpb, grid_rows):
    rows_per_tile = Q_TILE // GRID_W
    wh = min(NA_WIN_H, grid_rows)
    kinds = ((0, 0), (rows_per_tile, 0), (grid_rows - rows_per_tile, grid_rows - NA_KEY_ROWS))
    col = np.arange(GRID_W)
    c0 = np.clip(col - NA_WIN_W // 2, 0, GRID_W - NA_WIN_W)
    col_ok = (col[:, None] >= c0[None, :]) & (col[:, None] < c0[None, :] + NA_WIN_W)
    d_col = np.clip(col[:, None] - col[None, :], -(NA_WIN_W - 1), NA_WIN_W - 1) + (NA_WIN_W - 1)
    col_hot = (d_col[..., None] == np.arange(2 * NA_WIN_W - 1)).astype(np.float32)
    row_hot, valid = [], []
    for q_row0, k_row0 in kinds:
        r = q_row0 + np.arange(rows_per_tile)
        kr = k_row0 + np.arange(NA_KEY_ROWS)
        r0 = np.clip(r - wh // 2, 0, grid_rows - wh)
        row_ok = (kr[:, None] >= r0[None, :]) & (kr[:, None] < r0[None, :] + wh)
        d_row = kr[:, None] - r[None, :] + (NA_WIN_H - 1)
        row_hot.append((d_row[..., None] == np.arange(2 * NA_WIN_H - 1)).astype(np.float32))
        valid.append(row_ok[:, None, :, None] & col_ok[None, :, None, :])
    bias = jnp.einsum("nkqa,lhab,dcb->lnhkdqc", jnp.asarray(np.stack(row_hot)), rpb.astype(F32),
                      jnp.asarray(col_hot), precision=lax.Precision.HIGHEST)
    ab = jnp.where(jnp.asarray(np.stack(valid))[None, :, None], bias * LOG2E, NEG_BIG)
    depth = rpb.shape[0]
    return ab.reshape(depth, 3, BRANCH_HEADS, NA_KEY_ROWS * GRID_W, Q_TILE)


def _pad_heads(w, head_w, to_w):
    lead = w.shape[:-1]
    w = w.reshape(lead + (BRANCH_HEADS, head_w))
    w = jnp.pad(w, [(0, 0)] * len(lead) + [(0, 0), (0, to_w - head_w)])
    return w.reshape(lead + (BRANCH_HEADS * to_w,))


def _gqa_head_order(a, axis):
    parts = jnp.split(a, BRANCH_HEADS, axis=axis)
    return jnp.concatenate([parts[0], parts[2], parts[1], parts[3]], axis=axis)


def kernel(x, c, ctx, c_ctx, w_ada, b_ada, norm_g, ffn_w1, ffn_w3, ffn_w2, w_in, na_g_q, na_g_k, na_rpb, gqa_g_q, gqa_g_k, diff_g_q, diff_g_k, diff_lambda, diff_g_sub, mla_g_cq, mla_g_ckv, mla_w_uq, mla_w_ukv, mla_g_q, mla_g_k, w_gate, b_gate, w_branch, w_out):
    batch, seq, d = x.shape
    ctx_len = ctx.shape[1]
    depth = w_ada.shape[0]
    tm = TOKEN_TILE
    assert d == D_MODEL and ctx_len == Q_TILE and seq % tm == 0 and (batch * ctx_len) % tm == 0
    assert seq % min(KEY_CHUNK, seq) == 0 and seq // GRID_W >= NA_KEY_ROWS and batch < 16
    n_lat_rows = batch * seq
    lat_tiles = n_lat_rows // tm
    all_tiles = lat_tiles + batch * ctx_len // tm
    tiles_per_batch = seq // tm

    def mod_idx(i):
        return jnp.where(i < lat_tiles, i // tiles_per_batch, batch)

    def tab_idx(i):
        return jnp.where(i < lat_tiles, i % tiles_per_batch, tiles_per_batch)

    c16 = jnp.concatenate([c, c_ctx[None, :], jnp.zeros((16 - batch - 1, d), F32)], axis=0)
    mods = _mod_call(c16, w_ada, b_ada).reshape(depth, 16, N_MOD, d)
    w1, w3, w2 = ffn_w1.astype(BF16), ffn_w3.astype(BF16), ffn_w2.astype(BF16)
    na_w, gq_w, df_w, ml_w = jnp.split(w_in, [768, 1280, 2048], axis=-1)
    w_in_p = jnp.concatenate([
        na_w, _gqa_head_order(gq_w[..., :256], -1), gq_w[..., 256:], df_w, ml_w,
        jnp.zeros((depth, d, IN_COLS_PAD - w_in.shape[-1]), F32)], axis=-1).astype(BF16)
    wuq = _pad_heads(mla_w_uq, MLA_QK_DIM, LANES).astype(BF16)
    wukv = mla_w_ukv.reshape(depth, MLA_KV_RANK, BRANCH_HEADS, MLA_NOPE_DIM + HEAD_DIM)
    wuk = _pad_heads(wukv[..., :MLA_NOPE_DIM].reshape(depth, MLA_KV_RANK, -1), MLA_NOPE_DIM, LANES).astype(BF16)
    wuv = wukv[..., MLA_NOPE_DIM:].reshape(depth, MLA_KV_RANK, BRANCH_WIDTH).astype(BF16)

    def vec_row(v, reps=1, scale=1.0):
        v = jnp.tile(v, (1, reps)) * scale
        return jnp.pad(v, ((0, 0), (0, 512 - v.shape[-1])))

    pad_g = lambda g: jnp.pad(g, ((0, 0), (0, LANES - MLA_QK_DIM)))
    vec_rows = [
        vec_row(na_g_q, 4, HEAD_DIM ** -0.5 * LOG2E), vec_row(na_g_k, 4),
        vec_row(gqa_g_q, 4, HEAD_DIM ** -0.5 * LOG2E), vec_row(gqa_g_k, 2),
        vec_row(diff_g_q, 8, DIFF_QK_DIM ** -0.5 * LOG2E), vec_row(diff_g_k, 8),
        vec_row(mla_g_cq), vec_row(mla_g_ckv),
        vec_row(pad_g(mla_g_q), 4, MLA_QK_DIM ** -0.5 * LOG2E), vec_row(pad_g(mla_g_k), 4),
    ]
    vecs = jnp.stack(vec_rows + [jnp.zeros_like(vec_rows[0])] * (16 - len(vec_rows)), axis=1)
    bds = (_block_diag_ones(256, HEAD_DIM), _block_diag_ones(256, DIFF_QK_DIM), _block_diag_ones(256, LANES))
    tables = _all_rope_tables(seq, tm)
    na_ab = _na_bias_tables(na_rpb, seq // GRID_W)
    gsub = jnp.broadcast_to(diff_g_sub[:, :, None], (depth, HEAD_DIM, Q_TILE))
    wg, wo = w_gate.astype(BF16), w_out.astype(BF16)
    wb = jnp.concatenate([w_branch[:, 0:1], _gqa_head_order(w_branch[:, 1:2], 2), w_branch[:, 2:]], axis=1).astype(BF16)
    bg = b_gate.reshape(depth, N_BRANCH, 1, d)

    xa = jnp.concatenate([x.reshape(n_lat_rows, d), ctx.reshape(batch * ctx_len, d)], axis=0)

    gqa_subs = ((0, 0, 64, 0, 0), (0, 64, 128, 0, 1), (1, 0, 64, 0, 0), (1, 64, 128, 0, 1))
    diff_subs = ((0, 0, 32, 0, 0), (0, 32, 64, 0, 0), (0, 64, 96, 0, 1), (0, 96, 128, 0, 1))
    mla_subs = tuple((h, 0, LANES, h, h) for h in range(BRANCH_HEADS))

    for i in range(depth):
        need_ctx = i < depth - 1
        mod = mods[i]
        g = norm_g[i].reshape(3, 1, d)
        xa = _ffn_call(xa, mod, g[0], w1, w3, w2, i, 0, 0, all_tiles, mod_idx)
        (n, naq, nak, nav, gqr, gqn, gk, gv, dqr, dqn, dk, dv, mqr, mqn, mk, mv) = _proj_call(
            xa, mod, g[1], w_in_p, vecs, bds, wuq, wuk, wuv, tables, i, all_tiles, mod_idx, tab_idx)
        out_rows = xa.shape[0] if need_ctx else n_lat_rows
        common = dict(batch=batch, seq=seq, ctx_len=ctx_len, with_ctx_queries=need_ctx)
        lam_init = 0.8 - 0.6 * math.exp(-0.3 * i)
        def joint(bound, *operands, **kw):
            return lax.cond(bound <= SCORE_BOUND,
                            lambda *a: _joint_call(a, out_rows, bounded=True, **kw, **common),
                            lambda *a: _joint_call(a, out_rows, bounded=False, **kw, **common), *operands)

        outs = [
            _na_call(naq, nak, nav, na_ab[i], out_rows, **common),
            joint(_score_bound(gqa_g_q[i], gqa_g_k[i], HEAD_DIM), gqr, gqn, gk, gv,
                  subs=gqa_subs, n_blocks=1, name="gqa_attention"),
            joint(_score_bound(diff_g_q[i], diff_g_k[i], DIFF_QK_DIM), dqr, dqn, dk, dv, diff_lambda[i], gsub[i],
                  subs=diff_subs, n_blocks=2, name="diff_attention", diff_lam_init=lam_init),
            joint(_score_bound(mla_g_q[i], mla_g_k[i], MLA_QK_DIM), mqr, mqn, mk, mv,
                  subs=mla_subs, n_blocks=1, name="mla_attention"),
        ]
        n_tiles = all_tiles if need_ctx else lat_tiles
        xa = _merge_call(xa, mod, n, outs, wg, bg, wb, wo, i, n_tiles, mod_idx)
        xa = _ffn_call(xa, mod, g[2], w1, w3, w2, i, 1, 6, n_tiles, mod_idx)
    return xa.reshape(batch, seq, d)
```

```python
import functools
import math

import jax
import jax.numpy as jnp
import numpy as np
from jax import lax
from jax.experimental import pallas as pl
from jax.experimental.pallas import tpu as pltpu

F32 = jnp.float32
BF16 = jnp.bfloat16

D_MODEL = 1024
GRID_W = 64
ROPE_THETA = 10000.0
RMS_EPS = 1e-6
HEAD_DIM = 64
N_BRANCH = 4
BRANCH_HEADS = 4
BRANCH_WIDTH = 256
NA_WIN_H = 8
NA_WIN_W = 16
DIFF_QK_DIM = 32
MLA_Q_RANK = 256
MLA_KV_RANK = 128
MLA_NOPE_DIM = 64
MLA_ROPE_DIM = 32
MLA_QK_DIM = 96
D_FF = 2816
N_MOD = 9
LOG2E = 1.4426950408889634

LANES = 128
TOKEN_TILE = 512
Q_TILE = 256
KEY_CHUNK = 4096
FF_SLAB = 256
MERGE_SLAB = 256
PROJ_ROWS = 256
NA_KEY_ROWS = 12
NEG_BIG = -1e30
SCORE_BOUND = 40.0
VMEM_LIMIT = 56 * 1024 * 1024

C_NA_Q, C_NA_K, C_NA_V = 0, 256, 512
C_GQ_Q, C_GQ_K, C_GQ_V = 768, 1024, 1152
C_DF_Q, C_DF_K, C_DF_V = 1280, 1536, 1792
C_ML_CQ, C_ML_CKV, C_ML_ROPE = 2048, 2304, 2432
IN_COLS_PAD = 2560


def _cparams(sem):
    return pltpu.CompilerParams(dimension_semantics=sem, vmem_limit_bytes=VMEM_LIMIT)


def _dot(a, b):
    return jnp.dot(a, b, preferred_element_type=F32)


def _dot_nt(a, b):
    return lax.dot_general(a, b, (((1,), (1,)), ((), ())), preferred_element_type=F32)


def _silu(a):
    return a * jax.nn.sigmoid(a)


def _modulated_norm(x, g, shift, scale):
    ms = jnp.mean(x * x, axis=-1, keepdims=True)
    return (x * lax.rsqrt(ms + RMS_EPS) * g) * (1.0 + scale) + shift


def _mod_kernel(c_ref, w_ref, b_ref, o_ref):
    c = c_ref[...]
    o_ref[...] = _dot(_silu(c).astype(BF16), w_ref[...].astype(BF16)) + b_ref[...]


def _mod_call(c16, w_ada, b_ada):
    depth, d, nd = w_ada.shape
    tn = 1024
    return pl.pallas_call(
        _mod_kernel,
        grid=(depth, nd // tn),
        in_specs=[
            pl.BlockSpec((16, d), lambda l, j: (0, 0)),
            pl.BlockSpec((None, d, tn), lambda l, j: (l, 0, j)),
            pl.BlockSpec((None, 1, tn), lambda l, j: (l, 0, j)),
        ],
        out_specs=pl.BlockSpec((None, 16, tn), lambda l, j: (l, 0, j)),
        out_shape=jax.ShapeDtypeStruct((depth, 16, nd), F32),
        compiler_params=_cparams(("arbitrary", "arbitrary")),
        name="adaln_mod",
    )(c16, w_ada, b_ada.reshape(depth, 1, nd))


def _ffn_kernel(x_ref, mod_ref, g_ref, w1_ref, w3_ref, w2_ref, o_ref, act_scr, *, row0):
    x = x_ref[...]
    h = _modulated_norm(x, g_ref[...], mod_ref[row0:row0 + 1, :], mod_ref[row0 + 1:row0 + 2, :]).astype(BF16)
    for f0 in range(0, D_FF, FF_SLAB):
        a = _dot(h, w1_ref[:, f0:f0 + FF_SLAB])
        b = _dot(h, w3_ref[:, f0:f0 + FF_SLAB])
        act_scr[:, f0:f0 + FF_SLAB] = (_silu(a) * b).astype(BF16)
    o_ref[...] = x + (0.5 * mod_ref[row0 + 2:row0 + 3, :]) * _dot(act_scr[...], w2_ref[...])


def _resident(block_shape, index_map):
    return pl.BlockSpec(block_shape, index_map, pipeline_mode=pl.Buffered(1))


def _ffn_call(x, mod, g, w1, w3, w2, layer, half, row0, n_tiles, mod_idx):
    d = x.shape[1]
    tm = TOKEN_TILE
    return pl.pallas_call(
        functools.partial(_ffn_kernel, row0=row0),
        grid=(n_tiles,),
        in_specs=[
            pl.BlockSpec((tm, d), lambda i: (i, 0)),
            pl.BlockSpec((None, N_MOD, d), lambda i: (mod_idx(i), 0, 0)),
            _resident((1, d), lambda i: (0, 0)),
            _resident((None, None, d, D_FF), lambda i: (layer, half, 0, 0)),
            _resident((None, None, d, D_FF), lambda i: (layer, half, 0, 0)),
            _resident((None, None, D_FF, d), lambda i: (layer, half, 0, 0)),
        ],
        out_specs=pl.BlockSpec((tm, d), lambda i: (i, 0)),
        out_shape=jax.ShapeDtypeStruct((n_tiles * tm, d), F32),
        scratch_shapes=[pltpu.VMEM((tm, D_FF), BF16)],
        compiler_params=_cparams(("arbitrary",)),
        name="ffn_half_step",
    )(x, mod, g, w1, w3, w2)


def _segnorm(x, bd, seg_len, gain):
    x2 = x * x
    hi = x2.astype(BF16)
    lo = (x2 - hi.astype(F32)).astype(BF16)
    groups = []
    for c0 in range(0, x.shape[1], bd.shape[0]):
        w = min(bd.shape[0], x.shape[1] - c0)
        groups.append(_dot(hi[:, c0:c0 + w], bd[0:w, 0:w]) + _dot(lo[:, c0:c0 + w], bd[0:w, 0:w]))
    ss = groups[0] if len(groups) == 1 else jnp.concatenate(groups, axis=-1)
    return x * lax.rsqrt(ss * (1.0 / seg_len) + RMS_EPS) * gain


def _tile_lanes(t, width):
    reps = width // t.shape[-1]
    return t if reps == 1 else jnp.concatenate([t] * reps, axis=-1)


def _rope(x, cos, sin_signed, quarter):
    width = x.shape[-1]
    lane = lax.broadcasted_iota(jnp.int32, x.shape, 1)
    first = (lane % (2 * quarter)) < quarter
    rot = jnp.where(first, pltpu.roll(x, width - quarter, 1), pltpu.roll(x, quarter, 1))
    return x * _tile_lanes(cos, width) + rot * _tile_lanes(sin_signed, width)


def _proj_kernel(x_ref, mod_ref, g_ref, w_ref, vec_ref, bd64_ref, bd32_ref, bd128_ref,
                 wuq_ref, wuk_ref, wuv_ref, c64_ref, s64_ref, c32_ref, s32_ref, cm_ref, sm_ref,
                 n_ref, naq_ref, nak_ref, nav_ref,
                 gqr_ref, gqn_ref, gk_ref, gv_ref,
                 dqr_ref, dqn_ref, dk_ref, dv_ref,
                 mqr_ref, mqn_ref, mk_ref, mv_ref):
    bd64, bd32, bd128 = bd64_ref[...], bd32_ref[...], bd128_ref[...]

    def vec(row, width):
        return vec_ref[row:row + 1, 0:width]

    groups = [slice(r, r + PROJ_ROWS) for r in range(0, x_ref.shape[0], PROJ_ROWS)]
    projected = []
    for rows in groups:
        n = _modulated_norm(x_ref[rows, :], g_ref[...], mod_ref[3:4, :], mod_ref[4:5, :]).astype(BF16)
        n_ref[rows, :] = n
        projected.append(_dot(n, w_ref[...]))

    for rows, p in zip(groups, projected):
        c64, s64 = c64_ref[rows, :], s64_ref[rows, :]
        c32, s32 = c32_ref[rows, :], s32_ref[rows, :]
        cm, sm = cm_ref[rows, :], sm_ref[rows, :]

        naq_ref[rows, :] = _segnorm(p[:, C_NA_Q:C_NA_Q + 256], bd64, HEAD_DIM, vec(0, 256)).astype(BF16)
        nak_ref[rows, :] = _segnorm(p[:, C_NA_K:C_NA_K + 256], bd64, HEAD_DIM, vec(1, 256)).astype(BF16)
        nav_ref[rows, :] = p[:, C_NA_V:C_NA_V + 256].astype(BF16)

        q = _segnorm(p[:, C_GQ_Q:C_GQ_Q + 256], bd64, HEAD_DIM, vec(2, 256))
        gqn_ref[rows, :] = q.astype(BF16)
        gqr_ref[rows, :] = _rope(q, c64, s64, HEAD_DIM // 4).astype(BF16)
        k = _segnorm(p[:, C_GQ_K:C_GQ_K + 128], bd64, HEAD_DIM, vec(3, 128))
        gk_ref[rows, :] = _rope(k, c64, s64, HEAD_DIM // 4).astype(BF16)
        gv_ref[rows, :] = p[:, C_GQ_V:C_GQ_V + 128].astype(BF16)

        q = _segnorm(p[:, C_DF_Q:C_DF_Q + 256], bd32, DIFF_QK_DIM, vec(4, 256))
        dqn_ref[rows, :] = q.astype(BF16)
        dqr_ref[rows, :] = _rope(q, c32, s32, DIFF_QK_DIM // 4).astype(BF16)
        k = _segnorm(p[:, C_DF_K:C_DF_K + 256], bd32, DIFF_QK_DIM, vec(5, 256))
        dk_ref[rows, :] = _rope(k, c32, s32, DIFF_QK_DIM // 4).astype(BF16)
        dv_ref[rows, :] = p[:, C_DF_V:C_DF_V + 256].astype(BF16)

        cq = p[:, C_ML_CQ:C_ML_CQ + 256]
        cq = cq * lax.rsqrt(jnp.mean(cq * cq, axis=-1, keepdims=True) + RMS_EPS) * vec(6, 256)
        q = _segnorm(_dot(cq.astype(BF16), wuq_ref[...]), bd128, MLA_QK_DIM, vec(8, 512))
        mqn_ref[rows, :] = q.astype(BF16)
        mqr_ref[rows, :] = _rope(q, cm, sm, MLA_ROPE_DIM // 4).astype(BF16)
        ckv = p[:, C_ML_CKV:C_ML_CKV + 128]
        ckv = (ckv * lax.rsqrt(jnp.mean(ckv * ckv, axis=-1, keepdims=True) + RMS_EPS) * vec(7, 128)).astype(BF16)
        k_rope = pltpu.roll(p[:, C_ML_ROPE:C_ML_ROPE + 128], MLA_NOPE_DIM, 1)
        k = _dot(ckv, wuk_ref[...]) + _tile_lanes(k_rope, 512)
        k = _segnorm(k, bd128, MLA_QK_DIM, vec(9, 512))
        mk_ref[rows, :] = _rope(k, cm, sm, MLA_ROPE_DIM // 4).astype(BF16)
        mv_ref[rows, :] = _dot(ckv, wuv_ref[...]).astype(BF16)


_PROJ_OUT_WIDTHS = (1024, 256, 256, 256, 256, 256, 128, 128, 256, 256, 256, 256, 512, 512, 512, 256)


def _proj_call(x, mod, g, w_in, vecs, bds, wuq, wuk, wuv, tables, layer, n_tiles, mod_idx, tab_idx):
    d = x.shape[1]
    tm = TOKEN_TILE
    const2 = lambda i: (0, 0)
    in_specs = [
        pl.BlockSpec((tm, d), lambda i: (i, 0)),
        pl.BlockSpec((None, N_MOD, d), lambda i: (mod_idx(i), 0, 0)),
        pl.BlockSpec((1, d), const2),
        pl.BlockSpec((None, d, IN_COLS_PAD), lambda i: (layer, 0, 0)),
        pl.BlockSpec((None, 16, 512), lambda i: (layer, 0, 0)),
        pl.BlockSpec((256, 256), const2),
        pl.BlockSpec((256, 256), const2),
        pl.BlockSpec((256, 256), const2),
        pl.BlockSpec((None, 256, 512), lambda i: (layer, 0, 0)),
        pl.BlockSpec((None, 128, 512), lambda i: (layer, 0, 0)),
        pl.BlockSpec((None, 128, 256), lambda i: (layer, 0, 0)),
    ] + [pl.BlockSpec((tm, LANES), lambda i: (tab_idx(i), 0))] * 6
    rows = n_tiles * tm
    return pl.pallas_call(
        _proj_kernel,
        grid=(n_tiles,),
        in_specs=in_specs,
        out_specs=[pl.BlockSpec((tm, w), lambda i: (i, 0)) for w in _PROJ_OUT_WIDTHS],
        out_shape=[jax.ShapeDtypeStruct((rows, w), BF16) for w in _PROJ_OUT_WIDTHS],
        compiler_params=_cparams(("arbitrary",)),
        name="mixer_projection",
    )(x, mod, g, w_in, vecs, *bds, wuq, wuk, wuv, *tables)


def _attend(blocks, subs, bounded):
    scores = []
    for i, sub in enumerate(subs):
        ks = sub[3]
        per_block = []
        for k_ref, rows, _, q_t, bias in blocks:
            s = _dot(k_ref[rows, ks * LANES:(ks + 1) * LANES], q_t[i])
            per_block.append(s if bias is None else s + bias[i])
        scores.append(per_block)
    outs = []
    for i, sub in enumerate(subs):
        if not bounded:
            m = functools.reduce(jnp.maximum, [jnp.max(s, axis=0, keepdims=True) for s in scores[i]])
        l = acc = None
        for s, (_, _, v_t, _, _) in zip(scores[i], blocks):
            p = jnp.exp2(s if bounded else s - m)
            l_blk = jnp.sum(p, axis=0, keepdims=True)
            acc_blk = _dot(_head_rows(v_t, sub[4]), p.astype(BF16))
            l, acc = (l_blk, acc_blk) if l is None else (l + l_blk, acc + acc_blk)
        outs.append(acc / l)
    return outs


def _score_bound(g_q, g_k, dim):
    return 1.01 * dim * (dim ** -0.5 * LOG2E) * jnp.max(jnp.abs(g_q)) * jnp.max(jnp.abs(g_k))


def _transpose_bf16(eye, a):
    return _dot_nt(eye[0:a.shape[1], 0:a.shape[1]], a).astype(BF16)


def _q_slabs_t(eye, q_ref):
    out = []
    for j0 in range(0, q_ref.shape[1], 2 * LANES):
        w = min(2 * LANES, q_ref.shape[1] - j0)
        q_t = _dot_nt(eye[0:w, 0:w], q_ref[:, j0:j0 + w]).astype(BF16)
        out += [q_t[r:r + LANES, :] for r in range(0, w, LANES)]
    return out


def _keep_rows(q_t, lo, hi):
    if lo == 0 and hi == q_t.shape[0]:
        return q_t
    row = lax.broadcasted_iota(jnp.int32, q_t.shape, 0)
    return jnp.where((row >= lo) & (row < hi), q_t, jnp.zeros_like(q_t))


def _head_rows(v_t, head):
    return v_t[head * HEAD_DIM:(head + 1) * HEAD_DIM, :]


def _store_heads(eye, o_ref, o_t_heads):
    for p in range(0, len(o_t_heads), 4):
        o_t = jnp.concatenate(o_t_heads[p:p + 4], axis=0).astype(BF16)
        o_ref[:, p * HEAD_DIM:p * HEAD_DIM + o_t.shape[0]] = _dot_nt(eye, o_t).astype(o_ref.dtype)


def _stage_values_t(eye, v_ref, vt_scr, n_chunks, chunk):
    for c in range(n_chunks):
        for j in range(v_ref.shape[1] // LANES):
            vt_scr[c, j * LANES:(j + 1) * LANES, :] = _transpose_bf16(
                eye, v_ref[c * chunk:(c + 1) * chunk, j * LANES:(j + 1) * LANES])


def _joint_kernel(*refs, subs, n_lat_tiles, seq, chunk, with_ctx_queries, diff_lam_init, bounded):
    if diff_lam_init is None:
        qr_ref, qn_ref, kl_ref, kc_ref, vl_ref, vc_ref, eye_ref, o_ref, vt_scr, vct_scr = refs
    else:
        qr_ref, qn_ref, kl_ref, kc_ref, vl_ref, vc_ref, eye_ref, lam_ref, gsub_ref, o_ref, vt_scr, vct_scr = refs
    t = pl.program_id(2)
    n_chunks = seq // chunk
    eye = eye_ref[...]

    @pl.when(t == 0)
    def _():
        _stage_values_t(eye, vl_ref, vt_scr, n_chunks, chunk)
        _stage_values_t(eye, vc_ref, vct_scr, 1, vc_ref.shape[0])

    def operands(q_ref):
        slabs = _q_slabs_t(eye, q_ref)
        return [_keep_rows(slabs[qs], lo, hi) for (qs, lo, hi, _, _) in subs]

    def finish(outs):
        if diff_lam_init is None:
            _store_heads(eye, o_ref, outs)
            return
        lv = lam_ref[...]
        lam = (jnp.exp(jnp.sum(lv[0:1, :] * lv[1:2, :], axis=-1, keepdims=True))
               - jnp.exp(jnp.sum(lv[2:3, :] * lv[3:4, :], axis=-1, keepdims=True)) + diff_lam_init)
        heads = []
        for o1, o2 in zip(outs[0::2], outs[1::2]):
            d = o1 - lam * o2
            ms = jnp.mean(d * d, axis=0, keepdims=True)
            heads.append((d * lax.rsqrt(ms + RMS_EPS) * gsub_ref[...]) * (1.0 - diff_lam_init))
        _store_heads(eye, o_ref, heads)

    def ctx_block():
        return (kc_ref, slice(None), vct_scr[0], operands(qn_ref), None)

    @pl.when(t < n_lat_tiles)
    def _():
        q_t = operands(qr_ref)
        latent = [(kl_ref, slice(c * chunk, (c + 1) * chunk), vt_scr[c], q_t, None) for c in range(n_chunks)]
        finish(_attend(latent + [ctx_block()], subs, bounded))

    if with_ctx_queries:
        @pl.when(t >= n_lat_tiles)
        def _():
            finish(_attend([ctx_block()], subs, bounded))


def _joint_call(operands, out_rows, *, subs, n_blocks, batch, seq, ctx_len, with_ctx_queries, name, bounded,
                diff_lam_init=None):
    qr, qn, k, v = operands[:4]
    tq = Q_TILE
    n_lat = seq // tq
    nqt = n_lat + (1 if with_ctx_queries else 0)
    ctx_row0 = batch * seq // ctx_len
    q_width, k_width, v_width = qr.shape[1] // n_blocks, k.shape[1] // n_blocks, v.shape[1] // n_blocks
    out_width = BRANCH_WIDTH // n_blocks
    chunk = min(KEY_CHUNK, seq)
    n_chunks = seq // chunk

    def qrow(b, t):
        return jnp.where(t < n_lat, b * n_lat + t, batch * n_lat + b)

    in_specs = [
        pl.BlockSpec((tq, q_width), lambda b, blk, t: (qrow(b, t), blk)),
        pl.BlockSpec((tq, q_width), lambda b, blk, t: (qrow(b, t), blk)),
        pl.BlockSpec((seq, k_width), lambda b, blk, t: (b, blk)),
        pl.BlockSpec((ctx_len, k_width), lambda b, blk, t: (ctx_row0 + b, blk)),
        pl.BlockSpec((seq, v_width), lambda b, blk, t: (b, blk)),
        pl.BlockSpec((ctx_len, v_width), lambda b, blk, t: (ctx_row0 + b, blk)),
        pl.BlockSpec((tq, tq), lambda b, blk, t: (0, 0)),
    ]
    args = [qr, qn, k, k, v, v, jnp.eye(tq, dtype=BF16)]
    if diff_lam_init is not None:
        in_specs += [pl.BlockSpec((4, DIFF_QK_DIM), lambda b, blk, t: (0, 0)),
                     pl.BlockSpec((HEAD_DIM, tq), lambda b, blk, t: (0, 0))]
        args += list(operands[4:])
    return pl.pallas_call(
        functools.partial(_joint_kernel, subs=subs, n_lat_tiles=n_lat, seq=seq, chunk=chunk,
                          with_ctx_queries=with_ctx_queries, diff_lam_init=diff_lam_init, bounded=bounded),
        grid=(batch, n_blocks, nqt),
        in_specs=in_specs,
        out_specs=pl.BlockSpec((tq, out_width), lambda b, blk, t: (qrow(b, t), blk)),
        out_shape=jax.ShapeDtypeStruct((out_rows, BRANCH_WIDTH), BF16),
        scratch_shapes=[pltpu.VMEM((n_chunks, v_width, chunk), BF16), pltpu.VMEM((1, v_width, ctx_len), BF16)],
        compiler_params=_cparams(("arbitrary", "arbitrary", "arbitrary")),
        name=name,
    )(*args)


def _na_kernel(q_ref, kl_ref, kc_ref, vl_ref, vc_ref, ab_ref, eye_ref, o_ref, vt_scr, vct_scr, *, n_lat_tiles,
               bounded):
    t = pl.program_id(2)
    tq = o_ref.shape[0]
    eye = eye_ref[...]
    rows_per_tile = tq // GRID_W
    band_tiles = NA_KEY_ROWS // rows_per_tile
    heads = range(BRANCH_HEADS)

    @pl.when(t == 0)
    def _():
        _stage_values_t(eye, vl_ref, vt_scr, n_lat_tiles, tq)
        _stage_values_t(eye, vc_ref, vct_scr, 1, vc_ref.shape[0])

    subs = tuple((h // 2, (h % 2) * HEAD_DIM, (h % 2 + 1) * HEAD_DIM, h // 2, h) for h in heads)

    def attend(blocks):
        slabs = _q_slabs_t(eye, q_ref)
        q_t = [_keep_rows(slabs[qs], lo, hi) for (qs, lo, hi, _, _) in subs]
        outs = _attend([(k_ref, rows, v_t, q_t, bias) for k_ref, rows, v_t, bias in blocks], subs, bounded)
        _store_heads(eye, o_ref, outs)

    ctx_block = (kc_ref, slice(None), vct_scr[0], None)

    @pl.when(t < n_lat_tiles)
    def _():
        tile0 = jnp.clip(t - NA_WIN_H // 2 // rows_per_tile, 0, n_lat_tiles - band_tiles)
        rows = pl.ds(pl.multiple_of(tile0 * tq, tq), band_tiles * tq)
        v_t = jnp.concatenate([vt_scr[tile0 + j] for j in range(band_tiles)], axis=1)
        attend([(kl_ref, rows, v_t, ab_ref), ctx_block])

    @pl.when(t >= n_lat_tiles)
    def _():
        attend([ctx_block])


def _na_call(operands, out_rows, *, batch, seq, ctx_len, with_ctx_queries, bounded):
    q, k, v, ab = operands
    tq = Q_TILE
    n_lat = seq // tq
    nqt = n_lat + (1 if with_ctx_queries else 0)
    ctx_row0 = batch * seq // ctx_len
    n_keys = NA_KEY_ROWS * GRID_W
    width = BRANCH_WIDTH

    def qrow(b, t):
        return jnp.where(t < n_lat, b * n_lat + t, batch * n_lat + b)

    def pattern(t):
        return jnp.where(t == 0, 0, jnp.where(t >= n_lat - 1, 2, 1))

    return pl.pallas_call(
        functools.partial(_na_kernel, n_lat_tiles=n_lat, bounded=bounded),
        grid=(batch, 1, nqt),
        in_specs=[
            pl.BlockSpec((tq, width), lambda b, blk, t: (qrow(b, t), 0)),
            pl.BlockSpec((seq, width), lambda b, blk, t: (b, 0)),
            pl.BlockSpec((ctx_len, width), lambda b, blk, t: (ctx_row0 + b, 0)),
            pl.BlockSpec((seq, width), lambda b, blk, t: (b, 0)),
            pl.BlockSpec((ctx_len, width), lambda b, blk, t: (ctx_row0 + b, 0)),
            pl.BlockSpec((None, BRANCH_HEADS, n_keys, tq), lambda b, blk, t: (pattern(t), 0, 0, 0)),
            pl.BlockSpec((tq, tq), lambda b, blk, t: (0, 0)),
        ],
        out_specs=pl.BlockSpec((tq, width), lambda b, blk, t: (qrow(b, t), 0)),
        out_shape=jax.ShapeDtypeStruct((out_rows, width), BF16),
        scratch_shapes=[pltpu.VMEM((n_lat, width, tq), BF16), pltpu.VMEM((1, width, ctx_len), BF16)],
        compiler_params=_cparams(("arbitrary", "arbitrary", "arbitrary")),
        name="neighbourhood_attention",
    )(q, k, k, v, v, ab, jnp.eye(tq, dtype=BF16))


def _merge_kernel(x_ref, mod_ref, n_ref, o0_ref, o1_ref, o2_ref, o3_ref, wg_ref, bg_ref, wb_ref, wo_ref,
                  out_ref, y_scr):
    n = n_ref[...]
    branch = [o_ref[...] for o_ref in (o0_ref, o1_ref, o2_ref, o3_ref)]
    for c0 in range(0, n.shape[1], MERGE_SLAB):
        cols = slice(c0, c0 + MERGE_SLAB)
        y = None
        for j in range(N_BRANCH):
            gate = jax.nn.sigmoid(_dot(n, wg_ref[j, :, cols]) + bg_ref[j, :, cols])
            term = gate * _dot(branch[j], wb_ref[j, :, cols])
            y = term if y is None else y + term
        y_scr[:, cols] = y.astype(BF16)
    out_ref[...] = x_ref[...] + mod_ref[5:6, :] * _dot(y_scr[...], wo_ref[...])


def _merge_call(x, mod, n, outs, w_gate, b_gate, w_branch, w_out, layer, n_tiles, mod_idx):
    d = x.shape[1]
    tm = TOKEN_TILE
    row = lambda i: (i, 0)
    return pl.pallas_call(
        _merge_kernel,
        grid=(n_tiles,),
        in_specs=[
            pl.BlockSpec((tm, d), row),
            pl.BlockSpec((None, N_MOD, d), lambda i: (mod_idx(i), 0, 0)),
            pl.BlockSpec((tm, d), row),
        ] + [pl.BlockSpec((tm, BRANCH_WIDTH), row)] * 4 + [
            _resident((None, N_BRANCH, d, d), lambda i: (layer, 0, 0, 0)),
            _resident((None, N_BRANCH, 1, d), lambda i: (layer, 0, 0, 0)),
            _resident((None, N_BRANCH, BRANCH_WIDTH, d), lambda i: (layer, 0, 0, 0)),
            _resident((None, d, d), lambda i: (layer, 0, 0)),
        ],
        out_specs=pl.BlockSpec((tm, d), row),
        out_shape=jax.ShapeDtypeStruct((n_tiles * tm, d), F32),
        scratch_shapes=[pltpu.VMEM((tm, d), BF16)],
        compiler_params=_cparams(("arbitrary",)),
        name="branch_merge",
    )(x, mod, n, *outs, w_gate, b_gate, w_branch, w_out)


def _rope_tables(seq, rot_dim, pad_rows):
    t = jnp.arange(seq, dtype=jnp.int32)
    row = (t // GRID_W).astype(F32)
    col = (t % GRID_W).astype(F32)
    n_freq = rot_dim // 4
    inv = ROPE_THETA ** (-jnp.arange(n_freq, dtype=F32) / n_freq)
    ar = row[:, None] * inv[None, :]
    ac = col[:, None] * inv[None, :]
    ang = jnp.concatenate([ar, ar, ac, ac], axis=-1)
    sign = jnp.tile(jnp.concatenate([-jnp.ones((n_freq,), F32), jnp.ones((n_freq,), F32)]), 2)
    cos = jnp.concatenate([jnp.cos(ang), jnp.ones((pad_rows, rot_dim), F32)], axis=0)
    sin = jnp.concatenate([jnp.sin(ang) * sign[None, :], jnp.zeros((pad_rows, rot_dim), F32)], axis=0)
    return cos, sin


def _all_rope_tables(seq, pad_rows):
    c64, s64 = _rope_tables(seq, HEAD_DIM, pad_rows)
    c32, s32 = _rope_tables(seq, DIFF_QK_DIM, pad_rows)
    rows = seq + pad_rows
    ones = lambda w: jnp.ones((rows, w), F32)
    zeros = lambda w: jnp.zeros((rows, w), F32)
    pad = LANES - MLA_QK_DIM
    cm = jnp.concatenate([ones(MLA_NOPE_DIM), c32, ones(pad)], axis=-1)
    sm = jnp.concatenate([zeros(MLA_NOPE_DIM), s32, zeros(pad)], axis=-1)
    return (jnp.tile(c64, (1, 2)), jnp.tile(s64, (1, 2)), jnp.tile(c32, (1, 4)), jnp.tile(s32, (1, 4)), cm, sm)


def _block_diag_ones(width, seg):
    idx = np.arange(width) // seg
    return jnp.asarray(idx[:, None] == idx[None, :], dtype=BF16)


def _na_bias_tables(rpb, grid_rows):
    rows_per_tile = Q_TILE // GRID_W
    wh = min(NA_WIN_H, grid_rows)
    kinds = ((0, 0), (rows_per_tile, 0), (grid_rows - rows_per_tile, grid_rows - NA_KEY_ROWS))
    col = np.arange(GRID_W)
    c0 = np.clip(col - NA_WIN_W // 2, 0, GRID_W - NA_WIN_W)
    col_ok = (col[:, None] >= c0[None, :]) & (col[:, None] < c0[None, :] + NA_WIN_W)
    d_col = np.clip(col[:, None] - col[None, :], -(NA_WIN_W - 1), NA_WIN_W - 1) + (NA_WIN_W - 1)
    col_hot = (d_col[..., None] == np.arange(2 * NA_WIN_W - 1)).astype(np.float32)
    row_hot, valid = [], []
    for q_row0, k_row0 in kinds:
        r = q_row0 + np.arange(rows_per_tile)
        kr = k_row0 + np.arange(NA_KEY_ROWS)
        r0 = np.clip(r - wh // 2, 0, grid_rows - wh)
        row_ok = (kr[:, None] >= r0[None, :]) & (kr[:, None] < r0[None, :] + wh)
        d_row = kr[:, None] - r[None, :] + (NA_WIN_H - 1)
        row_hot.append((d_row[..., None] == np.arange(2 * NA_WIN_H - 1)).astype(np.float32))
        valid.append(row_ok[:, None, :, None] & col_ok[None, :, None, :])
    bias = jnp.einsum("nkqa,lhab,dcb->lnhkdqc", jnp.asarray(np.stack(row_hot)), rpb.astype(F32),
                      jnp.asarray(col_hot), precision=lax.Precision.HIGHEST)
    ab = jnp.where(jnp.asarray(np.stack(valid))[None, :, None], bias * LOG2E, NEG_BIG)
    depth = rpb.shape[0]
    return ab.reshape(depth, 3, BRANCH_HEADS, NA_KEY_ROWS * GRID_W, Q_TILE)


def _pad_heads(w, head_w, to_w):
    lead = w.shape[:-1]
    w = w.reshape(lead + (BRANCH_HEADS, head_w))
    w = jnp.pad(w, [(0, 0)] * len(lead) + [(0, 0), (0, to_w - head_w)])
    return w.reshape(lead + (BRANCH_HEADS * to_w,))


def _gqa_head_order(a, axis):
    parts = jnp.split(a, BRANCH_HEADS, axis=axis)
    return jnp.concatenate([parts[0], parts[2], parts[1], parts[3]], axis=axis)


def kernel(x, c, ctx, c_ctx, w_ada, b_ada, norm_g, ffn_w1, ffn_w3, ffn_w2, w_in, na_g_q, na_g_k, na_rpb, gqa_g_q, gqa_g_k, diff_g_q, diff_g_k, diff_lambda, diff_g_sub, mla_g_cq, mla_g_ckv, mla_w_uq, mla_w_ukv, mla_g_q, mla_g_k, w_gate, b_gate, w_branch, w_out):
    batch, seq, d = x.shape
    ctx_len = ctx.shape[1]
    depth = w_ada.shape[0]
    tm = TOKEN_TILE
    assert d == D_MODEL and ctx_len == Q_TILE and seq % tm == 0 and (batch * ctx_len) % tm == 0
    assert seq % min(KEY_CHUNK, seq) == 0 and seq // GRID_W >= NA_KEY_ROWS and batch < 16
    n_lat_rows = batch * seq
    lat_tiles = n_lat_rows // tm
    all_tiles = lat_tiles + batch * ctx_len // tm
    tiles_per_batch = seq // tm

    def mod_idx(i):
        return jnp.where(i < lat_tiles, i // tiles_per_batch, batch)

    def tab_idx(i):
        return jnp.where(i < lat_tiles, i % tiles_per_batch, tiles_per_batch)

    c16 = jnp.concatenate([c, c_ctx[None, :], jnp.zeros((16 - batch - 1, d), F32)], axis=0)
    mods = _mod_call(c16, w_ada, b_ada).reshape(depth, 16, N_MOD, d)
    w1, w3, w2 = ffn_w1.astype(BF16), ffn_w3.astype(BF16), ffn_w2.astype(BF16)
    na_w, gq_w, df_w, ml_w = jnp.split(w_in, [768, 1280, 2048], axis=-1)
    w_in_p = jnp.concatenate([
        na_w, _gqa_head_order(gq_w[..., :256], -1), gq_w[..., 256:], df_w, ml_w,
        jnp.zeros((depth, d, IN_COLS_PAD - w_in.shape[-1]), F32)], axis=-1).astype(BF16)
    wuq = _pad_heads(mla_w_uq, MLA_QK_DIM, LANES).astype(BF16)
    wukv = mla_w_ukv.reshape(depth, MLA_KV_RANK, BRANCH_HEADS, MLA_NOPE_DIM + HEAD_DIM)
    wuk = _pad_heads(wukv[..., :MLA_NOPE_DIM].reshape(depth, MLA_KV_RANK, -1), MLA_NOPE_DIM, LANES).astype(BF16)
    wuv = wukv[..., MLA_NOPE_DIM:].reshape(depth, MLA_KV_RANK, BRANCH_WIDTH).astype(BF16)

    def vec_row(v, reps=1, scale=1.0):
        v = jnp.tile(v, (1, reps)) * scale
        return jnp.pad(v, ((0, 0), (0, 512 - v.shape[-1])))

    pad_g = lambda g: jnp.pad(g, ((0, 0), (0, LANES - MLA_QK_DIM)))
    vec_rows = [
        vec_row(na_g_q, 4, HEAD_DIM ** -0.5 * LOG2E), vec_row(na_g_k, 4),
        vec_row(gqa_g_q, 4, HEAD_DIM ** -0.5 * LOG2E), vec_row(gqa_g_k, 2),
        vec_row(diff_g_q, 8, DIFF_QK_DIM ** -0.5 * LOG2E), vec_row(diff_g_k, 8),
        vec_row(mla_g_cq), vec_row(mla_g_ckv),
        vec_row(pad_g(mla_g_q), 4, MLA_QK_DIM ** -0.5 * LOG2E), vec_row(pad_g(mla_g_k), 4),
    ]
    vecs = jnp.stack(vec_rows + [jnp.zeros_like(vec_rows[0])] * (16 - len(vec_rows)), axis=1)
    bds = (_block_diag_ones(256, HEAD_DIM), _block_diag_ones(256, DIFF_QK_DIM), _block_diag_ones(256, LANES))
    tables = _all_rope_tables(seq, tm)
    na_ab = _na_bias_tables(na_rpb, seq // GRID_W)
    gsub = jnp.broadcast_to(diff_g_sub[:, :, None], (depth, HEAD_DIM, Q_TILE))
    wg, wo = w_gate.astype(BF16), w_out.astype(BF16)
    wb = jnp.concatenate([w_branch[:, 0:1], _gqa_head_order(w_branch[:, 1:2], 2), w_branch[:, 2:]], axis=1).astype(BF16)
    bg = b_gate.reshape(depth, N_BRANCH, 1, d)

    xa = jnp.concatenate([x.reshape(n_lat_rows, d), ctx.reshape(batch * ctx_len, d)], axis=0)

    gqa_subs = ((0, 0, 64, 0, 0), (0, 64, 128, 0, 1), (1, 0, 64, 0, 0), (1, 64, 128, 0, 1))
    diff_subs = ((0, 0, 32, 0, 0), (0, 32, 64, 0, 0), (0, 64, 96, 0, 1), (0, 96, 128, 0, 1))
    mla_subs = tuple((h, 0, LANES, h, h) for h in range(BRANCH_HEADS))

    for i in range(depth):
        need_ctx = i < depth - 1
        mod = mods[i]
        g = norm_g[i].reshape(3, 1, d)
        xa = _ffn_call(xa, mod, g[0], w1, w3, w2, i, 0, 0, all_tiles, mod_idx)
        (n, naq, nak, nav, gqr, gqn, gk, gv, dqr, dqn, dk, dv, mqr, mqn, mk, mv) = _proj_call(
            xa, mod, g[1], w_in_p, vecs, bds, wuq, wuk, wuv, tables, i, all_tiles, mod_idx, tab_idx)
        out_rows = xa.shape[0] if need_ctx else n_lat_rows
        common = dict(batch=batch, seq=seq, ctx_len=ctx_len, with_ctx_queries=need_ctx)
        lam_init = 0.8 - 0.6 * math.exp(-0.3 * i)
        def dispatch(call, bound, *operands, **kw):
            return lax.cond(bound <= SCORE_BOUND,
                            lambda *a: call(a, out_rows, bounded=True, **kw, **common),
                            lambda *a: call(a, out_rows, bounded=False, **kw, **common), *operands)

        joint = functools.partial(dispatch, _joint_call)
        na_bound = _score_bound(na_g_q[i], na_g_k[i], HEAD_DIM) + LOG2E * jnp.max(jnp.abs(na_rpb[i]))

        outs = [
            dispatch(_na_call, na_bound, naq, nak, nav, na_ab[i]),
            joint(_score_bound(gqa_g_q[i], gqa_g_k[i], HEAD_DIM), gqr, gqn, gk, gv,
                  subs=gqa_subs, n_blocks=1, name="gqa_attention"),
            joint(_score_bound(diff_g_q[i], diff_g_k[i], DIFF_QK_DIM), dqr, dqn, dk, dv, diff_lambda[i], gsub[i],
                  subs=diff_subs, n_blocks=2, name="diff_attention", diff_lam_init=lam_init),
            joint(_score_bound(mla_g_q[i], mla_g_k[i], MLA_QK_DIM), mqr, mqn, mk, mv,
                  subs=mla_subs, n_blocks=1, name="mla_attention"),
        ]
        n_tiles = all_tiles if need_ctx else lat_tiles
        xa = _merge_call(xa, mod, n, outs, wg, bg, wb, wo, i, n_tiles, mod_idx)
        xa = _ffn_call(xa, mod, g[2], w1, w3, w2, i, 1, 6, n_tiles, mod_idx)
    return xa.reshape(batch, seq, d)
```

```python
import functools
import math

import jax
import jax.numpy as jnp
import numpy as np
from jax import lax
from jax.experimental import pallas as pl
from jax.experimental.pallas import tpu as pltpu

F32 = jnp.float32
BF16 = jnp.bfloat16

D_MODEL = 1024
GRID_W = 64
ROPE_THETA = 10000.0
RMS_EPS = 1e-6
HEAD_DIM = 64
N_BRANCH = 4
BRANCH_HEADS = 4
BRANCH_WIDTH = 256
NA_WIN_H = 8
NA_WIN_W = 16
DIFF_QK_DIM = 32
MLA_Q_RANK = 256
MLA_KV_RANK = 128
MLA_NOPE_DIM = 64
MLA_ROPE_DIM = 32
MLA_QK_DIM = 96
D_FF = 2816
N_MOD = 9
LOG2E = 1.4426950408889634

LANES = 128
TOKEN_TILE = 512
Q_TILE = 256
KEY_CHUNK = 4096
FF_SLAB = 256
MERGE_SLAB = 256
PROJ_ROWS = 256
NA_KEY_ROWS = 12
NEG_BIG = -1e30
SCORE_BOUND = 40.0
VMEM_LIMIT = 56 * 1024 * 1024

C_NA_Q, C_NA_K, C_NA_V = 0, 256, 512
C_GQ_Q, C_GQ_K, C_GQ_V = 768, 1024, 1152
C_DF_Q, C_DF_K, C_DF_V = 1280, 1536, 1792
C_ML_CQ, C_ML_CKV, C_ML_ROPE = 2048, 2304, 2432
IN_COLS_PAD = 2560


def _cparams(sem):
    return pltpu.CompilerParams(dimension_semantics=sem, vmem_limit_bytes=VMEM_LIMIT)


def _dot(a, b):
    return jnp.dot(a, b, preferred_element_type=F32)


def _dot_nt(a, b):
    return lax.dot_general(a, b, (((1,), (1,)), ((), ())), preferred_element_type=F32)


def _silu(a):
    return a * jax.nn.sigmoid(a)


def _modulated_norm(x, g, shift, scale):
    ms = jnp.mean(x * x, axis=-1, keepdims=True)
    return (x * lax.rsqrt(ms + RMS_EPS) * g) * (1.0 + scale) + shift


def _mod_kernel(c_ref, w_ref, b_ref, o_ref):
    c = c_ref[...]
    o_ref[...] = _dot(_silu(c).astype(BF16), w_ref[...].astype(BF16)) + b_ref[...]


def _mod_call(c16, w_ada, b_ada):
    depth, d, nd = w_ada.shape
    tn = 1024
    return pl.pallas_call(
        _mod_kernel,
        grid=(depth, nd // tn),
        in_specs=[
            pl.BlockSpec((16, d), lambda l, j: (0, 0)),
            pl.BlockSpec((None, d, tn), lambda l, j: (l, 0, j)),
            pl.BlockSpec((None, 1, tn), lambda l, j: (l, 0, j)),
        ],
        out_specs=pl.BlockSpec((None, 16, tn), lambda l, j: (l, 0, j)),
        out_shape=jax.ShapeDtypeStruct((depth, 16, nd), F32),
        compiler_params=_cparams(("arbitrary", "arbitrary")),
        name="adaln_mod",
    )(c16, w_ada, b_ada.reshape(depth, 1, nd))


def _ffn_kernel(x_ref, mod_ref, g_ref, w1_ref, w3_ref, w2_ref, o_ref, act_scr, *, row0):
    x = x_ref[...]
    h = _modulated_norm(x, g_ref[...], mod_ref[row0:row0 + 1, :], mod_ref[row0 + 1:row0 + 2, :]).astype(BF16)
    for f0 in range(0, D_FF, FF_SLAB):
        a = _dot(h, w1_ref[:, f0:f0 + FF_SLAB])
        b = _dot(h, w3_ref[:, f0:f0 + FF_SLAB])
        act_scr[:, f0:f0 + FF_SLAB] = (_silu(a) * b).astype(BF16)
    o_ref[...] = x + (0.5 * mod_ref[row0 + 2:row0 + 3, :]) * _dot(act_scr[...], w2_ref[...])


def _resident(block_shape, index_map):
    return pl.BlockSpec(block_shape, index_map, pipeline_mode=pl.Buffered(1))


def _ffn_call(x, mod, g, w1, w3, w2, layer, half, row0, n_tiles, mod_idx):
    d = x.shape[1]
    tm = TOKEN_TILE
    return pl.pallas_call(
        functools.partial(_ffn_kernel, row0=row0),
        grid=(n_tiles,),
        in_specs=[
            pl.BlockSpec((tm, d), lambda i: (i, 0)),
            pl.BlockSpec((None, None, N_MOD, d), lambda i: (layer, mod_idx(i), 0, 0)),
            _resident((1, d), lambda i: (0, 0)),
            _resident((None, None, d, D_FF), lambda i: (layer, half, 0, 0)),
            _resident((None, None, d, D_FF), lambda i: (layer, half, 0, 0)),
            _resident((None, None, D_FF, d), lambda i: (layer, half, 0, 0)),
        ],
        out_specs=pl.BlockSpec((tm, d), lambda i: (i, 0)),
        out_shape=jax.ShapeDtypeStruct((n_tiles * tm, d), F32),
        scratch_shapes=[pltpu.VMEM((tm, D_FF), BF16)],
        compiler_params=_cparams(("arbitrary",)),
        name="ffn_half_step",
    )(x, mod, g, w1, w3, w2)


def _segnorm(x, bd, seg_len, gain):
    x2 = x * x
    hi = x2.astype(BF16)
    lo = (x2 - hi.astype(F32)).astype(BF16)
    groups = []
    for c0 in range(0, x.shape[1], bd.shape[0]):
        w = min(bd.shape[0], x.shape[1] - c0)
        groups.append(_dot(hi[:, c0:c0 + w], bd[0:w, 0:w]) + _dot(lo[:, c0:c0 + w], bd[0:w, 0:w]))
    ss = groups[0] if len(groups) == 1 else jnp.concatenate(groups, axis=-1)
    return x * lax.rsqrt(ss * (1.0 / seg_len) + RMS_EPS) * gain


def _tile_lanes(t, width):
    reps = width // t.shape[-1]
    return t if reps == 1 else jnp.concatenate([t] * reps, axis=-1)


def _rope(x, cos, sin_signed, quarter):
    width = x.shape[-1]
    lane = lax.broadcasted_iota(jnp.int32, x.shape, 1)
    first = (lane % (2 * quarter)) < quarter
    rot = jnp.where(first, pltpu.roll(x, width - quarter, 1), pltpu.roll(x, quarter, 1))
    return x * _tile_lanes(cos, width) + rot * _tile_lanes(sin_signed, width)


def _proj_kernel(x_ref, mod_ref, g_ref, w_ref, vec_ref, bd64_ref, bd32_ref, bd128_ref,
                 wuq_ref, wuk_ref, wuv_ref, c64_ref, s64_ref, c32_ref, s32_ref, cm_ref, sm_ref,
                 n_ref, naq_ref, nak_ref, nav_ref,
                 gqr_ref, gqn_ref, gk_ref, gv_ref,
                 dqr_ref, dqn_ref, dk_ref, dv_ref,
                 mqr_ref, mqn_ref, mk_ref, mv_ref):
    bd64, bd32, bd128 = bd64_ref[...], bd32_ref[...], bd128_ref[...]

    def vec(row, width):
        return vec_ref[row:row + 1, 0:width]

    groups = [slice(r, r + PROJ_ROWS) for r in range(0, x_ref.shape[0], PROJ_ROWS)]
    projected = []
    for rows in groups:
        n = _modulated_norm(x_ref[rows, :], g_ref[...], mod_ref[3:4, :], mod_ref[4:5, :]).astype(BF16)
        n_ref[rows, :] = n
        projected.append(_dot(n, w_ref[...]))

    for rows, p in zip(groups, projected):
        c64, s64 = c64_ref[rows, :], s64_ref[rows, :]
        c32, s32 = c32_ref[rows, :], s32_ref[rows, :]
        cm, sm = cm_ref[rows, :], sm_ref[rows, :]

        naq_ref[rows, :] = _segnorm(p[:, C_NA_Q:C_NA_Q + 256], bd64, HEAD_DIM, vec(0, 256)).astype(BF16)
        nak_ref[rows, :] = _segnorm(p[:, C_NA_K:C_NA_K + 256], bd64, HEAD_DIM, vec(1, 256)).astype(BF16)
        nav_ref[rows, :] = p[:, C_NA_V:C_NA_V + 256].astype(BF16)

        q = _segnorm(p[:, C_GQ_Q:C_GQ_Q + 256], bd64, HEAD_DIM, vec(2, 256))
        gqn_ref[rows, :] = q.astype(BF16)
        gqr_ref[rows, :] = _rope(q, c64, s64, HEAD_DIM // 4).astype(BF16)
        k = _segnorm(p[:, C_GQ_K:C_GQ_K + 128], bd64, HEAD_DIM, vec(3, 128))
        gk_ref[rows, :] = _rope(k, c64, s64, HEAD_DIM // 4).astype(BF16)
        gv_ref[rows, :] = p[:, C_GQ_V:C_GQ_V + 128].astype(BF16)

        q = _segnorm(p[:, C_DF_Q:C_DF_Q + 256], bd32, DIFF_QK_DIM, vec(4, 256))
        dqn_ref[rows, :] = q.astype(BF16)
        dqr_ref[rows, :] = _rope(q, c32, s32, DIFF_QK_DIM // 4).astype(BF16)
        k = _segnorm(p[:, C_DF_K:C_DF_K + 256], bd32, DIFF_QK_DIM, vec(5, 256))
        dk_ref[rows, :] = _rope(k, c32, s32, DIFF_QK_DIM // 4).astype(BF16)
        dv_ref[rows, :] = p[:, C_DF_V:C_DF_V + 256].astype(BF16)

        cq = p[:, C_ML_CQ:C_ML_CQ + 256]
        cq = cq * lax.rsqrt(jnp.mean(cq * cq, axis=-1, keepdims=True) + RMS_EPS) * vec(6, 256)
        q = _segnorm(_dot(cq.astype(BF16), wuq_ref[...]), bd128, MLA_QK_DIM, vec(8, 512))
        mqn_ref[rows, :] = q.astype(BF16)
        mqr_ref[rows, :] = _rope(q, cm, sm, MLA_ROPE_DIM // 4).astype(BF16)
        ckv = p[:, C_ML_CKV:C_ML_CKV + 128]
        ckv = (ckv * lax.rsqrt(jnp.mean(ckv * ckv, axis=-1, keepdims=True) + RMS_EPS) * vec(7, 128)).astype(BF16)
        k_rope = pltpu.roll(p[:, C_ML_ROPE:C_ML_ROPE + 128], MLA_NOPE_DIM, 1)
        k = _dot(ckv, wuk_ref[...]) + _tile_lanes(k_rope, 512)
        k = _segnorm(k, bd128, MLA_QK_DIM, vec(9, 512))
        mk_ref[rows, :] = _rope(k, cm, sm, MLA_ROPE_DIM // 4).astype(BF16)
        mv_ref[rows, :] = _dot(ckv, wuv_ref[...]).astype(BF16)


_PROJ_OUT_WIDTHS = (1024, 256, 256, 256, 256, 256, 128, 128, 256, 256, 256, 256, 512, 512, 512, 256)


def _proj_call(x, mod, g, w_in, vecs, bds, wuq, wuk, wuv, tables, layer, n_tiles, mod_idx, tab_idx):
    d = x.shape[1]
    tm = TOKEN_TILE
    const2 = lambda i: (0, 0)
    in_specs = [
        pl.BlockSpec((tm, d), lambda i: (i, 0)),
        pl.BlockSpec((None, None, N_MOD, d), lambda i: (layer, mod_idx(i), 0, 0)),
        pl.BlockSpec((1, d), const2),
        pl.BlockSpec((None, d, IN_COLS_PAD), lambda i: (layer, 0, 0)),
        pl.BlockSpec((None, 16, 512), lambda i: (layer, 0, 0)),
        pl.BlockSpec((256, 256), const2),
        pl.BlockSpec((256, 256), const2),
        pl.BlockSpec((256, 256), const2),
        pl.BlockSpec((None, 256, 512), lambda i: (layer, 0, 0)),
        pl.BlockSpec((None, 128, 512), lambda i: (layer, 0, 0)),
        pl.BlockSpec((None, 128, 256), lambda i: (layer, 0, 0)),
    ] + [pl.BlockSpec((tm, LANES), lambda i: (tab_idx(i), 0))] * 6
    rows = n_tiles * tm
    return pl.pallas_call(
        _proj_kernel,
        grid=(n_tiles,),
        in_specs=in_specs,
        out_specs=[pl.BlockSpec((tm, w), lambda i: (i, 0)) for w in _PROJ_OUT_WIDTHS],
        out_shape=[jax.ShapeDtypeStruct((rows, w), BF16) for w in _PROJ_OUT_WIDTHS],
        compiler_params=_cparams(("arbitrary",)),
        name="mixer_projection",
    )(x, mod, g, w_in, vecs, *bds, wuq, wuk, wuv, *tables)


def _attend(blocks, subs, bounded):
    scores = []
    for i, sub in enumerate(subs):
        ks = sub[3]
        per_block = []
        for k_ref, rows, _, q_t, bias in blocks:
            s = _dot(k_ref[rows, ks * LANES:(ks + 1) * LANES], q_t[i])
            per_block.append(s if bias is None else s + bias[i])
        scores.append(per_block)
    outs = []
    for i, sub in enumerate(subs):
        if not bounded:
            m = functools.reduce(jnp.maximum, [jnp.max(s, axis=0, keepdims=True) for s in scores[i]])
        l = acc = None
        for s, (_, _, v_t, _, _) in zip(scores[i], blocks):
            p = jnp.exp2(s if bounded else s - m)
            l_blk = jnp.sum(p, axis=0, keepdims=True)
            acc_blk = _dot(_head_rows(v_t, sub[4]), p.astype(BF16))
            l, acc = (l_blk, acc_blk) if l is None else (l + l_blk, acc + acc_blk)
        outs.append(acc / l)
    return outs


def _score_bound(g_q, g_k, dim):
    return 1.01 * dim * (dim ** -0.5 * LOG2E) * jnp.max(jnp.abs(g_q)) * jnp.max(jnp.abs(g_k))


def _transpose_bf16(eye, a):
    return _dot_nt(eye[0:a.shape[1], 0:a.shape[1]], a).astype(BF16)


def _q_slabs_t(eye, q_ref):
    out = []
    for j0 in range(0, q_ref.shape[1], 2 * LANES):
        w = min(2 * LANES, q_ref.shape[1] - j0)
        q_t = _dot_nt(eye[0:w, 0:w], q_ref[:, j0:j0 + w]).astype(BF16)
        out += [q_t[r:r + LANES, :] for r in range(0, w, LANES)]
    return out


def _keep_rows(q_t, lo, hi):
    if lo == 0 and hi == q_t.shape[0]:
        return q_t
    row = lax.broadcasted_iota(jnp.int32, q_t.shape, 0)
    return jnp.where((row >= lo) & (row < hi), q_t, jnp.zeros_like(q_t))


def _head_rows(v_t, head):
    return v_t[head * HEAD_DIM:(head + 1) * HEAD_DIM, :]


def _store_heads(eye, o_ref, o_t_heads):
    for p in range(0, len(o_t_heads), 4):
        o_t = jnp.concatenate(o_t_heads[p:p + 4], axis=0).astype(BF16)
        o_ref[:, p * HEAD_DIM:p * HEAD_DIM + o_t.shape[0]] = _dot_nt(eye, o_t).astype(o_ref.dtype)


def _stage_values_t(eye, v_ref, vt_scr, n_chunks, chunk):
    for c in range(n_chunks):
        for j in range(v_ref.shape[1] // LANES):
            vt_scr[c, j * LANES:(j + 1) * LANES, :] = _transpose_bf16(
                eye, v_ref[c * chunk:(c + 1) * chunk, j * LANES:(j + 1) * LANES])


def _joint_kernel(*refs, subs, n_lat_tiles, seq, chunk, with_ctx_queries, diff_lam_init, bounded):
    if diff_lam_init is None:
        qr_ref, qn_ref, kl_ref, kc_ref, vl_ref, vc_ref, eye_ref, o_ref, vt_scr, vct_scr = refs
    else:
        qr_ref, qn_ref, kl_ref, kc_ref, vl_ref, vc_ref, eye_ref, lam_ref, gsub_ref, o_ref, vt_scr, vct_scr = refs
    t = pl.program_id(2)
    n_chunks = seq // chunk
    eye = eye_ref[...]

    @pl.when(t == 0)
    def _():
        _stage_values_t(eye, vl_ref, vt_scr, n_chunks, chunk)
        _stage_values_t(eye, vc_ref, vct_scr, 1, vc_ref.shape[0])

    def operands(q_ref):
        slabs = _q_slabs_t(eye, q_ref)
        return [_keep_rows(slabs[qs], lo, hi) for (qs, lo, hi, _, _) in subs]

    def finish(outs):
        if diff_lam_init is None:
            _store_heads(eye, o_ref, outs)
            return
        lv = lam_ref[...]
        lam = (jnp.exp(jnp.sum(lv[0:1, :] * lv[1:2, :], axis=-1, keepdims=True))
               - jnp.exp(jnp.sum(lv[2:3, :] * lv[3:4, :], axis=-1, keepdims=True)) + diff_lam_init)
        heads = []
        for o1, o2 in zip(outs[0::2], outs[1::2]):
            d = o1 - lam * o2
            ms = jnp.mean(d * d, axis=0, keepdims=True)
            heads.append((d * lax.rsqrt(ms + RMS_EPS) * gsub_ref[...]) * (1.0 - diff_lam_init))
        _store_heads(eye, o_ref, heads)

    def ctx_block():
        return (kc_ref, slice(None), vct_scr[0], operands(qn_ref), None)

    @pl.when(t < n_lat_tiles)
    def _():
        q_t = operands(qr_ref)
        latent = [(kl_ref, slice(c * chunk, (c + 1) * chunk), vt_scr[c], q_t, None) for c in range(n_chunks)]
        finish(_attend(latent + [ctx_block()], subs, bounded))

    if with_ctx_queries:
        @pl.when(t >= n_lat_tiles)
        def _():
            finish(_attend([ctx_block()], subs, bounded))


def _joint_call(operands, out_rows, *, subs, n_blocks, batch, seq, ctx_len, with_ctx_queries, name, bounded,
                diff_lam_init=None):
    qr, qn, k, v = operands[:4]
    tq = Q_TILE
    n_lat = seq // tq
    nqt = n_lat + (1 if with_ctx_queries else 0)
    ctx_row0 = batch * seq // ctx_len
    q_width, k_width, v_width = qr.shape[1] // n_blocks, k.shape[1] // n_blocks, v.shape[1] // n_blocks
    out_width = BRANCH_WIDTH // n_blocks
    chunk = min(KEY_CHUNK, seq)
    n_chunks = seq // chunk

    def qrow(b, t):
        return jnp.where(t < n_lat, b * n_lat + t, batch * n_lat + b)

    in_specs = [
        pl.BlockSpec((tq, q_width), lambda b, blk, t: (qrow(b, t), blk)),
        pl.BlockSpec((tq, q_width), lambda b, blk, t: (qrow(b, t), blk)),
        pl.BlockSpec((seq, k_width), lambda b, blk, t: (b, blk)),
        pl.BlockSpec((ctx_len, k_width), lambda b, blk, t: (ctx_row0 + b, blk)),
        pl.BlockSpec((seq, v_width), lambda b, blk, t: (b, blk)),
        pl.BlockSpec((ctx_len, v_width), lambda b, blk, t: (ctx_row0 + b, blk)),
        pl.BlockSpec((tq, tq), lambda b, blk, t: (0, 0)),
    ]
    args = [qr, qn, k, k, v, v, jnp.eye(tq, dtype=BF16)]
    if diff_lam_init is not None:
        in_specs += [pl.BlockSpec((4, DIFF_QK_DIM), lambda b, blk, t: (0, 0)),
                     pl.BlockSpec((HEAD_DIM, tq), lambda b, blk, t: (0, 0))]
        args += list(operands[4:])
    return pl.pallas_call(
        functools.partial(_joint_kernel, subs=subs, n_lat_tiles=n_lat, seq=seq, chunk=chunk,
                          with_ctx_queries=with_ctx_queries, diff_lam_init=diff_lam_init, bounded=bounded),
        grid=(batch, n_blocks, nqt),
        in_specs=in_specs,
        out_specs=pl.BlockSpec((tq, out_width), lambda b, blk, t: (qrow(b, t), blk)),
        out_shape=jax.ShapeDtypeStruct((out_rows, BRANCH_WIDTH), BF16),
        scratch_shapes=[pltpu.VMEM((n_chunks, v_width, chunk), BF16), pltpu.VMEM((1, v_width, ctx_len), BF16)],
        compiler_params=_cparams(("arbitrary", "arbitrary", "arbitrary")),
        name=name,
    )(*args)


def _na_kernel(q_ref, kl_ref, kc_ref, vl_ref, vc_ref, ab_ref, eye_ref, o_ref, vt_scr, vct_scr, *, n_lat_tiles,
               bounded):
    t = pl.program_id(2)
    tq = o_ref.shape[0]
    eye = eye_ref[...]
    rows_per_tile = tq // GRID_W
    band_tiles = NA_KEY_ROWS // rows_per_tile
    heads = range(BRANCH_HEADS)

    @pl.when(t == 0)
    def _():
        _stage_values_t(eye, vl_ref, vt_scr, n_lat_tiles, tq)
        _stage_values_t(eye, vc_ref, vct_scr, 1, vc_ref.shape[0])

    subs = tuple((h // 2, (h % 2) * HEAD_DIM, (h % 2 + 1) * HEAD_DIM, h // 2, h) for h in heads)

    def attend(blocks):
        slabs = _q_slabs_t(eye, q_ref)
        q_t = [_keep_rows(slabs[qs], lo, hi) for (qs, lo, hi, _, _) in subs]
        outs = _attend([(k_ref, rows, v_t, q_t, bias) for k_ref, rows, v_t, bias in blocks], subs, bounded)
        _store_heads(eye, o_ref, outs)

    ctx_block = (kc_ref, slice(None), vct_scr[0], None)

    @pl.when(t < n_lat_tiles)
    def _():
        tile0 = jnp.clip(t - NA_WIN_H // 2 // rows_per_tile, 0, n_lat_tiles - band_tiles)
        rows = pl.ds(pl.multiple_of(tile0 * tq, tq), band_tiles * tq)
        v_t = jnp.concatenate([vt_scr[tile0 + j] for j in range(band_tiles)], axis=1)
        attend([(kl_ref, rows, v_t, ab_ref), ctx_block])

    @pl.when(t >= n_lat_tiles)
    def _():
        attend([ctx_block])


def _na_call(operands, out_rows, *, layer, batch, seq, ctx_len, with_ctx_queries, bounded):
    q, k, v, ab = operands
    tq = Q_TILE
    n_lat = seq // tq
    nqt = n_lat + (1 if with_ctx_queries else 0)
    ctx_row0 = batch * seq // ctx_len
    n_keys = NA_KEY_ROWS * GRID_W
    width = BRANCH_WIDTH

    def qrow(b, t):
        return jnp.where(t < n_lat, b * n_lat + t, batch * n_lat + b)

    def pattern(t):
        return jnp.where(t == 0, 0, jnp.where(t >= n_lat - 1, 2, 1))

    return pl.pallas_call(
        functools.partial(_na_kernel, n_lat_tiles=n_lat, bounded=bounded),
        grid=(batch, 1, nqt),
        in_specs=[
            pl.BlockSpec((tq, width), lambda b, blk, t: (qrow(b, t), 0)),
            pl.BlockSpec((seq, width), lambda b, blk, t: (b, 0)),
            pl.BlockSpec((ctx_len, width), lambda b, blk, t: (ctx_row0 + b, 0)),
            pl.BlockSpec((seq, width), lambda b, blk, t: (b, 0)),
            pl.BlockSpec((ctx_len, width), lambda b, blk, t: (ctx_row0 + b, 0)),
            pl.BlockSpec((None, None, BRANCH_HEADS, n_keys, tq), lambda b, blk, t: (layer, pattern(t), 0, 0, 0)),
            pl.BlockSpec((tq, tq), lambda b, blk, t: (0, 0)),
        ],
        out_specs=pl.BlockSpec((tq, width), lambda b, blk, t: (qrow(b, t), 0)),
        out_shape=jax.ShapeDtypeStruct((out_rows, width), BF16),
        scratch_shapes=[pltpu.VMEM((n_lat, width, tq), BF16), pltpu.VMEM((1, width, ctx_len), BF16)],
        compiler_params=_cparams(("arbitrary", "arbitrary", "arbitrary")),
        name="neighbourhood_attention",
    )(q, k, k, v, v, ab, jnp.eye(tq, dtype=BF16))


def _merge_kernel(x_ref, mod_ref, n_ref, o0_ref, o1_ref, o2_ref, o3_ref, wg_ref, bg_ref, wb_ref, wo_ref,
                  out_ref, y_scr):
    n = n_ref[...]
    branch = [o_ref[...] for o_ref in (o0_ref, o1_ref, o2_ref, o3_ref)]
    for c0 in range(0, n.shape[1], MERGE_SLAB):
        cols = slice(c0, c0 + MERGE_SLAB)
        y = None
        for j in range(N_BRANCH):
            gate = jax.nn.sigmoid(_dot(n, wg_ref[j, :, cols]) + bg_ref[j, :, cols])
            term = gate * _dot(branch[j], wb_ref[j, :, cols])
            y = term if y is None else y + term
        y_scr[:, cols] = y.astype(BF16)
    out_ref[...] = x_ref[...] + mod_ref[5:6, :] * _dot(y_scr[...], wo_ref[...])


def _merge_call(x, mod, n, outs, w_gate, b_gate, w_branch, w_out, layer, n_tiles, mod_idx):
    d = x.shape[1]
    tm = TOKEN_TILE
    row = lambda i: (i, 0)
    return pl.pallas_call(
        _merge_kernel,
        grid=(n_tiles,),
        in_specs=[
            pl.BlockSpec((tm, d), row),
            pl.BlockSpec((None, None, N_MOD, d), lambda i: (layer, mod_idx(i), 0, 0)),
            pl.BlockSpec((tm, d), row),
        ] + [pl.BlockSpec((tm, BRANCH_WIDTH), row)] * 4 + [
            _resident((None, N_BRANCH, d, d), lambda i: (layer, 0, 0, 0)),
            _resident((None, N_BRANCH, 1, d), lambda i: (layer, 0, 0, 0)),
            _resident((None, N_BRANCH, BRANCH_WIDTH, d), lambda i: (layer, 0, 0, 0)),
            _resident((None, d, d), lambda i: (layer, 0, 0)),
        ],
        out_specs=pl.BlockSpec((tm, d), row),
        out_shape=jax.ShapeDtypeStruct((n_tiles * tm, d), F32),
        scratch_shapes=[pltpu.VMEM((tm, d), BF16)],
        compiler_params=_cparams(("arbitrary",)),
        name="branch_merge",
    )(x, mod, n, *outs, w_gate, b_gate, w_branch, w_out)


def _rope_tables(seq, rot_dim, pad_rows):
    t = jnp.arange(seq, dtype=jnp.int32)
    row = (t // GRID_W).astype(F32)
    col = (t % GRID_W).astype(F32)
    n_freq = rot_dim // 4
    inv = ROPE_THETA ** (-jnp.arange(n_freq, dtype=F32) / n_freq)
    ar = row[:, None] * inv[None, :]
    ac = col[:, None] * inv[None, :]
    ang = jnp.concatenate([ar, ar, ac, ac], axis=-1)
    sign = jnp.tile(jnp.concatenate([-jnp.ones((n_freq,), F32), jnp.ones((n_freq,), F32)]), 2)
    cos = jnp.concatenate([jnp.cos(ang), jnp.ones((pad_rows, rot_dim), F32)], axis=0)
    sin = jnp.concatenate([jnp.sin(ang) * sign[None, :], jnp.zeros((pad_rows, rot_dim), F32)], axis=0)
    return cos, sin


def _all_rope_tables(seq, pad_rows):
    c64, s64 = _rope_tables(seq, HEAD_DIM, pad_rows)
    c32, s32 = _rope_tables(seq, DIFF_QK_DIM, pad_rows)
    rows = seq + pad_rows
    ones = lambda w: jnp.ones((rows, w), F32)
    zeros = lambda w: jnp.zeros((rows, w), F32)
    pad = LANES - MLA_QK_DIM
    cm = jnp.concatenate([ones(MLA_NOPE_DIM), c32, ones(pad)], axis=-1)
    sm = jnp.concatenate([zeros(MLA_NOPE_DIM), s32, zeros(pad)], axis=-1)
    return (jnp.tile(c64, (1, 2)), jnp.tile(s64, (1, 2)), jnp.tile(c32, (1, 4)), jnp.tile(s32, (1, 4)), cm, sm)


def _block_diag_ones(width, seg):
    idx = np.arange(width) // seg
    return jnp.asarray(idx[:, None] == idx[None, :], dtype=BF16)


def _na_bias_tables(rpb, grid_rows):
    rows_per_tile = Q_TILE // GRID_W
    wh = min(NA_WIN_H, grid_rows)
    kinds = ((0, 0), (rows_per_tile, 0), (grid_rows - rows_per_tile, grid_rows - NA_KEY_ROWS))
    col = np.arange(GRID_W)
    c0 = np.clip(col - NA_WIN_W // 2, 0, GRID_W - NA_WIN_W)
    col_ok = (col[:, None] >= c0[None, :]) & (col[:, None] < c0[None, :] + NA_WIN_W)
    d_col = np.clip(col[:, None] - col[None, :], -(NA_WIN_W - 1), NA_WIN_W - 1) + (NA_WIN_W - 1)
    col_hot = (d_col[..., None] == np.arange(2 * NA_WIN_W - 1)).astype(np.float32)
    blocks = jnp.einsum("lhab,dcb->lhadc", rpb.astype(F32), jnp.asarray(col_hot), precision=lax.Precision.HIGHEST)
    blocks = jnp.where(jnp.asarray(col_ok)[None, None, None], blocks * LOG2E, NEG_BIG)
    masked = jnp.full(blocks.shape[:2] + (GRID_W, GRID_W), NEG_BIG, F32)
    tables = []
    for q_row0, k_row0 in kinds:
        r = q_row0 + np.arange(rows_per_tile)
        kr = k_row0 + np.arange(NA_KEY_ROWS)
        r0 = np.clip(r - wh // 2, 0, grid_rows - wh)
        key_rows = []
        for kk in range(NA_KEY_ROWS):
            in_window = (kr[kk] >= r0) & (kr[kk] < r0 + wh)
            key_rows.append(jnp.concatenate(
                [blocks[:, :, kr[kk] - r[qr] + (NA_WIN_H - 1)] if in_window[qr] else masked
                 for qr in range(rows_per_tile)], axis=-1))
        tables.append(jnp.concatenate(key_rows, axis=-2))
    return jnp.stack(tables, axis=1)


def _pad_heads(w, head_w, to_w):
    lead = w.shape[:-1]
    w = w.reshape(lead + (BRANCH_HEADS, head_w))
    w = jnp.pad(w, [(0, 0)] * len(lead) + [(0, 0), (0, to_w - head_w)])
    return w.reshape(lead + (BRANCH_HEADS * to_w,))


def _gqa_head_order(a, axis):
    parts = jnp.split(a, BRANCH_HEADS, axis=axis)
    return jnp.concatenate([parts[0], parts[2], parts[1], parts[3]], axis=axis)


def kernel(x, c, ctx, c_ctx, w_ada, b_ada, norm_g, ffn_w1, ffn_w3, ffn_w2, w_in, na_g_q, na_g_k, na_rpb, gqa_g_q, gqa_g_k, diff_g_q, diff_g_k, diff_lambda, diff_g_sub, mla_g_cq, mla_g_ckv, mla_w_uq, mla_w_ukv, mla_g_q, mla_g_k, w_gate, b_gate, w_branch, w_out):
    batch, seq, d = x.shape
    ctx_len = ctx.shape[1]
    depth = w_ada.shape[0]
    tm = TOKEN_TILE
    assert d == D_MODEL and ctx_len == Q_TILE and seq % tm == 0 and (batch * ctx_len) % tm == 0
    assert seq % min(KEY_CHUNK, seq) == 0 and seq // GRID_W >= NA_KEY_ROWS and batch < 16
    n_lat_rows = batch * seq
    lat_tiles = n_lat_rows // tm
    all_tiles = lat_tiles + batch * ctx_len // tm
    tiles_per_batch = seq // tm

    def mod_idx(i):
        return jnp.where(i < lat_tiles, i // tiles_per_batch, batch)

    def tab_idx(i):
        return jnp.where(i < lat_tiles, i % tiles_per_batch, tiles_per_batch)

    c16 = jnp.concatenate([c, c_ctx[None, :], jnp.zeros((16 - batch - 1, d), F32)], axis=0)
    mods = _mod_call(c16, w_ada, b_ada).reshape(depth, 16, N_MOD, d)
    w1, w3, w2 = ffn_w1.astype(BF16), ffn_w3.astype(BF16), ffn_w2.astype(BF16)
    na_w, gq_w, df_w, ml_w = jnp.split(w_in, [768, 1280, 2048], axis=-1)
    w_in_p = jnp.concatenate([
        na_w, _gqa_head_order(gq_w[..., :256], -1), gq_w[..., 256:], df_w, ml_w,
        jnp.zeros((depth, d, IN_COLS_PAD - w_in.shape[-1]), F32)], axis=-1).astype(BF16)
    wuq = _pad_heads(mla_w_uq, MLA_QK_DIM, LANES).astype(BF16)
    wukv = mla_w_ukv.reshape(depth, MLA_KV_RANK, BRANCH_HEADS, MLA_NOPE_DIM + HEAD_DIM)
    wuk = _pad_heads(wukv[..., :MLA_NOPE_DIM].reshape(depth, MLA_KV_RANK, -1), MLA_NOPE_DIM, LANES).astype(BF16)
    wuv = wukv[..., MLA_NOPE_DIM:].reshape(depth, MLA_KV_RANK, BRANCH_WIDTH).astype(BF16)

    def vec_row(v, reps=1, scale=1.0):
        v = jnp.tile(v, (1, reps)) * scale
        return jnp.pad(v, ((0, 0), (0, 512 - v.shape[-1])))

    pad_g = lambda g: jnp.pad(g, ((0, 0), (0, LANES - MLA_QK_DIM)))
    vec_rows = [
        vec_row(na_g_q, 4, HEAD_DIM ** -0.5 * LOG2E), vec_row(na_g_k, 4),
        vec_row(gqa_g_q, 4, HEAD_DIM ** -0.5 * LOG2E), vec_row(gqa_g_k, 2),
        vec_row(diff_g_q, 8, DIFF_QK_DIM ** -0.5 * LOG2E), vec_row(diff_g_k, 8),
        vec_row(mla_g_cq), vec_row(mla_g_ckv),
        vec_row(pad_g(mla_g_q), 4, MLA_QK_DIM ** -0.5 * LOG2E), vec_row(pad_g(mla_g_k), 4),
    ]
    vecs = jnp.stack(vec_rows + [jnp.zeros_like(vec_rows[0])] * (16 - len(vec_rows)), axis=1)
    bds = (_block_diag_ones(256, HEAD_DIM), _block_diag_ones(256, DIFF_QK_DIM), _block_diag_ones(256, LANES))
    tables = _all_rope_tables(seq, tm)
    na_ab = _na_bias_tables(na_rpb, seq // GRID_W)
    gsub = jnp.broadcast_to(diff_g_sub[:, :, None], (depth, HEAD_DIM, Q_TILE))
    wg, wo = w_gate.astype(BF16), w_out.astype(BF16)
    wb = jnp.concatenate([w_branch[:, 0:1], _gqa_head_order(w_branch[:, 1:2], 2), w_branch[:, 2:]], axis=1).astype(BF16)
    bg = b_gate.reshape(depth, N_BRANCH, 1, d)

    xa = jnp.concatenate([x.reshape(n_lat_rows, d), ctx.reshape(batch * ctx_len, d)], axis=0)

    gqa_subs = ((0, 0, 64, 0, 0), (0, 64, 128, 0, 1), (1, 0, 64, 0, 0), (1, 64, 128, 0, 1))
    diff_subs = ((0, 0, 32, 0, 0), (0, 32, 64, 0, 0), (0, 64, 96, 0, 1), (0, 96, 128, 0, 1))
    mla_subs = tuple((h, 0, LANES, h, h) for h in range(BRANCH_HEADS))

    for i in range(depth):
        need_ctx = i < depth - 1
        mod = mods
        g = norm_g[i].reshape(3, 1, d)
        xa = _ffn_call(xa, mod, g[0], w1, w3, w2, i, 0, 0, all_tiles, mod_idx)
        (n, naq, nak, nav, gqr, gqn, gk, gv, dqr, dqn, dk, dv, mqr, mqn, mk, mv) = _proj_call(
            xa, mod, g[1], w_in_p, vecs, bds, wuq, wuk, wuv, tables, i, all_tiles, mod_idx, tab_idx)
        out_rows = xa.shape[0] if need_ctx else n_lat_rows
        common = dict(batch=batch, seq=seq, ctx_len=ctx_len, with_ctx_queries=need_ctx)
        lam_init = 0.8 - 0.6 * math.exp(-0.3 * i)
        def dispatch(call, bound, *operands, **kw):
            return lax.cond(bound <= SCORE_BOUND,
                            lambda *a: call(a, out_rows, bounded=True, **kw, **common),
                            lambda *a: call(a, out_rows, bounded=False, **kw, **common), *operands)

        joint = functools.partial(dispatch, _joint_call)
        na_bound = _score_bound(na_g_q[i], na_g_k[i], HEAD_DIM) + LOG2E * jnp.max(jnp.abs(na_rpb[i]))

        outs = [
            dispatch(_na_call, na_bound, naq, nak, nav, na_ab, layer=i),
            joint(_score_bound(gqa_g_q[i], gqa_g_k[i], HEAD_DIM), gqr, gqn, gk, gv,
                  subs=gqa_subs, n_blocks=1, name="gqa_attention"),
            joint(_score_bound(diff_g_q[i], diff_g_k[i], DIFF_QK_DIM), dqr, dqn, dk, dv, diff_lambda[i], gsub[i],
                  subs=diff_subs, n_blocks=2, name="diff_attention", diff_lam_init=lam_init),
            joint(_score_bound(mla_g_q[i], mla_g_k[i], MLA_QK_DIM), mqr, mqn, mk, mv,
                  subs=mla_subs, n_blocks=1, name="mla_attention"),
        ]
        n_tiles = all_tiles if need_ctx else lat_tiles
        xa = _merge_call(xa, mod, n, outs, wg, bg, wb, wo, i, n_tiles, mod_idx)
        xa = _ffn_call(xa, mod, g[2], w1, w3, w2, i, 1, 6, n_tiles, mod_idx)
    return xa.reshape(batch, seq, d)
```

```python
import functools
import math

import jax
import jax.numpy as jnp
import numpy as np
from jax import lax
from jax.experimental import pallas as pl
from jax.experimental.pallas import tpu as pltpu

F32 = jnp.float32
BF16 = jnp.bfloat16

D_MODEL = 1024
GRID_W = 64
ROPE_THETA = 10000.0
RMS_EPS = 1e-6
HEAD_DIM = 64
N_BRANCH = 4
BRANCH_HEADS = 4
BRANCH_WIDTH = 256
NA_WIN_H = 8
NA_WIN_W = 16
DIFF_QK_DIM = 32
MLA_Q_RANK = 256
MLA_KV_RANK = 128
MLA_NOPE_DIM = 64
MLA_ROPE_DIM = 32
MLA_QK_DIM = 96
D_FF = 2816
N_MOD = 9
LOG2E = 1.4426950408889634

LANES = 128
TOKEN_TILE = 512
Q_TILE = 256
KEY_CHUNK = 4096
FF_SLAB = 256
MERGE_SLAB = 256
PROJ_ROWS = 256
NA_KEY_ROWS = 12
NEG_BIG = -1e30
SCORE_BOUND = 40.0
VMEM_LIMIT = 56 * 1024 * 1024

C_NA_Q, C_NA_K, C_NA_V = 0, 256, 512
C_GQ_Q, C_GQ_K, C_GQ_V = 768, 1024, 1152
C_DF_Q, C_DF_K, C_DF_V = 1280, 1536, 1792
C_ML_CQ, C_ML_CKV, C_ML_ROPE = 2048, 2304, 2432
IN_COLS_PAD = 2560


def _cparams(sem):
    return pltpu.CompilerParams(dimension_semantics=sem, vmem_limit_bytes=VMEM_LIMIT)


def _dot(a, b):
    return jnp.dot(a, b, preferred_element_type=F32)


def _dot_nt(a, b):
    return lax.dot_general(a, b, (((1,), (1,)), ((), ())), preferred_element_type=F32)


def _silu(a):
    return a * jax.nn.sigmoid(a)


def _modulated_norm(x, g, shift, scale):
    ms = jnp.mean(x * x, axis=-1, keepdims=True)
    return (x * lax.rsqrt(ms + RMS_EPS) * g) * (1.0 + scale) + shift


def _mod_kernel(c_ref, w_ref, b_ref, o_ref):
    c = c_ref[...]
    o_ref[...] = _dot(_silu(c).astype(BF16), w_ref[...].astype(BF16)) + b_ref[...]


def _mod_call(c16, w_ada, b_ada):
    depth, d, nd = w_ada.shape
    tn = 1024
    return pl.pallas_call(
        _mod_kernel,
        grid=(depth, nd // tn),
        in_specs=[
            pl.BlockSpec((16, d), lambda l, j: (0, 0)),
            pl.BlockSpec((None, d, tn), lambda l, j: (l, 0, j)),
            pl.BlockSpec((None, 1, tn), lambda l, j: (l, 0, j)),
        ],
        out_specs=pl.BlockSpec((None, 16, tn), lambda l, j: (l, 0, j)),
        out_shape=jax.ShapeDtypeStruct((depth, 16, nd), F32),
        compiler_params=_cparams(("arbitrary", "arbitrary")),
        name="adaln_mod",
    )(c16, w_ada, b_ada.reshape(depth, 1, nd))


def _ffn_kernel(*refs, row0, head_tiles):
    if head_tiles is None:
        x_ref, mod_ref, g_ref, w1_ref, w3_ref, w2_ref, o_ref, act_scr = refs
        x = x_ref[...]
    else:
        x_ref, tail_ref, mod_ref, g_ref, w1_ref, w3_ref, w2_ref, o_ref, act_scr = refs
        x = jnp.where(pl.program_id(0) < head_tiles, x_ref[...], tail_ref[...])
    h = _modulated_norm(x, g_ref[...], mod_ref[row0:row0 + 1, :], mod_ref[row0 + 1:row0 + 2, :]).astype(BF16)
    for f0 in range(0, D_FF, FF_SLAB):
        a = _dot(h, w1_ref[:, f0:f0 + FF_SLAB])
        b = _dot(h, w3_ref[:, f0:f0 + FF_SLAB])
        act_scr[:, f0:f0 + FF_SLAB] = (_silu(a) * b).astype(BF16)
    o_ref[...] = x + (0.5 * mod_ref[row0 + 2:row0 + 3, :]) * _dot(act_scr[...], w2_ref[...])


def _resident(block_shape, index_map):
    return pl.BlockSpec(block_shape, index_map, pipeline_mode=pl.Buffered(1))


def _ffn_call(x, mod, g, w1, w3, w2, layer, half, row0, n_tiles, mod_idx, x_tail=None):
    d = x.shape[1]
    tm = TOKEN_TILE
    if x_tail is None:
        head_tiles, x_specs, xs = None, [pl.BlockSpec((tm, d), lambda i: (i, 0))], [x]
    else:
        head_tiles = x.shape[0] // tm
        x_specs = [pl.BlockSpec((tm, d), lambda i: (jnp.minimum(i, head_tiles - 1), 0)),
                   pl.BlockSpec((tm, d), lambda i: (jnp.maximum(i - head_tiles, 0), 0))]
        xs = [x, x_tail]
    return pl.pallas_call(
        functools.partial(_ffn_kernel, row0=row0, head_tiles=head_tiles),
        grid=(n_tiles,),
        in_specs=x_specs + [
            pl.BlockSpec((None, None, N_MOD, d), lambda i: (layer, mod_idx(i), 0, 0)),
            _resident((1, d), lambda i: (0, 0)),
            _resident((None, None, d, D_FF), lambda i: (layer, half, 0, 0)),
            _resident((None, None, d, D_FF), lambda i: (layer, half, 0, 0)),
            _resident((None, None, D_FF, d), lambda i: (layer, half, 0, 0)),
        ],
        out_specs=pl.BlockSpec((tm, d), lambda i: (i, 0)),
        out_shape=jax.ShapeDtypeStruct((n_tiles * tm, d), F32),
        scratch_shapes=[pltpu.VMEM((tm, D_FF), BF16)],
        compiler_params=_cparams(("arbitrary",)),
        name="ffn_half_step",
    )(*xs, mod, g, w1, w3, w2)


def _segnorm(x, bd, seg_len, gain):
    x2 = x * x
    hi = x2.astype(BF16)
    lo = (x2 - hi.astype(F32)).astype(BF16)
    groups = []
    for c0 in range(0, x.shape[1], bd.shape[0]):
        w = min(bd.shape[0], x.shape[1] - c0)
        groups.append(_dot(hi[:, c0:c0 + w], bd[0:w, 0:w]) + _dot(lo[:, c0:c0 + w], bd[0:w, 0:w]))
    ss = groups[0] if len(groups) == 1 else jnp.concatenate(groups, axis=-1)
    return x * lax.rsqrt(ss * (1.0 / seg_len) + RMS_EPS) * gain


def _tile_lanes(t, width):
    reps = width // t.shape[-1]
    return t if reps == 1 else jnp.concatenate([t] * reps, axis=-1)


def _rope(x, cos, sin_signed, quarter):
    width = x.shape[-1]
    lane = lax.broadcasted_iota(jnp.int32, x.shape, 1)
    first = (lane % (2 * quarter)) < quarter
    rot = jnp.where(first, pltpu.roll(x, width - quarter, 1), pltpu.roll(x, quarter, 1))
    return x * _tile_lanes(cos, width) + rot * _tile_lanes(sin_signed, width)


def _proj_kernel(x_ref, mod_ref, g_ref, w_ref, vec_ref, bd64_ref, bd32_ref, bd128_ref,
                 wuq_ref, wuk_ref, wuv_ref, c64_ref, s64_ref, c32_ref, s32_ref, cm_ref, sm_ref,
                 n_ref, naq_ref, nak_ref, nav_ref,
                 gqr_ref, gqn_ref, gk_ref, gv_ref,
                 dqr_ref, dqn_ref, dk_ref, dv_ref,
                 mqr_ref, mqn_ref, mk_ref, mv_ref):
    bd64, bd32, bd128 = bd64_ref[...], bd32_ref[...], bd128_ref[...]

    def vec(row, width):
        return vec_ref[row:row + 1, 0:width]

    groups = [slice(r, r + PROJ_ROWS) for r in range(0, x_ref.shape[0], PROJ_ROWS)]
    projected = []
    for rows in groups:
        n = _modulated_norm(x_ref[rows, :], g_ref[...], mod_ref[3:4, :], mod_ref[4:5, :]).astype(BF16)
        n_ref[rows, :] = n
        projected.append(_dot(n, w_ref[...]))

    for rows, p in zip(groups, projected):
        c64, s64 = c64_ref[rows, :], s64_ref[rows, :]
        c32, s32 = c32_ref[rows, :], s32_ref[rows, :]
        cm, sm = cm_ref[rows, :], sm_ref[rows, :]

        naq_ref[rows, :] = _segnorm(p[:, C_NA_Q:C_NA_Q + 256], bd64, HEAD_DIM, vec(0, 256)).astype(BF16)
        nak_ref[rows, :] = _segnorm(p[:, C_NA_K:C_NA_K + 256], bd64, HEAD_DIM, vec(1, 256)).astype(BF16)
        nav_ref[rows, :] = p[:, C_NA_V:C_NA_V + 256].astype(BF16)

        q = _segnorm(p[:, C_GQ_Q:C_GQ_Q + 256], bd64, HEAD_DIM, vec(2, 256))
        gqn_ref[rows, :] = q.astype(BF16)
        gqr_ref[rows, :] = _rope(q, c64, s64, HEAD_DIM // 4).astype(BF16)
        k = _segnorm(p[:, C_GQ_K:C_GQ_K + 128], bd64, HEAD_DIM, vec(3, 128))
        gk_ref[rows, :] = _rope(k, c64, s64, HEAD_DIM // 4).astype(BF16)
        gv_ref[rows, :] = p[:, C_GQ_V:C_GQ_V + 128].astype(BF16)

        q = _segnorm(p[:, C_DF_Q:C_DF_Q + 256], bd32, DIFF_QK_DIM, vec(4, 256))
        dqn_ref[rows, :] = q.astype(BF16)
        dqr_ref[rows, :] = _rope(q, c32, s32, DIFF_QK_DIM // 4).astype(BF16)
        k = _segnorm(p[:, C_DF_K:C_DF_K + 256], bd32, DIFF_QK_DIM, vec(5, 256))
        dk_ref[rows, :] = _rope(k, c32, s32, DIFF_QK_DIM // 4).astype(BF16)
        dv_ref[rows, :] = p[:, C_DF_V:C_DF_V + 256].astype(BF16)

        cq = p[:, C_ML_CQ:C_ML_CQ + 256]
        cq = cq * lax.rsqrt(jnp.mean(cq * cq, axis=-1, keepdims=True) + RMS_EPS) * vec(6, 256)
        q = _segnorm(_dot(cq.astype(BF16), wuq_ref[...]), bd128, MLA_QK_DIM, vec(8, 512))
        mqn_ref[rows, :] = q.astype(BF16)
        mqr_ref[rows, :] = _rope(q, cm, sm, MLA_ROPE_DIM // 4).astype(BF16)
        ckv = p[:, C_ML_CKV:C_ML_CKV + 128]
        ckv = (ckv * lax.rsqrt(jnp.mean(ckv * ckv, axis=-1, keepdims=True) + RMS_EPS) * vec(7, 128)).astype(BF16)
        k_rope = pltpu.roll(p[:, C_ML_ROPE:C_ML_ROPE + 128], MLA_NOPE_DIM, 1)
        k = _dot(ckv, wuk_ref[...]) + _tile_lanes(k_rope, 512)
        k = _segnorm(k, bd128, MLA_QK_DIM, vec(9, 512))
        mk_ref[rows, :] = _rope(k, cm, sm, MLA_ROPE_DIM // 4).astype(BF16)
        mv_ref[rows, :] = _dot(ckv, wuv_ref[...]).astype(BF16)


_PROJ_OUT_WIDTHS = (1024, 256, 256, 256, 256, 256, 128, 128, 256, 256, 256, 256, 512, 512, 512, 256)


def _proj_call(x, mod, g, w_in, vecs, bds, wuq, wuk, wuv, tables, layer, n_tiles, mod_idx, tab_idx):
    d = x.shape[1]
    tm = TOKEN_TILE
    const2 = lambda i: (0, 0)
    in_specs = [
        pl.BlockSpec((tm, d), lambda i: (i, 0)),
        pl.BlockSpec((None, None, N_MOD, d), lambda i: (layer, mod_idx(i), 0, 0)),
        pl.BlockSpec((1, d), const2),
        pl.BlockSpec((None, d, IN_COLS_PAD), lambda i: (layer, 0, 0)),
        pl.BlockSpec((None, 16, 512), lambda i: (layer, 0, 0)),
        pl.BlockSpec((256, 256), const2),
        pl.BlockSpec((256, 256), const2),
        pl.BlockSpec((256, 256), const2),
        pl.BlockSpec((None, 256, 512), lambda i: (layer, 0, 0)),
        pl.BlockSpec((None, 128, 512), lambda i: (layer, 0, 0)),
        pl.BlockSpec((None, 128, 256), lambda i: (layer, 0, 0)),
    ] + [pl.BlockSpec((tm, LANES), lambda i: (tab_idx(i), 0))] * 6
    rows = n_tiles * tm
    return pl.pallas_call(
        _proj_kernel,
        grid=(n_tiles,),
        in_specs=in_specs,
        out_specs=[pl.BlockSpec((tm, w), lambda i: (i, 0)) for w in _PROJ_OUT_WIDTHS],
        out_shape=[jax.ShapeDtypeStruct((rows, w), BF16) for w in _PROJ_OUT_WIDTHS],
        compiler_params=_cparams(("arbitrary",)),
        name="mixer_projection",
    )(x, mod, g, w_in, vecs, *bds, wuq, wuk, wuv, *tables)


def _attend(blocks, subs, bounded):
    scores = []
    for i, sub in enumerate(subs):
        ks = sub[3]
        per_block = []
        for k_ref, rows, _, q_t, bias in blocks:
            s = _dot(k_ref[rows, ks * LANES:(ks + 1) * LANES], q_t[i])
            per_block.append(s if bias is None else s + bias[i])
        scores.append(per_block)
    outs = []
    for i, sub in enumerate(subs):
        if not bounded:
            m = functools.reduce(jnp.maximum, [jnp.max(s, axis=0, keepdims=True) for s in scores[i]])
        l = acc = None
        for s, (_, _, v_t, _, _) in zip(scores[i], blocks):
            p = jnp.exp2(s if bounded else s - m)
            l_blk = jnp.sum(p, axis=0, keepdims=True)
            acc_blk = _dot(_head_rows(v_t, sub[4]), p.astype(BF16))
            l, acc = (l_blk, acc_blk) if l is None else (l + l_blk, acc + acc_blk)
        outs.append(acc / l)
    return outs


def _score_bound(g_q, g_k, dim):
    return 1.01 * dim * (dim ** -0.5 * LOG2E) * jnp.max(jnp.abs(g_q)) * jnp.max(jnp.abs(g_k))


def _transpose_bf16(eye, a):
    return _dot_nt(eye[0:a.shape[1], 0:a.shape[1]], a).astype(BF16)


def _q_slabs_t(eye, q_ref):
    out = []
    for j0 in range(0, q_ref.shape[1], 2 * LANES):
        w = min(2 * LANES, q_ref.shape[1] - j0)
        q_t = _dot_nt(eye[0:w, 0:w], q_ref[:, j0:j0 + w]).astype(BF16)
        out += [q_t[r:r + LANES, :] for r in range(0, w, LANES)]
    return out


def _keep_rows(q_t, lo, hi):
    if lo == 0 and hi == q_t.shape[0]:
        return q_t
    row = lax.broadcasted_iota(jnp.int32, q_t.shape, 0)
    return jnp.where((row >= lo) & (row < hi), q_t, jnp.zeros_like(q_t))


def _head_rows(v_t, head):
    return v_t[head * HEAD_DIM:(head + 1) * HEAD_DIM, :]


def _store_heads(eye, o_ref, o_t_heads):
    for p in range(0, len(o_t_heads), 4):
        o_t = jnp.concatenate(o_t_heads[p:p + 4], axis=0).astype(BF16)
        o_ref[:, p * HEAD_DIM:p * HEAD_DIM + o_t.shape[0]] = _dot_nt(eye, o_t).astype(o_ref.dtype)


def _stage_values_t(eye, v_ref, vt_scr, n_chunks, chunk):
    for c in range(n_chunks):
        for j in range(v_ref.shape[1] // LANES):
            vt_scr[c, j * LANES:(j + 1) * LANES, :] = _transpose_bf16(
                eye, v_ref[c * chunk:(c + 1) * chunk, j * LANES:(j + 1) * LANES])


def _joint_kernel(*refs, subs, n_lat_tiles, seq, chunk, with_ctx_queries, diff_lam_init, bounded):
    if diff_lam_init is None:
        qr_ref, qn_ref, kl_ref, kc_ref, vl_ref, vc_ref, eye_ref, o_ref, vt_scr, vct_scr = refs
    else:
        qr_ref, qn_ref, kl_ref, kc_ref, vl_ref, vc_ref, eye_ref, lam_ref, gsub_ref, o_ref, vt_scr, vct_scr = refs
    t = pl.program_id(2)
    n_chunks = seq // chunk
    eye = eye_ref[...]

    @pl.when(t == 0)
    def _():
        _stage_values_t(eye, vl_ref, vt_scr, n_chunks, chunk)
        _stage_values_t(eye, vc_ref, vct_scr, 1, vc_ref.shape[0])

    def operands(q_ref):
        slabs = _q_slabs_t(eye, q_ref)
        return [_keep_rows(slabs[qs], lo, hi) for (qs, lo, hi, _, _) in subs]

    def finish(outs):
        if diff_lam_init is None:
            _store_heads(eye, o_ref, outs)
            return
        lv = lam_ref[...]
        lam = (jnp.exp(jnp.sum(lv[0:1, :] * lv[1:2, :], axis=-1, keepdims=True))
               - jnp.exp(jnp.sum(lv[2:3, :] * lv[3:4, :], axis=-1, keepdims=True)) + diff_lam_init)
        heads = []
        for o1, o2 in zip(outs[0::2], outs[1::2]):
            d = o1 - lam * o2
            ms = jnp.mean(d * d, axis=0, keepdims=True)
            heads.append((d * lax.rsqrt(ms + RMS_EPS) * gsub_ref[...]) * (1.0 - diff_lam_init))
        _store_heads(eye, o_ref, heads)

    def ctx_block():
        return (kc_ref, slice(None), vct_scr[0], operands(qn_ref), None)

    @pl.when(t < n_lat_tiles)
    def _():
        q_t = operands(qr_ref)
        latent = [(kl_ref, slice(c * chunk, (c + 1) * chunk), vt_scr[c], q_t, None) for c in range(n_chunks)]
        finish(_attend(latent + [ctx_block()], subs, bounded))

    if with_ctx_queries:
        @pl.when(t >= n_lat_tiles)
        def _():
            finish(_attend([ctx_block()], subs, bounded))


def _joint_call(operands, out_rows, *, subs, n_blocks, batch, seq, ctx_len, with_ctx_queries, name, bounded,
                diff_lam_init=None):
    qr, qn, k, v = operands[:4]
    tq = Q_TILE
    n_lat = seq // tq
    nqt = n_lat + (1 if with_ctx_queries else 0)
    ctx_row0 = batch * seq // ctx_len
    q_width, k_width, v_width = qr.shape[1] // n_blocks, k.shape[1] // n_blocks, v.shape[1] // n_blocks
    out_width = BRANCH_WIDTH // n_blocks
    chunk = min(KEY_CHUNK, seq)
    n_chunks = seq // chunk

    def qrow(b, t):
        return jnp.where(t < n_lat, b * n_lat + t, batch * n_lat + b)

    in_specs = [
        pl.BlockSpec((tq, q_width), lambda b, blk, t: (qrow(b, t), blk)),
        pl.BlockSpec((tq, q_width), lambda b, blk, t: (qrow(b, t), blk)),
        pl.BlockSpec((seq, k_width), lambda b, blk, t: (b, blk)),
        pl.BlockSpec((ctx_len, k_width), lambda b, blk, t: (ctx_row0 + b, blk)),
        pl.BlockSpec((seq, v_width), lambda b, blk, t: (b, blk)),
        pl.BlockSpec((ctx_len, v_width), lambda b, blk, t: (ctx_row0 + b, blk)),
        pl.BlockSpec((tq, tq), lambda b, blk, t: (0, 0)),
    ]
    args = [qr, qn, k, k, v, v, jnp.eye(tq, dtype=BF16)]
    if diff_lam_init is not None:
        in_specs += [pl.BlockSpec((4, DIFF_QK_DIM), lambda b, blk, t: (0, 0)),
                     pl.BlockSpec((HEAD_DIM, tq), lambda b, blk, t: (0, 0))]
        args += list(operands[4:])
    return pl.pallas_call(
        functools.partial(_joint_kernel, subs=subs, n_lat_tiles=n_lat, seq=seq, chunk=chunk,
                          with_ctx_queries=with_ctx_queries, diff_lam_init=diff_lam_init, bounded=bounded),
        grid=(batch, n_blocks, nqt),
        in_specs=in_specs,
        out_specs=pl.BlockSpec((tq, out_width), lambda b, blk, t: (qrow(b, t), blk)),
        out_shape=jax.ShapeDtypeStruct((out_rows, BRANCH_WIDTH), BF16),
        scratch_shapes=[pltpu.VMEM((n_chunks, v_width, chunk), BF16), pltpu.VMEM((1, v_width, ctx_len), BF16)],
        compiler_params=_cparams(("arbitrary", "arbitrary", "arbitrary")),
        name=name,
    )(*args)


def _na_kernel(q_ref, kl_ref, kc_ref, vl_ref, vc_ref, ab_ref, eye_ref, o_ref, vt_scr, vct_scr, *, n_lat_tiles,
               bounded):
    t = pl.program_id(2)
    tq = o_ref.shape[0]
    eye = eye_ref[...]
    rows_per_tile = tq // GRID_W
    band_tiles = NA_KEY_ROWS // rows_per_tile
    heads = range(BRANCH_HEADS)

    @pl.when(t == 0)
    def _():
        _stage_values_t(eye, vl_ref, vt_scr, n_lat_tiles, tq)
        _stage_values_t(eye, vc_ref, vct_scr, 1, vc_ref.shape[0])

    subs = tuple((h // 2, (h % 2) * HEAD_DIM, (h % 2 + 1) * HEAD_DIM, h // 2, h) for h in heads)

    def attend(blocks):
        slabs = _q_slabs_t(eye, q_ref)
        q_t = [_keep_rows(slabs[qs], lo, hi) for (qs, lo, hi, _, _) in subs]
        outs = _attend([(k_ref, rows, v_t, q_t, bias) for k_ref, rows, v_t, bias in blocks], subs, bounded)
        _store_heads(eye, o_ref, outs)

    ctx_block = (kc_ref, slice(None), vct_scr[0], None)

    @pl.when(t < n_lat_tiles)
    def _():
        tile0 = jnp.clip(t - NA_WIN_H // 2 // rows_per_tile, 0, n_lat_tiles - band_tiles)
        rows = pl.ds(pl.multiple_of(tile0 * tq, tq), band_tiles * tq)
        v_t = jnp.concatenate([vt_scr[tile0 + j] for j in range(band_tiles)], axis=1)
        attend([(kl_ref, rows, v_t, ab_ref), ctx_block])

    @pl.when(t >= n_lat_tiles)
    def _():
        attend([ctx_block])


def _na_call(operands, out_rows, *, batch, seq, ctx_len, with_ctx_queries, bounded):
    q, k, v, ab = operands
    tq = Q_TILE
    n_lat = seq // tq
    nqt = n_lat + (1 if with_ctx_queries else 0)
    ctx_row0 = batch * seq // ctx_len
    n_keys = NA_KEY_ROWS * GRID_W
    width = BRANCH_WIDTH

    def qrow(b, t):
        return jnp.where(t < n_lat, b * n_lat + t, batch * n_lat + b)

    def pattern(t):
        return jnp.where(t == 0, 0, jnp.where(t >= n_lat - 1, 2, 1))

    return pl.pallas_call(
        functools.partial(_na_kernel, n_lat_tiles=n_lat, bounded=bounded),
        grid=(batch, 1, nqt),
        in_specs=[
            pl.BlockSpec((tq, width), lambda b, blk, t: (qrow(b, t), 0)),
            pl.BlockSpec((seq, width), lambda b, blk, t: (b, 0)),
            pl.BlockSpec((ctx_len, width), lambda b, blk, t: (ctx_row0 + b, 0)),
            pl.BlockSpec((seq, width), lambda b, blk, t: (b, 0)),
            pl.BlockSpec((ctx_len, width), lambda b, blk, t: (ctx_row0 + b, 0)),
            pl.BlockSpec((None, BRANCH_HEADS, n_keys, tq), lambda b, blk, t: (pattern(t), 0, 0, 0)),
            pl.BlockSpec((tq, tq), lambda b, blk, t: (0, 0)),
        ],
        out_specs=pl.BlockSpec((tq, width), lambda b, blk, t: (qrow(b, t), 0)),
        out_shape=jax.ShapeDtypeStruct((out_rows, width), BF16),
        scratch_shapes=[pltpu.VMEM((n_lat, width, tq), BF16), pltpu.VMEM((1, width, ctx_len), BF16)],
        compiler_params=_cparams(("arbitrary", "arbitrary", "arbitrary")),
        name="neighbourhood_attention",
    )(q, k, k, v, v, ab, jnp.eye(tq, dtype=BF16))


def _merge_kernel(x_ref, mod_ref, n_ref, o0_ref, o1_ref, o2_ref, o3_ref, wg_ref, bg_ref, wb_ref, wo_ref,
                  out_ref, y_scr):
    n = n_ref[...]
    branch = [o_ref[...] for o_ref in (o0_ref, o1_ref, o2_ref, o3_ref)]
    for c0 in range(0, n.shape[1], MERGE_SLAB):
        cols = slice(c0, c0 + MERGE_SLAB)
        y = None
        for j in range(N_BRANCH):
            gate = jax.nn.sigmoid(_dot(n, wg_ref[j, :, cols]) + bg_ref[j, :, cols])
            term = gate * _dot(branch[j], wb_ref[j, :, cols])
            y = term if y is None else y + term
        y_scr[:, cols] = y.astype(BF16)
    out_ref[...] = x_ref[...] + mod_ref[5:6, :] * _dot(y_scr[...], wo_ref[...])


def _merge_call(x, mod, n, outs, w_gate, b_gate, w_branch, w_out, layer, n_tiles, mod_idx):
    d = x.shape[1]
    tm = TOKEN_TILE
    row = lambda i: (i, 0)
    return pl.pallas_call(
        _merge_kernel,
        grid=(n_tiles,),
        in_specs=[
            pl.BlockSpec((tm, d), row),
            pl.BlockSpec((None, None, N_MOD, d), lambda i: (layer, mod_idx(i), 0, 0)),
            pl.BlockSpec((tm, d), row),
        ] + [pl.BlockSpec((tm, BRANCH_WIDTH), row)] * 4 + [
            _resident((None, N_BRANCH, d, d), lambda i: (layer, 0, 0, 0)),
            _resident((None, N_BRANCH, 1, d), lambda i: (layer, 0, 0, 0)),
            _resident((None, N_BRANCH, BRANCH_WIDTH, d), lambda i: (layer, 0, 0, 0)),
            _resident((None, d, d), lambda i: (layer, 0, 0)),
        ],
        out_specs=pl.BlockSpec((tm, d), row),
        out_shape=jax.ShapeDtypeStruct((n_tiles * tm, d), F32),
        scratch_shapes=[pltpu.VMEM((tm, d), BF16)],
        compiler_params=_cparams(("arbitrary",)),
        name="branch_merge",
    )(x, mod, n, *outs, w_gate, b_gate, w_branch, w_out)


def _rope_tables(seq, rot_dim, pad_rows):
    t = jnp.arange(seq, dtype=jnp.int32)
    row = (t // GRID_W).astype(F32)
    col = (t % GRID_W).astype(F32)
    n_freq = rot_dim // 4
    inv = ROPE_THETA ** (-jnp.arange(n_freq, dtype=F32) / n_freq)
    ar = row[:, None] * inv[None, :]
    ac = col[:, None] * inv[None, :]
    ang = jnp.concatenate([ar, ar, ac, ac], axis=-1)
    sign = jnp.tile(jnp.concatenate([-jnp.ones((n_freq,), F32), jnp.ones((n_freq,), F32)]), 2)
    cos = jnp.concatenate([jnp.cos(ang), jnp.ones((pad_rows, rot_dim), F32)], axis=0)
    sin = jnp.concatenate([jnp.sin(ang) * sign[None, :], jnp.zeros((pad_rows, rot_dim), F32)], axis=0)
    return cos, sin


def _all_rope_tables(seq, pad_rows):
    c64, s64 = _rope_tables(seq, HEAD_DIM, pad_rows)
    c32, s32 = _rope_tables(seq, DIFF_QK_DIM, pad_rows)
    rows = seq + pad_rows
    ones = lambda w: jnp.ones((rows, w), F32)
    zeros = lambda w: jnp.zeros((rows, w), F32)
    pad = LANES - MLA_QK_DIM
    cm = jnp.concatenate([ones(MLA_NOPE_DIM), c32, ones(pad)], axis=-1)
    sm = jnp.concatenate([zeros(MLA_NOPE_DIM), s32, zeros(pad)], axis=-1)
    return (jnp.tile(c64, (1, 2)), jnp.tile(s64, (1, 2)), jnp.tile(c32, (1, 4)), jnp.tile(s32, (1, 4)), cm, sm)


def _block_diag_ones(width, seg):
    idx = np.arange(width) // seg
    return jnp.asarray(idx[:, None] == idx[None, :], dtype=BF16)


def _na_bias_tables(rpb, grid_rows):
    rows_per_tile = Q_TILE // GRID_W
    wh = min(NA_WIN_H, grid_rows)
    kinds = ((0, 0), (rows_per_tile, 0), (grid_rows - rows_per_tile, grid_rows - NA_KEY_ROWS))
    col = np.arange(GRID_W)
    c0 = np.clip(col - NA_WIN_W // 2, 0, GRID_W - NA_WIN_W)
    col_ok = (col[:, None] >= c0[None, :]) & (col[:, None] < c0[None, :] + NA_WIN_W)
    d_col = np.clip(col[:, None] - col[None, :], -(NA_WIN_W - 1), NA_WIN_W - 1) + (NA_WIN_W - 1)
    col_hot = (d_col[..., None] == np.arange(2 * NA_WIN_W - 1)).astype(np.float32)
    blocks = jnp.einsum("hab,dcb->hadc", rpb.astype(F32), jnp.asarray(col_hot), precision=lax.Precision.HIGHEST)
    blocks = jnp.where(jnp.asarray(col_ok)[None, None], blocks * LOG2E, NEG_BIG)
    masked = jnp.full(blocks.shape[:1] + (GRID_W, GRID_W), NEG_BIG, F32)
    tables = []
    for q_row0, k_row0 in kinds:
        r = q_row0 + np.arange(rows_per_tile)
        kr = k_row0 + np.arange(NA_KEY_ROWS)
        r0 = np.clip(r - wh // 2, 0, grid_rows - wh)
        key_rows = []
        for kk in range(NA_KEY_ROWS):
            in_window = (kr[kk] >= r0) & (kr[kk] < r0 + wh)
            key_rows.append(jnp.concatenate(
                [blocks[:, kr[kk] - r[qr] + (NA_WIN_H - 1)] if in_window[qr] else masked
                 for qr in range(rows_per_tile)], axis=-1))
        tables.append(jnp.concatenate(key_rows, axis=-2))
    return jnp.stack(tables, axis=0)


def _pad_heads(w, head_w, to_w):
    lead = w.shape[:-1]
    w = w.reshape(lead + (BRANCH_HEADS, head_w))
    w = jnp.pad(w, [(0, 0)] * len(lead) + [(0, 0), (0, to_w - head_w)])
    return w.reshape(lead + (BRANCH_HEADS * to_w,))


def _gqa_head_order(a, axis):
    parts = jnp.split(a, BRANCH_HEADS, axis=axis)
    return jnp.concatenate([parts[0], parts[2], parts[1], parts[3]], axis=axis)


def kernel(x, c, ctx, c_ctx, w_ada, b_ada, norm_g, ffn_w1, ffn_w3, ffn_w2, w_in, na_g_q, na_g_k, na_rpb, gqa_g_q, gqa_g_k, diff_g_q, diff_g_k, diff_lambda, diff_g_sub, mla_g_cq, mla_g_ckv, mla_w_uq, mla_w_ukv, mla_g_q, mla_g_k, w_gate, b_gate, w_branch, w_out):
    batch, seq, d = x.shape
    ctx_len = ctx.shape[1]
    depth = w_ada.shape[0]
    tm = TOKEN_TILE
    assert d == D_MODEL and ctx_len == Q_TILE and seq % tm == 0 and (batch * ctx_len) % tm == 0
    assert seq % min(KEY_CHUNK, seq) == 0 and seq // GRID_W >= NA_KEY_ROWS and batch < 16
    n_lat_rows = batch * seq
    lat_tiles = n_lat_rows // tm
    all_tiles = lat_tiles + batch * ctx_len // tm
    tiles_per_batch = seq // tm

    def mod_idx(i):
        return jnp.where(i < lat_tiles, i // tiles_per_batch, batch)

    def tab_idx(i):
        return jnp.where(i < lat_tiles, i % tiles_per_batch, tiles_per_batch)

    c16 = jnp.concatenate([c, c_ctx[None, :], jnp.zeros((16 - batch - 1, d), F32)], axis=0)
    mods = _mod_call(c16, w_ada, b_ada).reshape(depth, 16, N_MOD, d)
    w1, w3, w2 = ffn_w1.astype(BF16), ffn_w3.astype(BF16), ffn_w2.astype(BF16)
    na_w, gq_w, df_w, ml_w = jnp.split(w_in, [768, 1280, 2048], axis=-1)
    w_in_p = jnp.concatenate([
        na_w, _gqa_head_order(gq_w[..., :256], -1), gq_w[..., 256:], df_w, ml_w,
        jnp.zeros((depth, d, IN_COLS_PAD - w_in.shape[-1]), F32)], axis=-1).astype(BF16)
    wuq = _pad_heads(mla_w_uq, MLA_QK_DIM, LANES).astype(BF16)
    wukv = mla_w_ukv.reshape(depth, MLA_KV_RANK, BRANCH_HEADS, MLA_NOPE_DIM + HEAD_DIM)
    wuk = _pad_heads(wukv[..., :MLA_NOPE_DIM].reshape(depth, MLA_KV_RANK, -1), MLA_NOPE_DIM, LANES).astype(BF16)
    wuv = wukv[..., MLA_NOPE_DIM:].reshape(depth, MLA_KV_RANK, BRANCH_WIDTH).astype(BF16)

    def vec_row(v, reps=1, scale=1.0):
        v = jnp.tile(v, (1, reps)) * scale
        return jnp.pad(v, ((0, 0), (0, 512 - v.shape[-1])))

    pad_g = lambda g: jnp.pad(g, ((0, 0), (0, LANES - MLA_QK_DIM)))
    vec_rows = [
        vec_row(na_g_q, 4, HEAD_DIM ** -0.5 * LOG2E), vec_row(na_g_k, 4),
        vec_row(gqa_g_q, 4, HEAD_DIM ** -0.5 * LOG2E), vec_row(gqa_g_k, 2),
        vec_row(diff_g_q, 8, DIFF_QK_DIM ** -0.5 * LOG2E), vec_row(diff_g_k, 8),
        vec_row(mla_g_cq), vec_row(mla_g_ckv),
        vec_row(pad_g(mla_g_q), 4, MLA_QK_DIM ** -0.5 * LOG2E), vec_row(pad_g(mla_g_k), 4),
    ]
    vecs = jnp.stack(vec_rows + [jnp.zeros_like(vec_rows[0])] * (16 - len(vec_rows)), axis=1)
    bds = (_block_diag_ones(256, HEAD_DIM), _block_diag_ones(256, DIFF_QK_DIM), _block_diag_ones(256, LANES))
    tables = _all_rope_tables(seq, tm)
    na_ab = [_na_bias_tables(na_rpb[i], seq // GRID_W) for i in range(depth)]
    gsub = jnp.broadcast_to(diff_g_sub[:, :, None], (depth, HEAD_DIM, Q_TILE))
    wg, wo = w_gate.astype(BF16), w_out.astype(BF16)
    wb = jnp.concatenate([w_branch[:, 0:1], _gqa_head_order(w_branch[:, 1:2], 2), w_branch[:, 2:]], axis=1).astype(BF16)
    bg = b_gate.reshape(depth, N_BRANCH, 1, d)

    xa, x_tail = x.reshape(n_lat_rows, d), ctx.reshape(batch * ctx_len, d)

    gqa_subs = ((0, 0, 64, 0, 0), (0, 64, 128, 0, 1), (1, 0, 64, 0, 0), (1, 64, 128, 0, 1))
    diff_subs = ((0, 0, 32, 0, 0), (0, 32, 64, 0, 0), (0, 64, 96, 0, 1), (0, 96, 128, 0, 1))
    mla_subs = tuple((h, 0, LANES, h, h) for h in range(BRANCH_HEADS))

    for i in range(depth):
        need_ctx = i < depth - 1
        mod = mods
        g = norm_g[i].reshape(3, 1, d)
        xa = _ffn_call(xa, mod, g[0], w1, w3, w2, i, 0, 0, all_tiles, mod_idx, x_tail=x_tail)
        x_tail = None
        (n, naq, nak, nav, gqr, gqn, gk, gv, dqr, dqn, dk, dv, mqr, mqn, mk, mv) = _proj_call(
            xa, mod, g[1], w_in_p, vecs, bds, wuq, wuk, wuv, tables, i, all_tiles, mod_idx, tab_idx)
        out_rows = xa.shape[0] if need_ctx else n_lat_rows
        common = dict(batch=batch, seq=seq, ctx_len=ctx_len, with_ctx_queries=need_ctx)
        lam_init = 0.8 - 0.6 * math.exp(-0.3 * i)
        def dispatch(call, bound, *operands, **kw):
            return lax.cond(bound <= SCORE_BOUND,
                            lambda *a: call(a, out_rows, bounded=True, **kw, **common),
                            lambda *a: call(a, out_rows, bounded=False, **kw, **common), *operands)

        joint = functools.partial(dispatch, _joint_call)
        na_bound = _score_bound(na_g_q[i], na_g_k[i], HEAD_DIM) + LOG2E * jnp.max(jnp.abs(na_rpb[i]))

        outs = [
            dispatch(_na_call, na_bound, naq, nak, nav, na_ab[i]),
            joint(_score_bound(gqa_g_q[i], gqa_g_k[i], HEAD_DIM), gqr, gqn, gk, gv,
                  subs=gqa_subs, n_blocks=1, name="gqa_attention"),
            joint(_score_bound(diff_g_q[i], diff_g_k[i], DIFF_QK_DIM), dqr, dqn, dk, dv, diff_lambda[i], gsub[i],
                  subs=diff_subs, n_blocks=2, name="diff_attention", diff_lam_init=lam_init),
            joint(_score_bound(mla_g_q[i], mla_g_k[i], MLA_QK_DIM), mqr, mqn, mk, mv,
                  subs=mla_subs, n_blocks=1, name="mla_attention"),
        ]
        n_tiles = all_tiles if need_ctx else lat_tiles
        xa = _merge_call(xa, mod, n, outs, wg, bg, wb, wo, i, n_tiles, mod_idx)
        xa = _ffn_call(xa, mod, g[2], w1, w3, w2, i, 1, 6, n_tiles, mod_idx)
    return xa.reshape(batch, seq, d)
```

```python
import functools
import math

import jax
import jax.numpy as jnp
import numpy as np
from jax import lax
from jax.experimental import pallas as pl
from jax.experimental.pallas import tpu as pltpu

F32 = jnp.float32
BF16 = jnp.bfloat16

D_MODEL = 1024
GRID_W = 64
ROPE_THETA = 10000.0
RMS_EPS = 1e-6
HEAD_DIM = 64
N_BRANCH = 4
BRANCH_HEADS = 4
BRANCH_WIDTH = 256
NA_WIN_H = 8
NA_WIN_W = 16
DIFF_QK_DIM = 32
MLA_Q_RANK = 256
MLA_KV_RANK = 128
MLA_NOPE_DIM = 64
MLA_ROPE_DIM = 32
MLA_QK_DIM = 96
D_FF = 2816
N_MOD = 9
LOG2E = 1.4426950408889634

LANES = 128
TOKEN_TILE = 512
Q_TILE = 256
KEY_CHUNK = 4096
FF_SLAB = 256
MERGE_SLAB = 256
PROJ_ROWS = 256
NA_KEY_ROWS = 12
SCORE_LOOKAHEAD = 2
NEG_BIG = -1e30
SCORE_BOUND = 40.0
VMEM_LIMIT = 56 * 1024 * 1024

C_NA_Q, C_NA_K, C_NA_V = 0, 256, 512
C_GQ_Q, C_GQ_K, C_GQ_V = 768, 1024, 1152
C_DF_Q, C_DF_K, C_DF_V = 1280, 1536, 1792
C_ML_CQ, C_ML_CKV, C_ML_ROPE = 2048, 2304, 2432
IN_COLS_PAD = 2560


def _cparams(sem):
    return pltpu.CompilerParams(dimension_semantics=sem, vmem_limit_bytes=VMEM_LIMIT)


def _dot(a, b):
    return jnp.dot(a, b, preferred_element_type=F32)


def _dot_nt(a, b):
    return lax.dot_general(a, b, (((1,), (1,)), ((), ())), preferred_element_type=F32)


def _silu(a):
    return a * jax.nn.sigmoid(a)


def _modulated_norm(x, g, shift, scale):
    ms = jnp.mean(x * x, axis=-1, keepdims=True)
    return (x * lax.rsqrt(ms + RMS_EPS) * g) * (1.0 + scale) + shift


def _mod_kernel(c_ref, w_ref, b_ref, o_ref):
    c = c_ref[...]
    o_ref[...] = _dot(_silu(c).astype(BF16), w_ref[...].astype(BF16)) + b_ref[...]


def _mod_call(c16, w_ada, b_ada):
    depth, d, nd = w_ada.shape
    tn = 1024
    return pl.pallas_call(
        _mod_kernel,
        grid=(depth, nd // tn),
        in_specs=[
            pl.BlockSpec((16, d), lambda l, j: (0, 0)),
            pl.BlockSpec((None, d, tn), lambda l, j: (l, 0, j)),
            pl.BlockSpec((None, 1, tn), lambda l, j: (l, 0, j)),
        ],
        out_specs=pl.BlockSpec((None, 16, tn), lambda l, j: (l, 0, j)),
        out_shape=jax.ShapeDtypeStruct((depth, 16, nd), F32),
        compiler_params=_cparams(("arbitrary", "arbitrary")),
        name="adaln_mod",
    )(c16, w_ada, b_ada.reshape(depth, 1, nd))


def _ffn_kernel(*refs, row0, head_tiles):
    if head_tiles is None:
        x_ref, mod_ref, g_ref, w1_ref, w3_ref, w2_ref, o_ref, act_scr = refs
        x = x_ref[...]
    else:
        x_ref, tail_ref, mod_ref, g_ref, w1_ref, w3_ref, w2_ref, o_ref, act_scr = refs
        x = jnp.where(pl.program_id(0) < head_tiles, x_ref[...], tail_ref[...])
    h = _modulated_norm(x, g_ref[...], mod_ref[row0:row0 + 1, :], mod_ref[row0 + 1:row0 + 2, :]).astype(BF16)
    for f0 in range(0, D_FF, FF_SLAB):
        a = _dot(h, w1_ref[:, f0:f0 + FF_SLAB])
        b = _dot(h, w3_ref[:, f0:f0 + FF_SLAB])
        act_scr[:, f0:f0 + FF_SLAB] = (_silu(a) * b).astype(BF16)
    o_ref[...] = x + (0.5 * mod_ref[row0 + 2:row0 + 3, :]) * _dot(act_scr[...], w2_ref[...])


def _resident(block_shape, index_map):
    return pl.BlockSpec(block_shape, index_map, pipeline_mode=pl.Buffered(1))


def _ffn_call(x, mod, g, w1, w3, w2, layer, half, row0, n_tiles, mod_idx, x_tail=None):
    d = x.shape[1]
    tm = TOKEN_TILE
    if x_tail is None:
        head_tiles, x_specs, xs = None, [pl.BlockSpec((tm, d), lambda i: (i, 0))], [x]
    else:
        head_tiles = x.shape[0] // tm
        x_specs = [pl.BlockSpec((tm, d), lambda i: (jnp.minimum(i, head_tiles - 1), 0)),
                   pl.BlockSpec((tm, d), lambda i: (jnp.maximum(i - head_tiles, 0), 0))]
        xs = [x, x_tail]
    return pl.pallas_call(
        functools.partial(_ffn_kernel, row0=row0, head_tiles=head_tiles),
        grid=(n_tiles,),
        in_specs=x_specs + [
            pl.BlockSpec((None, None, N_MOD, d), lambda i: (layer, mod_idx(i), 0, 0)),
            _resident((1, d), lambda i: (0, 0)),
            _resident((None, None, d, D_FF), lambda i: (layer, half, 0, 0)),
            _resident((None, None, d, D_FF), lambda i: (layer, half, 0, 0)),
            _resident((None, None, D_FF, d), lambda i: (layer, half, 0, 0)),
        ],
        out_specs=pl.BlockSpec((tm, d), lambda i: (i, 0)),
        out_shape=jax.ShapeDtypeStruct((n_tiles * tm, d), F32),
        scratch_shapes=[pltpu.VMEM((tm, D_FF), BF16)],
        compiler_params=_cparams(("arbitrary",)),
        name="ffn_half_step",
    )(*xs, mod, g, w1, w3, w2)


def _segnorm(x, bd, seg_len, gain):
    x2 = x * x
    hi = x2.astype(BF16)
    lo = (x2 - hi.astype(F32)).astype(BF16)
    groups = []
    for c0 in range(0, x.shape[1], bd.shape[0]):
        w = min(bd.shape[0], x.shape[1] - c0)
        groups.append(_dot(hi[:, c0:c0 + w], bd[0:w, 0:w]) + _dot(lo[:, c0:c0 + w], bd[0:w, 0:w]))
    ss = groups[0] if len(groups) == 1 else jnp.concatenate(groups, axis=-1)
    return x * lax.rsqrt(ss * (1.0 / seg_len) + RMS_EPS) * gain


def _tile_lanes(t, width):
    reps = width // t.shape[-1]
    return t if reps == 1 else jnp.concatenate([t] * reps, axis=-1)


def _rope(x, cos, sin_signed, quarter):
    width = x.shape[-1]
    lane = lax.broadcasted_iota(jnp.int32, x.shape, 1)
    first = (lane % (2 * quarter)) < quarter
    rot = jnp.where(first, pltpu.roll(x, width - quarter, 1), pltpu.roll(x, quarter, 1))
    return x * _tile_lanes(cos, width) + rot * _tile_lanes(sin_signed, width)


def _proj_kernel(x_ref, mod_ref, g_ref, w_ref, vec_ref, bd64_ref, bd32_ref, bd128_ref,
                 wuq_ref, wuk_ref, wuv_ref, c64_ref, s64_ref, c32_ref, s32_ref, cm_ref, sm_ref,
                 n_ref, naq_ref, nak_ref, nav_ref,
                 gqr_ref, gqn_ref, gk_ref, gv_ref,
                 dqr_ref, dqn_ref, dk_ref, dv_ref,
                 mqr_ref, mqn_ref, mk_ref, mv_ref):
    bd64, bd32, bd128 = bd64_ref[...], bd32_ref[...], bd128_ref[...]

    def vec(row, width):
        return vec_ref[row:row + 1, 0:width]

    groups = [slice(r, r + PROJ_ROWS) for r in range(0, x_ref.shape[0], PROJ_ROWS)]
    projected = []
    for rows in groups:
        n = _modulated_norm(x_ref[rows, :], g_ref[...], mod_ref[3:4, :], mod_ref[4:5, :]).astype(BF16)
        n_ref[rows, :] = n
        projected.append(_dot(n, w_ref[...]))

    for rows, p in zip(groups, projected):
        c64, s64 = c64_ref[rows, :], s64_ref[rows, :]
        c32, s32 = c32_ref[rows, :], s32_ref[rows, :]
        cm, sm = cm_ref[rows, :], sm_ref[rows, :]

        naq_ref[rows, :] = _segnorm(p[:, C_NA_Q:C_NA_Q + 256], bd64, HEAD_DIM, vec(0, 256)).astype(BF16)
        nak_ref[rows, :] = _segnorm(p[:, C_NA_K:C_NA_K + 256], bd64, HEAD_DIM, vec(1, 256)).astype(BF16)
        nav_ref[rows, :] = p[:, C_NA_V:C_NA_V + 256].astype(BF16)

        q = _segnorm(p[:, C_GQ_Q:C_GQ_Q + 256], bd64, HEAD_DIM, vec(2, 256))
        gqn_ref[rows, :] = q.astype(BF16)
        gqr_ref[rows, :] = _rope(q, c64, s64, HEAD_DIM // 4).astype(BF16)
        k = _segnorm(p[:, C_GQ_K:C_GQ_K + 128], bd64, HEAD_DIM, vec(3, 128))
        gk_ref[rows, :] = _rope(k, c64, s64, HEAD_DIM // 4).astype(BF16)
        gv_ref[rows, :] = p[:, C_GQ_V:C_GQ_V + 128].astype(BF16)

        q = _segnorm(p[:, C_DF_Q:C_DF_Q + 256], bd32, DIFF_QK_DIM, vec(4, 256))
        dqn_ref[rows, :] = q.astype(BF16)
        dqr_ref[rows, :] = _rope(q, c32, s32, DIFF_QK_DIM // 4).astype(BF16)
        k = _segnorm(p[:, C_DF_K:C_DF_K + 256], bd32, DIFF_QK_DIM, vec(5, 256))
        dk_ref[rows, :] = _rope(k, c32, s32, DIFF_QK_DIM // 4).astype(BF16)
        dv_ref[rows, :] = p[:, C_DF_V:C_DF_V + 256].astype(BF16)

        cq = p[:, C_ML_CQ:C_ML_CQ + 256]
        cq = cq * lax.rsqrt(jnp.mean(cq * cq, axis=-1, keepdims=True) + RMS_EPS) * vec(6, 256)
        q = _segnorm(_dot(cq.astype(BF16), wuq_ref[...]), bd128, MLA_QK_DIM, vec(8, 512))
        mqn_ref[rows, :] = q.astype(BF16)
        mqr_ref[rows, :] = _rope(q, cm, sm, MLA_ROPE_DIM // 4).astype(BF16)
        ckv = p[:, C_ML_CKV:C_ML_CKV + 128]
        ckv = (ckv * lax.rsqrt(jnp.mean(ckv * ckv, axis=-1, keepdims=True) + RMS_EPS) * vec(7, 128)).astype(BF16)
        k_rope = pltpu.roll(p[:, C_ML_ROPE:C_ML_ROPE + 128], MLA_NOPE_DIM, 1)
        k = _dot(ckv, wuk_ref[...]) + _tile_lanes(k_rope, 512)
        k = _segnorm(k, bd128, MLA_QK_DIM, vec(9, 512))
        mk_ref[rows, :] = _rope(k, cm, sm, MLA_ROPE_DIM // 4).astype(BF16)
        mv_ref[rows, :] = _dot(ckv, wuv_ref[...]).astype(BF16)


_PROJ_OUT_WIDTHS = (1024, 256, 256, 256, 256, 256, 128, 128, 256, 256, 256, 256, 512, 512, 512, 256)


def _proj_call(x, mod, g, w_in, vecs, bds, wuq, wuk, wuv, tables, layer, n_tiles, mod_idx, tab_idx):
    d = x.shape[1]
    tm = TOKEN_TILE
    const2 = lambda i: (0, 0)
    in_specs = [
        pl.BlockSpec((tm, d), lambda i: (i, 0)),
        pl.BlockSpec((None, None, N_MOD, d), lambda i: (layer, mod_idx(i), 0, 0)),
        pl.BlockSpec((1, d), const2),
        pl.BlockSpec((None, d, IN_COLS_PAD), lambda i: (layer, 0, 0)),
        pl.BlockSpec((None, 16, 512), lambda i: (layer, 0, 0)),
        pl.BlockSpec((256, 256), const2),
        pl.BlockSpec((256, 256), const2),
        pl.BlockSpec((256, 256), const2),
        pl.BlockSpec((None, 256, 512), lambda i: (layer, 0, 0)),
        pl.BlockSpec((None, 128, 512), lambda i: (layer, 0, 0)),
        pl.BlockSpec((None, 128, 256), lambda i: (layer, 0, 0)),
    ] + [pl.BlockSpec((tm, LANES), lambda i: (tab_idx(i), 0))] * 6
    rows = n_tiles * tm
    return pl.pallas_call(
        _proj_kernel,
        grid=(n_tiles,),
        in_specs=in_specs,
        out_specs=[pl.BlockSpec((tm, w), lambda i: (i, 0)) for w in _PROJ_OUT_WIDTHS],
        out_shape=[jax.ShapeDtypeStruct((rows, w), BF16) for w in _PROJ_OUT_WIDTHS],
        compiler_params=_cparams(("arbitrary",)),
        name="mixer_projection",
    )(x, mod, g, w_in, vecs, *bds, wuq, wuk, wuv, *tables)


def _attend(blocks, subs, bounded):
    def scores_of(i):
        ks = subs[i][3]
        per_block = []
        for k_ref, rows, _, q_t, bias in blocks:
            s = _dot(k_ref[rows, ks * LANES:(ks + 1) * LANES], q_t[i])
            per_block.append(s if bias is None else s + bias[i])
        return per_block

    scores = {i: scores_of(i) for i in range(min(SCORE_LOOKAHEAD, len(subs)))}
    outs = []
    for i, sub in enumerate(subs):
        if i + SCORE_LOOKAHEAD < len(subs):
            scores[i + SCORE_LOOKAHEAD] = scores_of(i + SCORE_LOOKAHEAD)
        mine = scores.pop(i)
        if not bounded:
            m = functools.reduce(jnp.maximum, [jnp.max(s, axis=0, keepdims=True) for s in mine])
        l = acc = None
        for s, (_, _, v_t, _, _) in zip(mine, blocks):
            p = jnp.exp2(s if bounded else s - m)
            l_blk = jnp.sum(p, axis=0, keepdims=True)
            acc_blk = _dot(_head_rows(v_t, sub[4]), p.astype(BF16))
            l, acc = (l_blk, acc_blk) if l is None else (l + l_blk, acc + acc_blk)
        outs.append(acc / l)
    return outs


def _score_bound(g_q, g_k, dim):
    return 1.01 * dim * (dim ** -0.5 * LOG2E) * jnp.max(jnp.abs(g_q)) * jnp.max(jnp.abs(g_k))


def _transpose_bf16(eye, a):
    return _dot_nt(eye[0:a.shape[1], 0:a.shape[1]], a).astype(BF16)


def _q_slabs_t(eye, q_ref):
    out = []
    for j0 in range(0, q_ref.shape[1], 2 * LANES):
        w = min(2 * LANES, q_ref.shape[1] - j0)
        q_t = _dot_nt(eye[0:w, 0:w], q_ref[:, j0:j0 + w]).astype(BF16)
        out += [q_t[r:r + LANES, :] for r in range(0, w, LANES)]
    return out


def _keep_rows(q_t, lo, hi):
    if lo == 0 and hi == q_t.shape[0]:
        return q_t
    row = lax.broadcasted_iota(jnp.int32, q_t.shape, 0)
    return jnp.where((row >= lo) & (row < hi), q_t, jnp.zeros_like(q_t))


def _head_rows(v_t, head):
    return v_t[head * HEAD_DIM:(head + 1) * HEAD_DIM, :]


def _store_heads(eye, o_ref, o_t_heads):
    for p in range(0, len(o_t_heads), 4):
        o_t = jnp.concatenate(o_t_heads[p:p + 4], axis=0).astype(BF16)
        o_ref[:, p * HEAD_DIM:p * HEAD_DIM + o_t.shape[0]] = _dot_nt(eye, o_t).astype(o_ref.dtype)


def _stage_values_t(eye, v_ref, vt_scr, n_chunks, chunk):
    for c in range(n_chunks):
        for j in range(v_ref.shape[1] // LANES):
            vt_scr[c, j * LANES:(j + 1) * LANES, :] = _transpose_bf16(
                eye, v_ref[c * chunk:(c + 1) * chunk, j * LANES:(j + 1) * LANES])


def _joint_kernel(*refs, subs, n_lat_tiles, seq, chunk, with_ctx_queries, diff_lam_init, bounded):
    if diff_lam_init is None:
        qr_ref, qn_ref, kl_ref, kc_ref, vl_ref, vc_ref, eye_ref, o_ref, vt_scr, vct_scr = refs
    else:
        qr_ref, qn_ref, kl_ref, kc_ref, vl_ref, vc_ref, eye_ref, lam_ref, gsub_ref, o_ref, vt_scr, vct_scr = refs
    t = pl.program_id(2)
    n_chunks = seq // chunk
    eye = eye_ref[...]

    @pl.when(t == 0)
    def _():
        _stage_values_t(eye, vl_ref, vt_scr, n_chunks, chunk)
        _stage_values_t(eye, vc_ref, vct_scr, 1, vc_ref.shape[0])

    def operands(q_ref):
        slabs = _q_slabs_t(eye, q_ref)
        return [_keep_rows(slabs[qs], lo, hi) for (qs, lo, hi, _, _) in subs]

    def finish(outs):
        if diff_lam_init is None:
            _store_heads(eye, o_ref, outs)
            return
        lv = lam_ref[...]
        lam = (jnp.exp(jnp.sum(lv[0:1, :] * lv[1:2, :], axis=-1, keepdims=True))
               - jnp.exp(jnp.sum(lv[2:3, :] * lv[3:4, :], axis=-1, keepdims=True)) + diff_lam_init)
        heads = []
        for o1, o2 in zip(outs[0::2], outs[1::2]):
            d = o1 - lam * o2
            ms = jnp.mean(d * d, axis=0, keepdims=True)
            heads.append((d * lax.rsqrt(ms + RMS_EPS) * gsub_ref[...]) * (1.0 - diff_lam_init))
        _store_heads(eye, o_ref, heads)

    def ctx_block():
        return (kc_ref, slice(None), vct_scr[0], operands(qn_ref), None)

    @pl.when(t < n_lat_tiles)
    def _():
        q_t = operands(qr_ref)
        latent = [(kl_ref, slice(c * chunk, (c + 1) * chunk), vt_scr[c], q_t, None) for c in range(n_chunks)]
        finish(_attend(latent + [ctx_block()], subs, bounded))

    if with_ctx_queries:
        @pl.when(t >= n_lat_tiles)
        def _():
            finish(_attend([ctx_block()], subs, bounded))


def _joint_call(operands, out_rows, *, subs, n_blocks, batch, seq, ctx_len, with_ctx_queries, name, bounded,
                diff_lam_init=None):
    qr, qn, k, v = operands[:4]
    tq = Q_TILE
    n_lat = seq // tq
    nqt = n_lat + (1 if with_ctx_queries else 0)
    ctx_row0 = batch * seq // ctx_len
    q_width, k_width, v_width = qr.shape[1] // n_blocks, k.shape[1] // n_blocks, v.shape[1] // n_blocks
    out_width = BRANCH_WIDTH // n_blocks
    chunk = min(KEY_CHUNK, seq)
    n_chunks = seq // chunk

    def qrow(b, t):
        return jnp.where(t < n_lat, b * n_lat + t, batch * n_lat + b)

    in_specs = [
        pl.BlockSpec((tq, q_width), lambda b, blk, t: (qrow(b, t), blk)),
        pl.BlockSpec((tq, q_width), lambda b, blk, t: (qrow(b, t), blk)),
        pl.BlockSpec((seq, k_width), lambda b, blk, t: (b, blk)),
        pl.BlockSpec((ctx_len, k_width), lambda b, blk, t: (ctx_row0 + b, blk)),
        pl.BlockSpec((seq, v_width), lambda b, blk, t: (b, blk)),
        pl.BlockSpec((ctx_len, v_width), lambda b, blk, t: (ctx_row0 + b, blk)),
        pl.BlockSpec((tq, tq), lambda b, blk, t: (0, 0)),
    ]
    args = [qr, qn, k, k, v, v, jnp.eye(tq, dtype=BF16)]
    if diff_lam_init is not None:
        in_specs += [pl.BlockSpec((4, DIFF_QK_DIM), lambda b, blk, t: (0, 0)),
                     pl.BlockSpec((HEAD_DIM, tq), lambda b, blk, t: (0, 0))]
        args += list(operands[4:])
    return pl.pallas_call(
        functools.partial(_joint_kernel, subs=subs, n_lat_tiles=n_lat, seq=seq, chunk=chunk,
                          with_ctx_queries=with_ctx_queries, diff_lam_init=diff_lam_init, bounded=bounded),
        grid=(batch, n_blocks, nqt),
        in_specs=in_specs,
        out_specs=pl.BlockSpec((tq, out_width), lambda b, blk, t: (qrow(b, t), blk)),
        out_shape=jax.ShapeDtypeStruct((out_rows, BRANCH_WIDTH), BF16),
        scratch_shapes=[pltpu.VMEM((n_chunks, v_width, chunk), BF16), pltpu.VMEM((1, v_width, ctx_len), BF16)],
        compiler_params=_cparams(("arbitrary", "arbitrary", "arbitrary")),
        name=name,
    )(*args)


def _na_kernel(q_ref, kl_ref, kc_ref, vl_ref, vc_ref, ab_ref, eye_ref, o_ref, vt_scr, vct_scr, *, n_lat_tiles,
               bounded):
    t = pl.program_id(2)
    tq = o_ref.shape[0]
    eye = eye_ref[...]
    rows_per_tile = tq // GRID_W
    band_tiles = NA_KEY_ROWS // rows_per_tile
    heads = range(BRANCH_HEADS)

    @pl.when(t == 0)
    def _():
        _stage_values_t(eye, vl_ref, vt_scr, n_lat_tiles, tq)
        _stage_values_t(eye, vc_ref, vct_scr, 1, vc_ref.shape[0])

    subs = tuple((h // 2, (h % 2) * HEAD_DIM, (h % 2 + 1) * HEAD_DIM, h // 2, h) for h in heads)

    def attend(blocks):
        slabs = _q_slabs_t(eye, q_ref)
        q_t = [_keep_rows(slabs[qs], lo, hi) for (qs, lo, hi, _, _) in subs]
        outs = _attend([(k_ref, rows, v_t, q_t, bias) for k_ref, rows, v_t, bias in blocks], subs, bounded)
        _store_heads(eye, o_ref, outs)

    ctx_block = (kc_ref, slice(None), vct_scr[0], None)

    @pl.when(t < n_lat_tiles)
    def _():
        tile0 = jnp.clip(t - NA_WIN_H // 2 // rows_per_tile, 0, n_lat_tiles - band_tiles)
        rows = pl.ds(pl.multiple_of(tile0 * tq, tq), band_tiles * tq)
        v_t = jnp.concatenate([vt_scr[tile0 + j] for j in range(band_tiles)], axis=1)
        attend([(kl_ref, rows, v_t, ab_ref), ctx_block])

    @pl.when(t >= n_lat_tiles)
    def _():
        attend([ctx_block])


def _na_call(operands, out_rows, *, batch, seq, ctx_len, with_ctx_queries, bounded):
    q, k, v, ab = operands
    tq = Q_TILE
    n_lat = seq // tq
    nqt = n_lat + (1 if with_ctx_queries else 0)
    ctx_row0 = batch * seq // ctx_len
    n_keys = NA_KEY_ROWS * GRID_W
    width = BRANCH_WIDTH

    def qrow(b, t):
        return jnp.where(t < n_lat, b * n_lat + t, batch * n_lat + b)

    def pattern(t):
        return jnp.where(t == 0, 0, jnp.where(t >= n_lat - 1, 2, 1))

    return pl.pallas_call(
        functools.partial(_na_kernel, n_lat_tiles=n_lat, bounded=bounded),
        grid=(batch, 1, nqt),
        in_specs=[
            pl.BlockSpec((tq, width), lambda b, blk, t: (qrow(b, t), 0)),
            pl.BlockSpec((seq, width), lambda b, blk, t: (b, 0)),
            pl.BlockSpec((ctx_len, width), lambda b, blk, t: (ctx_row0 + b, 0)),
            pl.BlockSpec((seq, width), lambda b, blk, t: (b, 0)),
            pl.BlockSpec((ctx_len, width), lambda b, blk, t: (ctx_row0 + b, 0)),
            pl.BlockSpec((None, BRANCH_HEADS, n_keys, tq), lambda b, blk, t: (pattern(t), 0, 0, 0)),
            pl.BlockSpec((tq, tq), lambda b, blk, t: (0, 0)),
        ],
        out_specs=pl.BlockSpec((tq, width), lambda b, blk, t: (qrow(b, t), 0)),
        out_shape=jax.ShapeDtypeStruct((out_rows, width), BF16),
        scratch_shapes=[pltpu.VMEM((n_lat, width, tq), BF16), pltpu.VMEM((1, width, ctx_len), BF16)],
        compiler_params=_cparams(("arbitrary", "arbitrary", "arbitrary")),
        name="neighbourhood_attention",
    )(q, k, k, v, v, ab, jnp.eye(tq, dtype=BF16))


def _merge_kernel(x_ref, mod_ref, n_ref, o0_ref, o1_ref, o2_ref, o3_ref, wg_ref, bg_ref, wb_ref, wo_ref,
                  out_ref, y_scr):
    n = n_ref[...]
    branch = [o_ref[...] for o_ref in (o0_ref, o1_ref, o2_ref, o3_ref)]
    for c0 in range(0, n.shape[1], MERGE_SLAB):
        cols = slice(c0, c0 + MERGE_SLAB)
        y = None
        for j in range(N_BRANCH):
            gate = jax.nn.sigmoid(_dot(n, wg_ref[j, :, cols]) + bg_ref[j, :, cols])
            term = gate * _dot(branch[j], wb_ref[j, :, cols])
            y = term if y is None else y + term
        y_scr[:, cols] = y.astype(BF16)
    out_ref[...] = x_ref[...] + mod_ref[5:6, :] * _dot(y_scr[...], wo_ref[...])


def _merge_call(x, mod, n, outs, w_gate, b_gate, w_branch, w_out, layer, n_tiles, mod_idx):
    d = x.shape[1]
    tm = TOKEN_TILE
    row = lambda i: (i, 0)
    return pl.pallas_call(
        _merge_kernel,
        grid=(n_tiles,),
        in_specs=[
            pl.BlockSpec((tm, d), row),
            pl.BlockSpec((None, None, N_MOD, d), lambda i: (layer, mod_idx(i), 0, 0)),
            pl.BlockSpec((tm, d), row),
        ] + [pl.BlockSpec((tm, BRANCH_WIDTH), row)] * 4 + [
            _resident((None, N_BRANCH, d, d), lambda i: (layer, 0, 0, 0)),
            _resident((None, N_BRANCH, 1, d), lambda i: (layer, 0, 0, 0)),
            _resident((None, N_BRANCH, BRANCH_WIDTH, d), lambda i: (layer, 0, 0, 0)),
            _resident((None, d, d), lambda i: (layer, 0, 0)),
        ],
        out_specs=pl.BlockSpec((tm, d), row),
        out_shape=jax.ShapeDtypeStruct((n_tiles * tm, d), F32),
        scratch_shapes=[pltpu.VMEM((tm, d), BF16)],
        compiler_params=_cparams(("arbitrary",)),
        name="branch_merge",
    )(x, mod, n, *outs, w_gate, b_gate, w_branch, w_out)


def _rope_tables(seq, rot_dim, pad_rows):
    t = jnp.arange(seq, dtype=jnp.int32)
    row = (t // GRID_W).astype(F32)
    col = (t % GRID_W).astype(F32)
    n_freq = rot_dim // 4
    inv = ROPE_THETA ** (-jnp.arange(n_freq, dtype=F32) / n_freq)
    ar = row[:, None] * inv[None, :]
    ac = col[:, None] * inv[None, :]
    ang = jnp.concatenate([ar, ar, ac, ac], axis=-1)
    sign = jnp.tile(jnp.concatenate([-jnp.ones((n_freq,), F32), jnp.ones((n_freq,), F32)]), 2)
    cos = jnp.concatenate([jnp.cos(ang), jnp.ones((pad_rows, rot_dim), F32)], axis=0)
    sin = jnp.concatenate([jnp.sin(ang) * sign[None, :], jnp.zeros((pad_rows, rot_dim), F32)], axis=0)
    return cos, sin


def _all_rope_tables(seq, pad_rows):
    c64, s64 = _rope_tables(seq, HEAD_DIM, pad_rows)
    c32, s32 = _rope_tables(seq, DIFF_QK_DIM, pad_rows)
    rows = seq + pad_rows
    ones = lambda w: jnp.ones((rows, w), F32)
    zeros = lambda w: jnp.zeros((rows, w), F32)
    pad = LANES - MLA_QK_DIM
    cm = jnp.concatenate([ones(MLA_NOPE_DIM), c32, ones(pad)], axis=-1)
    sm = jnp.concatenate([zeros(MLA_NOPE_DIM), s32, zeros(pad)], axis=-1)
    return (jnp.tile(c64, (1, 2)), jnp.tile(s64, (1, 2)), jnp.tile(c32, (1, 4)), jnp.tile(s32, (1, 4)), cm, sm)


def _block_diag_ones(width, seg):
    idx = np.arange(width) // seg
    return jnp.asarray(idx[:, None] == idx[None, :], dtype=BF16)


def _na_bias_tables(rpb, grid_rows):
    rows_per_tile = Q_TILE // GRID_W
    wh = min(NA_WIN_H, grid_rows)
    kinds = ((0, 0), (rows_per_tile, 0), (grid_rows - rows_per_tile, grid_rows - NA_KEY_ROWS))
    col = np.arange(GRID_W)
    c0 = np.clip(col - NA_WIN_W // 2, 0, GRID_W - NA_WIN_W)
    col_ok = (col[:, None] >= c0[None, :]) & (col[:, None] < c0[None, :] + NA_WIN_W)
    d_col = np.clip(col[:, None] - col[None, :], -(NA_WIN_W - 1), NA_WIN_W - 1) + (NA_WIN_W - 1)
    col_hot = (d_col[..., None] == np.arange(2 * NA_WIN_W - 1)).astype(np.float32)
    blocks = jnp.einsum("hab,dcb->hadc", rpb.astype(F32), jnp.asarray(col_hot), precision=lax.Precision.HIGHEST)
    blocks = jnp.where(jnp.asarray(col_ok)[None, None], blocks * LOG2E, NEG_BIG)
    masked = jnp.full(blocks.shape[:1] + (GRID_W, GRID_W), NEG_BIG, F32)
    tables = []
    for q_row0, k_row0 in kinds:
        r = q_row0 + np.arange(rows_per_tile)
        kr = k_row0 + np.arange(NA_KEY_ROWS)
        r0 = np.clip(r - wh // 2, 0, grid_rows - wh)
        key_rows = []
        for kk in range(NA_KEY_ROWS):
            in_window = (kr[kk] >= r0) & (kr[kk] < r0 + wh)
            key_rows.append(jnp.concatenate(
                [blocks[:, kr[kk] - r[qr] + (NA_WIN_H - 1)] if in_window[qr] else masked
                 for qr in range(rows_per_tile)], axis=-1))
        tables.append(jnp.concatenate(key_rows, axis=-2))
    return jnp.stack(tables, axis=0)


def _pad_heads(w, head_w, to_w):
    lead = w.shape[:-1]
    w = w.reshape(lead + (BRANCH_HEADS, head_w))
    w = jnp.pad(w, [(0, 0)] * len(lead) + [(0, 0), (0, to_w - head_w)])
    return w.reshape(lead + (BRANCH_HEADS * to_w,))


def _gqa_head_order(a, axis):
    parts = jnp.split(a, BRANCH_HEADS, axis=axis)
    return jnp.concatenate([parts[0], parts[2], parts[1], parts[3]], axis=axis)


def kernel(x, c, ctx, c_ctx, w_ada, b_ada, norm_g, ffn_w1, ffn_w3, ffn_w2, w_in, na_g_q, na_g_k, na_rpb, gqa_g_q, gqa_g_k, diff_g_q, diff_g_k, diff_lambda, diff_g_sub, mla_g_cq, mla_g_ckv, mla_w_uq, mla_w_ukv, mla_g_q, mla_g_k, w_gate, b_gate, w_branch, w_out):
    batch, seq, d = x.shape
    ctx_len = ctx.shape[1]
    depth = w_ada.shape[0]
    tm = TOKEN_TILE
    assert d == D_MODEL and ctx_len == Q_TILE and seq % tm == 0 and (batch * ctx_len) % tm == 0
    assert seq % min(KEY_CHUNK, seq) == 0 and seq // GRID_W >= NA_KEY_ROWS and batch < 16
    n_lat_rows = batch * seq
    lat_tiles = n_lat_rows // tm
    all_tiles = lat_tiles + batch * ctx_len // tm
    tiles_per_batch = seq // tm

    def mod_idx(i):
        return jnp.where(i < lat_tiles, i // tiles_per_batch, batch)

    def tab_idx(i):
        return jnp.where(i < lat_tiles, i % tiles_per_batch, tiles_per_batch)

    c16 = jnp.concatenate([c, c_ctx[None, :], jnp.zeros((16 - batch - 1, d), F32)], axis=0)
    mods = _mod_call(c16, w_ada, b_ada).reshape(depth, 16, N_MOD, d)
    w1, w3, w2 = ffn_w1.astype(BF16), ffn_w3.astype(BF16), ffn_w2.astype(BF16)
    na_w, gq_w, df_w, ml_w = jnp.split(w_in, [768, 1280, 2048], axis=-1)
    w_in_p = jnp.concatenate([
        na_w, _gqa_head_order(gq_w[..., :256], -1), gq_w[..., 256:], df_w, ml_w,
        jnp.zeros((depth, d, IN_COLS_PAD - w_in.shape[-1]), F32)], axis=-1).astype(BF16)
    wuq = _pad_heads(mla_w_uq, MLA_QK_DIM, LANES).astype(BF16)
    wukv = mla_w_ukv.reshape(depth, MLA_KV_RANK, BRANCH_HEADS, MLA_NOPE_DIM + HEAD_DIM)
    wuk = _pad_heads(wukv[..., :MLA_NOPE_DIM].reshape(depth, MLA_KV_RANK, -1), MLA_NOPE_DIM, LANES).astype(BF16)
    wuv = wukv[..., MLA_NOPE_DIM:].reshape(depth, MLA_KV_RANK, BRANCH_WIDTH).astype(BF16)

    def vec_row(v, reps=1, scale=1.0):
        v = jnp.tile(v, (1, reps)) * scale
        return jnp.pad(v, ((0, 0), (0, 512 - v.shape[-1])))

    pad_g = lambda g: jnp.pad(g, ((0, 0), (0, LANES - MLA_QK_DIM)))
    vec_rows = [
        vec_row(na_g_q, 4, HEAD_DIM ** -0.5 * LOG2E), vec_row(na_g_k, 4),
        vec_row(gqa_g_q, 4, HEAD_DIM ** -0.5 * LOG2E), vec_row(gqa_g_k, 2),
        vec_row(diff_g_q, 8, DIFF_QK_DIM ** -0.5 * LOG2E), vec_row(diff_g_k, 8),
        vec_row(mla_g_cq), vec_row(mla_g_ckv),
        vec_row(pad_g(mla_g_q), 4, MLA_QK_DIM ** -0.5 * LOG2E), vec_row(pad_g(mla_g_k), 4),
    ]
    vecs = jnp.stack(vec_rows + [jnp.zeros_like(vec_rows[0])] * (16 - len(vec_rows)), axis=1)
    bds = (_block_diag_ones(256, HEAD_DIM), _block_diag_ones(256, DIFF_QK_DIM), _block_diag_ones(256, LANES))
    tables = _all_rope_tables(seq, tm)
    na_ab = [_na_bias_tables(na_rpb[i], seq // GRID_W) for i in range(depth)]
    gsub = jnp.broadcast_to(diff_g_sub[:, :, None], (depth, HEAD_DIM, Q_TILE))
    wg, wo = w_gate.astype(BF16), w_out.astype(BF16)
    wb = jnp.concatenate([w_branch[:, 0:1], _gqa_head_order(w_branch[:, 1:2], 2), w_branch[:, 2:]], axis=1).astype(BF16)
    bg = b_gate.reshape(depth, N_BRANCH, 1, d)

    xa, x_tail = x.reshape(n_lat_rows, d), ctx.reshape(batch * ctx_len, d)

    gqa_subs = ((0, 0, 64, 0, 0), (0, 64, 128, 0, 1), (1, 0, 64, 0, 0), (1, 64, 128, 0, 1))
    diff_subs = tuple((h // 2, 32 * (2 * (h % 2) + j), 32 * (2 * (h % 2) + j + 1), h // 2, h)
                      for h in range(BRANCH_HEADS) for j in range(2))
    mla_subs = tuple((h, 0, LANES, h, h) for h in range(BRANCH_HEADS))

    for i in range(depth):
        need_ctx = i < depth - 1
        mod = mods
        g = norm_g[i].reshape(3, 1, d)
        xa = _ffn_call(xa, mod, g[0], w1, w3, w2, i, 0, 0, all_tiles, mod_idx, x_tail=x_tail)
        x_tail = None
        (n, naq, nak, nav, gqr, gqn, gk, gv, dqr, dqn, dk, dv, mqr, mqn, mk, mv) = _proj_call(
            xa, mod, g[1], w_in_p, vecs, bds, wuq, wuk, wuv, tables, i, all_tiles, mod_idx, tab_idx)
        out_rows = xa.shape[0] if need_ctx else n_lat_rows
        common = dict(batch=batch, seq=seq, ctx_len=ctx_len, with_ctx_queries=need_ctx)
        lam_init = 0.8 - 0.6 * math.exp(-0.3 * i)
        def dispatch(call, bound, *operands, **kw):
            return lax.cond(bound <= SCORE_BOUND,
                            lambda *a: call(a, out_rows, bounded=True, **kw, **common),
                            lambda *a: call(a, out_rows, bounded=False, **kw, **common), *operands)

        joint = functools.partial(dispatch, _joint_call)
        na_bound = _score_bound(na_g_q[i], na_g_k[i], HEAD_DIM) + LOG2E * jnp.max(jnp.abs(na_rpb[i]))

        outs = [
            dispatch(_na_call, na_bound, naq, nak, nav, na_ab[i]),
            joint(_score_bound(gqa_g_q[i], gqa_g_k[i], HEAD_DIM), gqr, gqn, gk, gv,
                  subs=gqa_subs, n_blocks=1, name="gqa_attention"),
            joint(_score_bound(diff_g_q[i], diff_g_k[i], DIFF_QK_DIM), dqr, dqn, dk, dv, diff_lambda[i], gsub[i],
                  subs=diff_subs, n_blocks=1, name="diff_attention", diff_lam_init=lam_init),
            joint(_score_bound(mla_g_q[i], mla_g_k[i], MLA_QK_DIM), mqr, mqn, mk, mv,
                  subs=mla_subs, n_blocks=1, name="mla_attention"),
        ]
        n_tiles = all_tiles if need_ctx else lat_tiles
        xa = _merge_call(xa, mod, n, outs, wg, bg, wb, wo, i, n_tiles, mod_idx)
        xa = _ffn_call(xa, mod, g[2], w1, w3, w2, i, 1, 6, n_tiles, mod_idx)
    return xa.reshape(batch, seq, d)
```

```python
import functools
import math

import jax
import jax.numpy as jnp
import numpy as np
from jax import lax
from jax.experimental import pallas as pl
from jax.experimental.pallas import tpu as pltpu

F32 = jnp.float32
BF16 = jnp.bfloat16

D_MODEL = 1024
GRID_W = 64
ROPE_THETA = 10000.0
RMS_EPS = 1e-6
HEAD_DIM = 64
N_BRANCH = 4
BRANCH_HEADS = 4
BRANCH_WIDTH = 256
NA_WIN_H = 8
NA_WIN_W = 16
DIFF_QK_DIM = 32
MLA_Q_RANK = 256
MLA_KV_RANK = 128
MLA_NOPE_DIM = 64
MLA_ROPE_DIM = 32
MLA_QK_DIM = 96
D_FF = 2816
N_MOD = 9
LOG2E = 1.4426950408889634

LANES = 128
TOKEN_TILE = 512
Q_TILE = 256
KEY_CHUNK = 4096
FF_SLAB = 256
MERGE_SLAB = 256
PROJ_ROWS = 256
NA_KEY_ROWS = 12
SCORE_LOOKAHEAD = 2
NEG_BIG = -1e30
SCORE_BOUND = 40.0
VMEM_LIMIT = 56 * 1024 * 1024

C_NA_Q, C_NA_K, C_NA_V = 0, 256, 512
C_GQ_Q, C_GQ_K, C_GQ_V = 768, 1024, 1152
C_DF_Q, C_DF_K, C_DF_V = 1280, 1536, 1792
C_ML_CQ, C_ML_CKV, C_ML_ROPE = 2048, 2304, 2432
IN_COLS_PAD = 2560


def _cparams(sem):
    return pltpu.CompilerParams(dimension_semantics=sem, vmem_limit_bytes=VMEM_LIMIT)


def _dot(a, b):
    return jnp.dot(a, b, preferred_element_type=F32)


def _dot_nt(a, b):
    return lax.dot_general(a, b, (((1,), (1,)), ((), ())), preferred_element_type=F32)


def _silu(a):
    return a * jax.nn.sigmoid(a)


def _modulated_norm(x, g, shift, scale):
    ms = jnp.mean(x * x, axis=-1, keepdims=True)
    return (x * lax.rsqrt(ms + RMS_EPS) * g) * (1.0 + scale) + shift


def _mod_kernel(c_ref, w_ref, b_ref, o_ref):
    c = c_ref[...]
    o_ref[...] = _dot(_silu(c).astype(BF16), w_ref[...].astype(BF16)) + b_ref[...]


def _mod_call(c16, w_ada, b_ada):
    depth, d, nd = w_ada.shape
    tn = 1024
    return pl.pallas_call(
        _mod_kernel,
        grid=(depth, nd // tn),
        in_specs=[
            pl.BlockSpec((16, d), lambda l, j: (0, 0)),
            pl.BlockSpec((None, d, tn), lambda l, j: (l, 0, j)),
            pl.BlockSpec((None, 1, tn), lambda l, j: (l, 0, j)),
        ],
        out_specs=pl.BlockSpec((None, 16, tn), lambda l, j: (l, 0, j)),
        out_shape=jax.ShapeDtypeStruct((depth, 16, nd), F32),
        compiler_params=_cparams(("arbitrary", "arbitrary")),
        name="adaln_mod",
    )(c16, w_ada, b_ada.reshape(depth, 1, nd))


def _ffn_kernel(*refs, row0, head_tiles):
    if head_tiles is None:
        x_ref, mod_ref, g_ref, w1_ref, w3_ref, w2_ref, o_ref, act_scr = refs
        x = x_ref[...]
    else:
        x_ref, tail_ref, mod_ref, g_ref, w1_ref, w3_ref, w2_ref, o_ref, act_scr = refs
        x = jnp.where(pl.program_id(0) < head_tiles, x_ref[...], tail_ref[...])
    h = _modulated_norm(x, g_ref[...], mod_ref[row0:row0 + 1, :], mod_ref[row0 + 1:row0 + 2, :]).astype(BF16)
    for f0 in range(0, D_FF, FF_SLAB):
        a = _dot(h, w1_ref[:, f0:f0 + FF_SLAB])
        b = _dot(h, w3_ref[:, f0:f0 + FF_SLAB])
        act_scr[:, f0:f0 + FF_SLAB] = (_silu(a) * b).astype(BF16)
    o_ref[...] = x + (0.5 * mod_ref[row0 + 2:row0 + 3, :]) * _dot(act_scr[...], w2_ref[...])


def _resident(block_shape, index_map):
    return pl.BlockSpec(block_shape, index_map, pipeline_mode=pl.Buffered(1))


def _ffn_call(x, mod, g, w1, w3, w2, layer, half, row0, n_tiles, mod_idx, x_tail=None):
    d = x.shape[1]
    tm = TOKEN_TILE
    if x_tail is None:
        head_tiles, x_specs, xs = None, [pl.BlockSpec((tm, d), lambda i: (i, 0))], [x]
    else:
        head_tiles = x.shape[0] // tm
        x_specs = [pl.BlockSpec((tm, d), lambda i: (jnp.minimum(i, head_tiles - 1), 0)),
                   pl.BlockSpec((tm, d), lambda i: (jnp.maximum(i - head_tiles, 0), 0))]
        xs = [x, x_tail]
    return pl.pallas_call(
        functools.partial(_ffn_kernel, row0=row0, head_tiles=head_tiles),
        grid=(n_tiles,),
        in_specs=x_specs + [
            pl.BlockSpec((None, None, N_MOD, d), lambda i: (layer, mod_idx(i), 0, 0)),
            _resident((1, d), lambda i: (0, 0)),
            _resident((None, None, d, D_FF), lambda i: (layer, half, 0, 0)),
            _resident((None, None, d, D_FF), lambda i: (layer, half, 0, 0)),
            _resident((None, None, D_FF, d), lambda i: (layer, half, 0, 0)),
        ],
        out_specs=pl.BlockSpec((tm, d), lambda i: (i, 0)),
        out_shape=jax.ShapeDtypeStruct((n_tiles * tm, d), F32),
        scratch_shapes=[pltpu.VMEM((tm, D_FF), BF16)],
        compiler_params=_cparams(("arbitrary",)),
        name="ffn_half_step",
    )(*xs, mod, g, w1, w3, w2)


def _segnorm(x, bd, seg_len, gain):
    x2 = x * x
    hi = x2.astype(BF16)
    lo = (x2 - hi.astype(F32)).astype(BF16)
    groups = []
    for c0 in range(0, x.shape[1], bd.shape[0]):
        w = min(bd.shape[0], x.shape[1] - c0)
        groups.append(_dot(hi[:, c0:c0 + w], bd[0:w, 0:w]) + _dot(lo[:, c0:c0 + w], bd[0:w, 0:w]))
    ss = groups[0] if len(groups) == 1 else jnp.concatenate(groups, axis=-1)
    return x * lax.rsqrt(ss * (1.0 / seg_len) + RMS_EPS) * gain


def _tile_lanes(t, width):
    reps = width // t.shape[-1]
    return t if reps == 1 else jnp.concatenate([t] * reps, axis=-1)


def _rope(x, cos, sin_signed, quarter):
    width = x.shape[-1]
    lane = lax.broadcasted_iota(jnp.int32, x.shape, 1)
    first = (lane % (2 * quarter)) < quarter
    rot = jnp.where(first, pltpu.roll(x, width - quarter, 1), pltpu.roll(x, quarter, 1))
    return x * _tile_lanes(cos, width) + rot * _tile_lanes(sin_signed, width)


def _proj_kernel(x_ref, mod_ref, g_ref, w_ref, vec_ref, bd64_ref, bd32_ref, bd128_ref,
                 wuq_ref, wuk_ref, wuv_ref, c64_ref, s64_ref, c32_ref, s32_ref, cm_ref, sm_ref,
                 n_ref, naq_ref, nak_ref, nav_ref,
                 gqr_ref, gqn_ref, gk_ref, gv_ref,
                 dqr_ref, dqn_ref, dk_ref, dv_ref,
                 mqr_ref, mqn_ref, mk_ref, mv_ref):
    bd64, bd32, bd128 = bd64_ref[...], bd32_ref[...], bd128_ref[...]

    def vec(row, width):
        return vec_ref[row:row + 1, 0:width]

    groups = [slice(r, r + PROJ_ROWS) for r in range(0, x_ref.shape[0], PROJ_ROWS)]
    projected = []
    for rows in groups:
        n = _modulated_norm(x_ref[rows, :], g_ref[...], mod_ref[3:4, :], mod_ref[4:5, :]).astype(BF16)
        n_ref[rows, :] = n
        projected.append(_dot(n, w_ref[...]))

    for rows, p in zip(groups, projected):
        c64, s64 = c64_ref[rows, :], s64_ref[rows, :]
        c32, s32 = c32_ref[rows, :], s32_ref[rows, :]
        cm, sm = cm_ref[rows, :], sm_ref[rows, :]

        naq_ref[rows, :] = _segnorm(p[:, C_NA_Q:C_NA_Q + 256], bd64, HEAD_DIM, vec(0, 256)).astype(BF16)
        nak_ref[rows, :] = _segnorm(p[:, C_NA_K:C_NA_K + 256], bd64, HEAD_DIM, vec(1, 256)).astype(BF16)
        nav_ref[rows, :] = p[:, C_NA_V:C_NA_V + 256].astype(BF16)

        q = _segnorm(p[:, C_GQ_Q:C_GQ_Q + 256], bd64, HEAD_DIM, vec(2, 256))
        gqn_ref[rows, :] = q.astype(BF16)
        gqr_ref[rows, :] = _rope(q, c64, s64, HEAD_DIM // 4).astype(BF16)
        k = _segnorm(p[:, C_GQ_K:C_GQ_K + 128], bd64, HEAD_DIM, vec(3, 128))
        gk_ref[rows, :] = _rope(k, c64, s64, HEAD_DIM // 4).astype(BF16)
        gv_ref[rows, :] = p[:, C_GQ_V:C_GQ_V + 128].astype(BF16)

        q = _segnorm(p[:, C_DF_Q:C_DF_Q + 256], bd32, DIFF_QK_DIM, vec(4, 256))
        dqn_ref[rows, :] = q.astype(BF16)
        dqr_ref[rows, :] = _rope(q, c32, s32, DIFF_QK_DIM // 4).astype(BF16)
        k = _segnorm(p[:, C_DF_K:C_DF_K + 256], bd32, DIFF_QK_DIM, vec(5, 256))
        dk_ref[rows, :] = _rope(k, c32, s32, DIFF_QK_DIM // 4).astype(BF16)
        dv_ref[rows, :] = p[:, C_DF_V:C_DF_V + 256].astype(BF16)

        cq = p[:, C_ML_CQ:C_ML_CQ + MLA_Q_RANK]
        cq = cq * lax.rsqrt(jnp.mean(cq * cq, axis=-1, keepdims=True) + RMS_EPS) * vec(6, 256)
        q = _segnorm(_dot(cq.astype(BF16), wuq_ref[...]), bd128, MLA_QK_DIM, vec(8, 512))
        mqn_ref[rows, :] = q.astype(BF16)
        mqr_ref[rows, :] = _rope(q, cm, sm, MLA_ROPE_DIM // 4).astype(BF16)
        ckv = p[:, C_ML_CKV:C_ML_CKV + MLA_KV_RANK]
        ckv = (ckv * lax.rsqrt(jnp.mean(ckv * ckv, axis=-1, keepdims=True) + RMS_EPS) * vec(7, 128)).astype(BF16)
        k_rope = pltpu.roll(p[:, C_ML_ROPE:C_ML_ROPE + 128], MLA_NOPE_DIM, 1)
        k = _dot(ckv, wuk_ref[...]) + _tile_lanes(k_rope, 512)
        k = _segnorm(k, bd128, MLA_QK_DIM, vec(9, 512))
        mk_ref[rows, :] = _rope(k, cm, sm, MLA_ROPE_DIM // 4).astype(BF16)
        mv_ref[rows, :] = _dot(ckv, wuv_ref[...]).astype(BF16)


_PROJ_OUT_WIDTHS = (1024, 256, 256, 256, 256, 256, 128, 128, 256, 256, 256, 256, 512, 512, 512, 256)


def _proj_call(x, mod, g, w_in, vecs, bds, wuq, wuk, wuv, tables, layer, n_tiles, mod_idx, tab_idx):
    d = x.shape[1]
    tm = TOKEN_TILE
    const2 = lambda i: (0, 0)
    in_specs = [
        pl.BlockSpec((tm, d), lambda i: (i, 0)),
        pl.BlockSpec((None, None, N_MOD, d), lambda i: (layer, mod_idx(i), 0, 0)),
        pl.BlockSpec((1, d), const2),
        pl.BlockSpec((None, d, IN_COLS_PAD), lambda i: (layer, 0, 0)),
        pl.BlockSpec((None, 16, 512), lambda i: (layer, 0, 0)),
        pl.BlockSpec((256, 256), const2),
        pl.BlockSpec((256, 256), const2),
        pl.BlockSpec((256, 256), const2),
        pl.BlockSpec((None, 256, 512), lambda i: (layer, 0, 0)),
        pl.BlockSpec((None, 128, 512), lambda i: (layer, 0, 0)),
        pl.BlockSpec((None, 128, 256), lambda i: (layer, 0, 0)),
    ] + [pl.BlockSpec((tm, LANES), lambda i: (tab_idx(i), 0))] * 6
    rows = n_tiles * tm
    return pl.pallas_call(
        _proj_kernel,
        grid=(n_tiles,),
        in_specs=in_specs,
        out_specs=[pl.BlockSpec((tm, w), lambda i: (i, 0)) for w in _PROJ_OUT_WIDTHS],
        out_shape=[jax.ShapeDtypeStruct((rows, w), BF16) for w in _PROJ_OUT_WIDTHS],
        compiler_params=_cparams(("arbitrary",)),
        name="mixer_projection",
    )(x, mod, g, w_in, vecs, *bds, wuq, wuk, wuv, *tables)


def _attend(blocks, subs, bounded, lookahead):
    def scores_of(i):
        ks = subs[i][3]
        per_block = []
        for k_ref, rows, _, q_t, bias in blocks:
            s = _dot(k_ref[rows, ks * LANES:(ks + 1) * LANES], q_t[i])
            per_block.append(s if bias is None else s + bias[i])
        return per_block

    scores = {i: scores_of(i) for i in range(min(lookahead, len(subs)))}
    outs = []
    for i, sub in enumerate(subs):
        if i + lookahead < len(subs):
            scores[i + lookahead] = scores_of(i + lookahead)
        mine = scores.pop(i)
        if not bounded:
            m = functools.reduce(jnp.maximum, [jnp.max(s, axis=0, keepdims=True) for s in mine])
        l = acc = None
        for s, (_, _, v_t, _, _) in zip(mine, blocks):
            p = jnp.exp2(s if bounded else s - m)
            l_blk = jnp.sum(p, axis=0, keepdims=True)
            acc_blk = _dot(_head_rows(v_t, sub[4]), p.astype(BF16))
            l, acc = (l_blk, acc_blk) if l is None else (l + l_blk, acc + acc_blk)
        outs.append(acc / l)
    return outs


def _score_bound(g_q, g_k, dim):
    return 1.01 * dim * (dim ** -0.5 * LOG2E) * jnp.max(jnp.abs(g_q)) * jnp.max(jnp.abs(g_k))


def _transpose_bf16(eye, a):
    return _dot_nt(eye[0:a.shape[1], 0:a.shape[1]], a).astype(BF16)


def _q_slabs_t(eye, q_ref):
    out = []
    for j0 in range(0, q_ref.shape[1], 2 * LANES):
        w = min(2 * LANES, q_ref.shape[1] - j0)
        q_t = _dot_nt(eye[0:w, 0:w], q_ref[:, j0:j0 + w]).astype(BF16)
        out += [q_t[r:r + LANES, :] for r in range(0, w, LANES)]
    return out


def _keep_rows(q_t, lo, hi):
    if lo == 0 and hi == q_t.shape[0]:
        return q_t
    row = lax.broadcasted_iota(jnp.int32, q_t.shape, 0)
    return jnp.where((row >= lo) & (row < hi), q_t, jnp.zeros_like(q_t))


def _head_rows(v_t, head):
    return v_t[head * HEAD_DIM:(head + 1) * HEAD_DIM, :]


def _store_heads(eye, o_ref, o_t_heads):
    for p in range(0, len(o_t_heads), 4):
        o_t = jnp.concatenate(o_t_heads[p:p + 4], axis=0).astype(BF16)
        o_ref[:, p * HEAD_DIM:p * HEAD_DIM + o_t.shape[0]] = _dot_nt(eye, o_t).astype(o_ref.dtype)


def _stage_values_t(eye, v_ref, vt_scr, n_chunks, chunk):
    for c in range(n_chunks):
        for j in range(v_ref.shape[1] // LANES):
            vt_scr[c, j * LANES:(j + 1) * LANES, :] = _transpose_bf16(
                eye, v_ref[c * chunk:(c + 1) * chunk, j * LANES:(j + 1) * LANES])


def _joint_kernel(*refs, subs, n_lat_tiles, seq, chunk, with_ctx_queries, diff_lam_init, bounded, lookahead):
    if diff_lam_init is None:
        qr_ref, qn_ref, kl_ref, kc_ref, vl_ref, vc_ref, eye_ref, o_ref, vt_scr, vct_scr = refs
    else:
        qr_ref, qn_ref, kl_ref, kc_ref, vl_ref, vc_ref, eye_ref, lam_ref, gsub_ref, o_ref, vt_scr, vct_scr = refs
    t = pl.program_id(2)
    n_chunks = seq // chunk
    eye = eye_ref[...]

    @pl.when(t == 0)
    def _():
        _stage_values_t(eye, vl_ref, vt_scr, n_chunks, chunk)
        _stage_values_t(eye, vc_ref, vct_scr, 1, vc_ref.shape[0])

    def operands(q_ref):
        slabs = _q_slabs_t(eye, q_ref)
        return [_keep_rows(slabs[qs], lo, hi) for (qs, lo, hi, _, _) in subs]

    def finish(outs):
        if diff_lam_init is None:
            _store_heads(eye, o_ref, outs)
            return
        lv = lam_ref[...]
        lam = (jnp.exp(jnp.sum(lv[0:1, :] * lv[1:2, :], axis=-1, keepdims=True))
               - jnp.exp(jnp.sum(lv[2:3, :] * lv[3:4, :], axis=-1, keepdims=True)) + diff_lam_init)
        heads = []
        for o1, o2 in zip(outs[0::2], outs[1::2]):
            d = o1 - lam * o2
            ms = jnp.mean(d * d, axis=0, keepdims=True)
            heads.append((d * lax.rsqrt(ms + RMS_EPS) * gsub_ref[...]) * (1.0 - diff_lam_init))
        _store_heads(eye, o_ref, heads)

    def ctx_block():
        return (kc_ref, slice(None), vct_scr[0], operands(qn_ref), None)

    @pl.when(t < n_lat_tiles)
    def _():
        q_t = operands(qr_ref)
        latent = [(kl_ref, slice(c * chunk, (c + 1) * chunk), vt_scr[c], q_t, None) for c in range(n_chunks)]
        finish(_attend(latent + [ctx_block()], subs, bounded, lookahead))

    if with_ctx_queries:
        @pl.when(t >= n_lat_tiles)
        def _():
            finish(_attend([ctx_block()], subs, bounded, lookahead))


def _joint_call(operands, out_rows, *, subs, n_blocks, batch, seq, ctx_len, with_ctx_queries, name, bounded,
                diff_lam_init=None, lookahead=SCORE_LOOKAHEAD):
    qr, qn, k, v = operands[:4]
    tq = Q_TILE
    n_lat = seq // tq
    nqt = n_lat + (1 if with_ctx_queries else 0)
    ctx_row0 = batch * seq // ctx_len
    q_width, k_width, v_width = qr.shape[1] // n_blocks, k.shape[1] // n_blocks, v.shape[1] // n_blocks
    out_width = BRANCH_WIDTH // n_blocks
    chunk = min(KEY_CHUNK, seq)
    n_chunks = seq // chunk

    def qrow(b, t):
        return jnp.where(t < n_lat, b * n_lat + t, batch * n_lat + b)

    in_specs = [
        pl.BlockSpec((tq, q_width), lambda b, blk, t: (qrow(b, t), blk)),
        pl.BlockSpec((tq, q_width), lambda b, blk, t: (qrow(b, t), blk)),
        pl.BlockSpec((seq, k_width), lambda b, blk, t: (b, blk)),
        pl.BlockSpec((ctx_len, k_width), lambda b, blk, t: (ctx_row0 + b, blk)),
        pl.BlockSpec((seq, v_width), lambda b, blk, t: (b, blk)),
        pl.BlockSpec((ctx_len, v_width), lambda b, blk, t: (ctx_row0 + b, blk)),
        pl.BlockSpec((tq, tq), lambda b, blk, t: (0, 0)),
    ]
    args = [qr, qn, k, k, v, v, jnp.eye(tq, dtype=BF16)]
    if diff_lam_init is not None:
        in_specs += [pl.BlockSpec((4, DIFF_QK_DIM), lambda b, blk, t: (0, 0)),
                     pl.BlockSpec((HEAD_DIM, tq), lambda b, blk, t: (0, 0))]
        args += list(operands[4:])
    return pl.pallas_call(
        functools.partial(_joint_kernel, subs=subs, n_lat_tiles=n_lat, seq=seq, chunk=chunk,
                          with_ctx_queries=with_ctx_queries, diff_lam_init=diff_lam_init, bounded=bounded,
                          lookahead=lookahead),
        grid=(batch, n_blocks, nqt),
        in_specs=in_specs,
        out_specs=pl.BlockSpec((tq, out_width), lambda b, blk, t: (qrow(b, t), blk)),
        out_shape=jax.ShapeDtypeStruct((out_rows, BRANCH_WIDTH), BF16),
        scratch_shapes=[pltpu.VMEM((n_chunks, v_width, chunk), BF16), pltpu.VMEM((1, v_width, ctx_len), BF16)],
        compiler_params=_cparams(("arbitrary", "arbitrary", "arbitrary")),
        name=name,
    )(*args)


def _na_kernel(q_ref, kl_ref, kc_ref, vl_ref, vc_ref, ab_ref, eye_ref, o_ref, vt_scr, vct_scr, *, n_lat_tiles,
               bounded):
    t = pl.program_id(2)
    tq = o_ref.shape[0]
    eye = eye_ref[...]
    rows_per_tile = tq // GRID_W
    band_tiles = NA_KEY_ROWS // rows_per_tile
    heads = range(BRANCH_HEADS)

    @pl.when(t == 0)
    def _():
        _stage_values_t(eye, vl_ref, vt_scr, n_lat_tiles, tq)
        _stage_values_t(eye, vc_ref, vct_scr, 1, vc_ref.shape[0])

    subs = tuple((h // 2, (h % 2) * HEAD_DIM, (h % 2 + 1) * HEAD_DIM, h // 2, h) for h in heads)

    def attend(blocks):
        slabs = _q_slabs_t(eye, q_ref)
        q_t = [_keep_rows(slabs[qs], lo, hi) for (qs, lo, hi, _, _) in subs]
        outs = _attend([(k_ref, rows, v_t, q_t, bias) for k_ref, rows, v_t, bias in blocks], subs, bounded,
                       SCORE_LOOKAHEAD)
        _store_heads(eye, o_ref, outs)

    ctx_block = (kc_ref, slice(None), vct_scr[0], None)

    @pl.when(t < n_lat_tiles)
    def _():
        tile0 = jnp.clip(t - NA_WIN_H // 2 // rows_per_tile, 0, n_lat_tiles - band_tiles)
        rows = pl.ds(pl.multiple_of(tile0 * tq, tq), band_tiles * tq)
        v_t = jnp.concatenate([vt_scr[tile0 + j] for j in range(band_tiles)], axis=1)
        attend([(kl_ref, rows, v_t, ab_ref), ctx_block])

    @pl.when(t >= n_lat_tiles)
    def _():
        attend([ctx_block])


def _na_call(operands, out_rows, *, batch, seq, ctx_len, with_ctx_queries, bounded):
    q, k, v, ab = operands
    tq = Q_TILE
    n_lat = seq // tq
    nqt = n_lat + (1 if with_ctx_queries else 0)
    ctx_row0 = batch * seq // ctx_len
    n_keys = NA_KEY_ROWS * GRID_W
    width = BRANCH_WIDTH

    def qrow(b, t):
        return jnp.where(t < n_lat, b * n_lat + t, batch * n_lat + b)

    def pattern(t):
        return jnp.where(t == 0, 0, jnp.where(t >= n_lat - 1, 2, 1))

    return pl.pallas_call(
        functools.partial(_na_kernel, n_lat_tiles=n_lat, bounded=bounded),
        grid=(batch, 1, nqt),
        in_specs=[
            pl.BlockSpec((tq, width), lambda b, blk, t: (qrow(b, t), 0)),
            pl.BlockSpec((seq, width), lambda b, blk, t: (b, 0)),
            pl.BlockSpec((ctx_len, width), lambda b, blk, t: (ctx_row0 + b, 0)),
            pl.BlockSpec((seq, width), lambda b, blk, t: (b, 0)),
            pl.BlockSpec((ctx_len, width), lambda b, blk, t: (ctx_row0 + b, 0)),
            pl.BlockSpec((None, BRANCH_HEADS, n_keys, tq), lambda b, blk, t: (pattern(t), 0, 0, 0)),
            pl.BlockSpec((tq, tq), lambda b, blk, t: (0, 0)),
        ],
        out_specs=pl.BlockSpec((tq, width), lambda b, blk, t: (qrow(b, t), 0)),
        out_shape=jax.ShapeDtypeStruct((out_rows, width), BF16),
        scratch_shapes=[pltpu.VMEM((n_lat, width, tq), BF16), pltpu.VMEM((1, width, ctx_len), BF16)],
        compiler_params=_cparams(("arbitrary", "arbitrary", "arbitrary")),
        name="neighbourhood_attention",
    )(q, k, k, v, v, ab, jnp.eye(tq, dtype=BF16))


def _merge_kernel(x_ref, mod_ref, n_ref, o0_ref, o1_ref, o2_ref, o3_ref, wg_ref, bg_ref, wb_ref, wo_ref,
                  out_ref, y_scr):
    n = n_ref[...]
    branch = [o_ref[...] for o_ref in (o0_ref, o1_ref, o2_ref, o3_ref)]
    for c0 in range(0, n.shape[1], MERGE_SLAB):
        cols = slice(c0, c0 + MERGE_SLAB)
        y = None
        for j in range(N_BRANCH):
            gate = jax.nn.sigmoid(_dot(n, wg_ref[j, :, cols]) + bg_ref[j, :, cols])
            term = gate * _dot(branch[j], wb_ref[j, :, cols])
            y = term if y is None else y + term
        y_scr[:, cols] = y.astype(BF16)
    out_ref[...] = x_ref[...] + mod_ref[5:6, :] * _dot(y_scr[...], wo_ref[...])


def _merge_call(x, mod, n, outs, w_gate, b_gate, w_branch, w_out, layer, n_tiles, mod_idx):
    d = x.shape[1]
    tm = TOKEN_TILE
    row = lambda i: (i, 0)
    return pl.pallas_call(
        _merge_kernel,
        grid=(n_tiles,),
        in_specs=[
            pl.BlockSpec((tm, d), row),
            pl.BlockSpec((None, None, N_MOD, d), lambda i: (layer, mod_idx(i), 0, 0)),
            pl.BlockSpec((tm, d), row),
        ] + [pl.BlockSpec((tm, BRANCH_WIDTH), row)] * 4 + [
            _resident((None, N_BRANCH, d, d), lambda i: (layer, 0, 0, 0)),
            _resident((None, N_BRANCH, 1, d), lambda i: (layer, 0, 0, 0)),
            _resident((None, N_BRANCH, BRANCH_WIDTH, d), lambda i: (layer, 0, 0, 0)),
            _resident((None, d, d), lambda i: (layer, 0, 0)),
        ],
        out_specs=pl.BlockSpec((tm, d), row),
        out_shape=jax.ShapeDtypeStruct((n_tiles * tm, d), F32),
        scratch_shapes=[pltpu.VMEM((tm, d), BF16)],
        compiler_params=_cparams(("arbitrary",)),
        name="branch_merge",
    )(x, mod, n, *outs, w_gate, b_gate, w_branch, w_out)


def _rope_tables(seq, rot_dim, pad_rows):
    t = jnp.arange(seq, dtype=jnp.int32)
    row = (t // GRID_W).astype(F32)
    col = (t % GRID_W).astype(F32)
    n_freq = rot_dim // 4
    inv = ROPE_THETA ** (-jnp.arange(n_freq, dtype=F32) / n_freq)
    ar = row[:, None] * inv[None, :]
    ac = col[:, None] * inv[None, :]
    ang = jnp.concatenate([ar, ar, ac, ac], axis=-1)
    sign = jnp.tile(jnp.concatenate([-jnp.ones((n_freq,), F32), jnp.ones((n_freq,), F32)]), 2)
    cos = jnp.concatenate([jnp.cos(ang), jnp.ones((pad_rows, rot_dim), F32)], axis=0)
    sin = jnp.concatenate([jnp.sin(ang) * sign[None, :], jnp.zeros((pad_rows, rot_dim), F32)], axis=0)
    return cos, sin


def _all_rope_tables(seq, pad_rows):
    c64, s64 = _rope_tables(seq, HEAD_DIM, pad_rows)
    c32, s32 = _rope_tables(seq, DIFF_QK_DIM, pad_rows)
    rows = seq + pad_rows
    ones = lambda w: jnp.ones((rows, w), F32)
    zeros = lambda w: jnp.zeros((rows, w), F32)
    pad = LANES - MLA_QK_DIM
    cm = jnp.concatenate([ones(MLA_NOPE_DIM), c32, ones(pad)], axis=-1)
    sm = jnp.concatenate([zeros(MLA_NOPE_DIM), s32, zeros(pad)], axis=-1)
    return (jnp.tile(c64, (1, 2)), jnp.tile(s64, (1, 2)), jnp.tile(c32, (1, 4)), jnp.tile(s32, (1, 4)), cm, sm)


def _block_diag_ones(width, seg):
    idx = np.arange(width) // seg
    return jnp.asarray(idx[:, None] == idx[None, :], dtype=BF16)


def _na_bias_tables(rpb, grid_rows):
    rows_per_tile = Q_TILE // GRID_W
    wh = min(NA_WIN_H, grid_rows)
    kinds = ((0, 0), (rows_per_tile, 0), (grid_rows - rows_per_tile, grid_rows - NA_KEY_ROWS))
    col = np.arange(GRID_W)
    c0 = np.clip(col - NA_WIN_W // 2, 0, GRID_W - NA_WIN_W)
    col_ok = (col[:, None] >= c0[None, :]) & (col[:, None] < c0[None, :] + NA_WIN_W)
    d_col = np.clip(col[:, None] - col[None, :], -(NA_WIN_W - 1), NA_WIN_W - 1) + (NA_WIN_W - 1)
    col_hot = (d_col[..., None] == np.arange(2 * NA_WIN_W - 1)).astype(np.float32)
    blocks = jnp.einsum("hab,dcb->hadc", rpb.astype(F32), jnp.asarray(col_hot), precision=lax.Precision.HIGHEST)
    blocks = jnp.where(jnp.asarray(col_ok)[None, None], blocks * LOG2E, NEG_BIG)
    masked = jnp.full(blocks.shape[:1] + (GRID_W, GRID_W), NEG_BIG, F32)
    tables = []
    for q_row0, k_row0 in kinds:
        r = q_row0 + np.arange(rows_per_tile)
        kr = k_row0 + np.arange(NA_KEY_ROWS)
        r0 = np.clip(r - wh // 2, 0, grid_rows - wh)
        key_rows = []
        for kk in range(NA_KEY_ROWS):
            in_window = (kr[kk] >= r0) & (kr[kk] < r0 + wh)
            key_rows.append(jnp.concatenate(
                [blocks[:, kr[kk] - r[qr] + (NA_WIN_H - 1)] if in_window[qr] else masked
                 for qr in range(rows_per_tile)], axis=-1))
        tables.append(jnp.concatenate(key_rows, axis=-2))
    return jnp.stack(tables, axis=0)


def _pad_heads(w, head_w, to_w):
    lead = w.shape[:-1]
    w = w.reshape(lead + (BRANCH_HEADS, head_w))
    w = jnp.pad(w, [(0, 0)] * len(lead) + [(0, 0), (0, to_w - head_w)])
    return w.reshape(lead + (BRANCH_HEADS * to_w,))


def _gqa_head_order(a, axis):
    parts = jnp.split(a, BRANCH_HEADS, axis=axis)
    return jnp.concatenate([parts[0], parts[2], parts[1], parts[3]], axis=axis)


def kernel(x, c, ctx, c_ctx, w_ada, b_ada, norm_g, ffn_w1, ffn_w3, ffn_w2, w_in, na_g_q, na_g_k, na_rpb, gqa_g_q, gqa_g_k, diff_g_q, diff_g_k, diff_lambda, diff_g_sub, mla_g_cq, mla_g_ckv, mla_w_uq, mla_w_ukv, mla_g_q, mla_g_k, w_gate, b_gate, w_branch, w_out):
    batch, seq, d = x.shape
    ctx_len = ctx.shape[1]
    depth = w_ada.shape[0]
    tm = TOKEN_TILE
    assert d == D_MODEL and ctx_len == Q_TILE and seq % tm == 0 and (batch * ctx_len) % tm == 0
    assert seq % min(KEY_CHUNK, seq) == 0 and seq // GRID_W >= NA_KEY_ROWS and batch < 16
    n_lat_rows = batch * seq
    lat_tiles = n_lat_rows // tm
    all_tiles = lat_tiles + batch * ctx_len // tm
    tiles_per_batch = seq // tm

    def mod_idx(i):
        return jnp.where(i < lat_tiles, i // tiles_per_batch, batch)

    def tab_idx(i):
        return jnp.where(i < lat_tiles, i % tiles_per_batch, tiles_per_batch)

    c16 = jnp.concatenate([c, c_ctx[None, :], jnp.zeros((16 - batch - 1, d), F32)], axis=0)
    mods = _mod_call(c16, w_ada, b_ada).reshape(depth, 16, N_MOD, d)
    w1, w3, w2 = ffn_w1.astype(BF16), ffn_w3.astype(BF16), ffn_w2.astype(BF16)
    na_w, gq_w, df_w, ml_w = jnp.split(w_in, [768, 1280, 2048], axis=-1)
    w_in_p = jnp.concatenate([
        na_w, _gqa_head_order(gq_w[..., :256], -1), gq_w[..., 256:], df_w, ml_w,
        jnp.zeros((depth, d, IN_COLS_PAD - w_in.shape[-1]), F32)], axis=-1).astype(BF16)
    wuq = _pad_heads(mla_w_uq, MLA_QK_DIM, LANES).astype(BF16)
    wukv = mla_w_ukv.reshape(depth, MLA_KV_RANK, BRANCH_HEADS, MLA_NOPE_DIM + HEAD_DIM)
    wuk = _pad_heads(wukv[..., :MLA_NOPE_DIM].reshape(depth, MLA_KV_RANK, -1), MLA_NOPE_DIM, LANES).astype(BF16)
    wuv = wukv[..., MLA_NOPE_DIM:].reshape(depth, MLA_KV_RANK, BRANCH_WIDTH).astype(BF16)

    def vec_row(v, reps=1, scale=1.0):
        v = jnp.tile(v, (1, reps)) * scale
        return jnp.pad(v, ((0, 0), (0, 512 - v.shape[-1])))

    pad_g = lambda g: jnp.pad(g, ((0, 0), (0, LANES - MLA_QK_DIM)))
    vec_rows = [
        vec_row(na_g_q, 4, HEAD_DIM ** -0.5 * LOG2E), vec_row(na_g_k, 4),
        vec_row(gqa_g_q, 4, HEAD_DIM ** -0.5 * LOG2E), vec_row(gqa_g_k, 2),
        vec_row(diff_g_q, 8, DIFF_QK_DIM ** -0.5 * LOG2E), vec_row(diff_g_k, 8),
        vec_row(mla_g_cq), vec_row(mla_g_ckv),
        vec_row(pad_g(mla_g_q), 4, MLA_QK_DIM ** -0.5 * LOG2E), vec_row(pad_g(mla_g_k), 4),
    ]
    vecs = jnp.stack(vec_rows + [jnp.zeros_like(vec_rows[0])] * (16 - len(vec_rows)), axis=1)
    bds = (_block_diag_ones(256, HEAD_DIM), _block_diag_ones(256, DIFF_QK_DIM), _block_diag_ones(256, LANES))
    tables = _all_rope_tables(seq, tm)
    na_ab = [_na_bias_tables(na_rpb[i], seq // GRID_W) for i in range(depth)]
    gsub = jnp.broadcast_to(diff_g_sub[:, :, None], (depth, HEAD_DIM, Q_TILE))
    wg, wo = w_gate.astype(BF16), w_out.astype(BF16)
    wb = jnp.concatenate([w_branch[:, 0:1], _gqa_head_order(w_branch[:, 1:2], 2), w_branch[:, 2:]], axis=1).astype(BF16)
    bg = b_gate.reshape(depth, N_BRANCH, 1, d)

    xa, x_tail = x.reshape(n_lat_rows, d), ctx.reshape(batch * ctx_len, d)

    gqa_subs = ((0, 0, 64, 0, 0), (0, 64, 128, 0, 1), (1, 0, 64, 0, 0), (1, 64, 128, 0, 1))
    diff_subs = tuple((h // 2, 32 * (2 * (h % 2) + j), 32 * (2 * (h % 2) + j + 1), h // 2, h)
                      for h in range(BRANCH_HEADS) for j in range(2))
    mla_subs = tuple((h, 0, LANES, h, h) for h in range(BRANCH_HEADS))

    for i in range(depth):
        need_ctx = i < depth - 1
        mod = mods
        g = norm_g[i].reshape(3, 1, d)
        xa = _ffn_call(xa, mod, g[0], w1, w3, w2, i, 0, 0, all_tiles, mod_idx, x_tail=x_tail)
        x_tail = None
        (n, naq, nak, nav, gqr, gqn, gk, gv, dqr, dqn, dk, dv, mqr, mqn, mk, mv) = _proj_call(
            xa, mod, g[1], w_in_p, vecs, bds, wuq, wuk, wuv, tables, i, all_tiles, mod_idx, tab_idx)
        out_rows = xa.shape[0] if need_ctx else n_lat_rows
        common = dict(batch=batch, seq=seq, ctx_len=ctx_len, with_ctx_queries=need_ctx)
        lam_init = 0.8 - 0.6 * math.exp(-0.3 * i)
        def dispatch(call, bound, *operands, **kw):
            return lax.cond(bound <= SCORE_BOUND,
                            lambda *a: call(a, out_rows, bounded=True, **kw, **common),
                            lambda *a: call(a, out_rows, bounded=False, **kw, **common), *operands)

        joint = functools.partial(dispatch, _joint_call)
        na_bound = _score_bound(na_g_q[i], na_g_k[i], HEAD_DIM) + LOG2E * jnp.max(jnp.abs(na_rpb[i]))

        outs = [
            dispatch(_na_call, na_bound, naq, nak, nav, na_ab[i]),
            joint(_score_bound(gqa_g_q[i], gqa_g_k[i], HEAD_DIM), gqr, gqn, gk, gv,
                  subs=gqa_subs, n_blocks=1, name="gqa_attention", lookahead=len(gqa_subs)),
            joint(_score_bound(diff_g_q[i], diff_g_k[i], DIFF_QK_DIM), dqr, dqn, dk, dv, diff_lambda[i], gsub[i],
                  subs=diff_subs, n_blocks=1, name="diff_attention", diff_lam_init=lam_init),
            joint(_score_bound(mla_g_q[i], mla_g_k[i], MLA_QK_DIM), mqr, mqn, mk, mv,
                  subs=mla_subs, n_blocks=1, name="mla_attention"),
        ]
        n_tiles = all_tiles if need_ctx else lat_tiles
        xa = _merge_call(xa, mod, n, outs, wg, bg, wb, wo, i, n_tiles, mod_idx)
        xa = _ffn_call(xa, mod, g[2], w1, w3, w2, i, 1, 6, n_tiles, mod_idx)
    return xa.reshape(batch, seq, d)
```

```python
import functools
import math

import jax
import jax.numpy as jnp
import numpy as np
from jax import lax
from jax.experimental import pallas as pl
from jax.experimental.pallas import tpu as pltpu

F32 = jnp.float32
BF16 = jnp.bfloat16

D_MODEL = 1024
GRID_W = 64
ROPE_THETA = 10000.0
RMS_EPS = 1e-6
HEAD_DIM = 64
N_BRANCH = 4
BRANCH_HEADS = 4
BRANCH_WIDTH = 256
NA_WIN_H = 8
NA_WIN_W = 16
DIFF_QK_DIM = 32
MLA_Q_RANK = 256
MLA_KV_RANK = 128
MLA_NOPE_DIM = 64
MLA_ROPE_DIM = 32
MLA_QK_DIM = 96
D_FF = 2816
N_MOD = 9
LOG2E = 1.4426950408889634

LANES = 128
TOKEN_TILE = 512
Q_TILE = 256
KEY_CHUNK = 4096
FF_SLAB = 256
MERGE_SLAB = 256
PROJ_ROWS = 256
NA_KEY_ROWS = 12
SCORE_LOOKAHEAD = 2
NEG_BIG = -1e30
SCORE_BOUND = 40.0
VMEM_LIMIT = 56 * 1024 * 1024

C_NA_Q, C_NA_K, C_NA_V = 0, 256, 512
C_GQ_Q, C_GQ_K, C_GQ_V = 768, 1024, 1152
C_DF_Q, C_DF_K, C_DF_V = 1280, 1536, 1792
C_ML_CQ, C_ML_CKV, C_ML_ROPE = 2048, 2304, 2432
IN_COLS_PAD = 2560


def _cparams(sem):
    return pltpu.CompilerParams(dimension_semantics=sem, vmem_limit_bytes=VMEM_LIMIT)


def _dot(a, b):
    return jnp.dot(a, b, preferred_element_type=F32)


def _dot_nt(a, b):
    return lax.dot_general(a, b, (((1,), (1,)), ((), ())), preferred_element_type=F32)


def _silu(a):
    return a * jax.nn.sigmoid(a)


def _modulated_norm(x, g, shift, scale):
    ms = jnp.mean(x * x, axis=-1, keepdims=True)
    return (x * lax.rsqrt(ms + RMS_EPS) * g) * (1.0 + scale) + shift


def _mod_kernel(c_ref, w_ref, b_ref, o_ref):
    c = c_ref[...]
    o_ref[...] = _dot(_silu(c).astype(BF16), w_ref[...].astype(BF16)) + b_ref[...]


def _mod_call(c16, w_ada, b_ada):
    depth, d, nd = w_ada.shape
    tn = 1024
    return pl.pallas_call(
        _mod_kernel,
        grid=(depth, nd // tn),
        in_specs=[
            pl.BlockSpec((16, d), lambda l, j: (0, 0)),
            pl.BlockSpec((None, d, tn), lambda l, j: (l, 0, j)),
            pl.BlockSpec((None, 1, tn), lambda l, j: (l, 0, j)),
        ],
        out_specs=pl.BlockSpec((None, 16, tn), lambda l, j: (l, 0, j)),
        out_shape=jax.ShapeDtypeStruct((depth, 16, nd), F32),
        compiler_params=_cparams(("arbitrary", "arbitrary")),
        name="adaln_mod",
    )(c16, w_ada, b_ada.reshape(depth, 1, nd))


def _ffn_kernel(*refs, row0, head_tiles):
    if head_tiles is None:
        x_ref, mod_ref, g_ref, w1_ref, w3_ref, w2_ref, o_ref, act_scr = refs
        x = x_ref[...]
    else:
        x_ref, tail_ref, mod_ref, g_ref, w1_ref, w3_ref, w2_ref, o_ref, act_scr = refs
        x = jnp.where(pl.program_id(0) < head_tiles, x_ref[...], tail_ref[...])
    h = _modulated_norm(x, g_ref[...], mod_ref[row0:row0 + 1, :], mod_ref[row0 + 1:row0 + 2, :]).astype(BF16)
    for f0 in range(0, D_FF, FF_SLAB):
        a = _dot(h, w1_ref[:, f0:f0 + FF_SLAB])
        b = _dot(h, w3_ref[:, f0:f0 + FF_SLAB])
        act_scr[:, f0:f0 + FF_SLAB] = (_silu(a) * b).astype(BF16)
    o_ref[...] = x + (0.5 * mod_ref[row0 + 2:row0 + 3, :]) * _dot(act_scr[...], w2_ref[...])


def _resident(block_shape, index_map):
    return pl.BlockSpec(block_shape, index_map, pipeline_mode=pl.Buffered(1))


def _ffn_call(x, mod, g, w1, w3, w2, layer, half, row0, n_tiles, mod_idx, x_tail=None):
    d = x.shape[1]
    tm = TOKEN_TILE
    if x_tail is None:
        head_tiles, x_specs, xs = None, [pl.BlockSpec((tm, d), lambda i: (i, 0))], [x]
    else:
        head_tiles = x.shape[0] // tm
        x_specs = [pl.BlockSpec((tm, d), lambda i: (jnp.minimum(i, head_tiles - 1), 0)),
                   pl.BlockSpec((tm, d), lambda i: (jnp.maximum(i - head_tiles, 0), 0))]
        xs = [x, x_tail]
    return pl.pallas_call(
        functools.partial(_ffn_kernel, row0=row0, head_tiles=head_tiles),
        grid=(n_tiles,),
        in_specs=x_specs + [
            pl.BlockSpec((None, None, N_MOD, d), lambda i: (layer, mod_idx(i), 0, 0)),
            _resident((1, d), lambda i: (0, 0)),
            _resident((None, None, d, D_FF), lambda i: (layer, half, 0, 0)),
            _resident((None, None, d, D_FF), lambda i: (layer, half, 0, 0)),
            _resident((None, None, D_FF, d), lambda i: (layer, half, 0, 0)),
        ],
        out_specs=pl.BlockSpec((tm, d), lambda i: (i, 0)),
        out_shape=jax.ShapeDtypeStruct((n_tiles * tm, d), F32),
        scratch_shapes=[pltpu.VMEM((tm, D_FF), BF16)],
        compiler_params=_cparams(("arbitrary",)),
        name="ffn_half_step",
    )(*xs, mod, g, w1, w3, w2)


def _segnorm(x, bd, seg_len, gain):
    x2 = x * x
    hi = x2.astype(BF16)
    lo = (x2 - hi.astype(F32)).astype(BF16)
    groups = []
    for c0 in range(0, x.shape[1], bd.shape[0]):
        w = min(bd.shape[0], x.shape[1] - c0)
        groups.append(_dot(hi[:, c0:c0 + w], bd[0:w, 0:w]) + _dot(lo[:, c0:c0 + w], bd[0:w, 0:w]))
    ss = groups[0] if len(groups) == 1 else jnp.concatenate(groups, axis=-1)
    return x * lax.rsqrt(ss * (1.0 / seg_len) + RMS_EPS) * gain


def _tile_lanes(t, width):
    reps = width // t.shape[-1]
    return t if reps == 1 else jnp.concatenate([t] * reps, axis=-1)


def _rope(x, cos, sin_signed, quarter):
    width = x.shape[-1]
    lane = lax.broadcasted_iota(jnp.int32, x.shape, 1)
    first = (lane % (2 * quarter)) < quarter
    rot = jnp.where(first, pltpu.roll(x, width - quarter, 1), pltpu.roll(x, quarter, 1))
    return x * _tile_lanes(cos, width) + rot * _tile_lanes(sin_signed, width)


def _proj_kernel(x_ref, mod_ref, g_ref, w_ref, vec_ref, bd64_ref, bd32_ref, bd128_ref,
                 wuq_ref, wuk_ref, wuv_ref, c64_ref, s64_ref, c32_ref, s32_ref, cm_ref, sm_ref,
                 n_ref, naq_ref, nak_ref, nav_ref,
                 gqr_ref, gqn_ref, gk_ref, gv_ref,
                 dqr_ref, dqn_ref, dk_ref, dv_ref,
                 mqr_ref, mqn_ref, mk_ref, mv_ref):
    bd64, bd32, bd128 = bd64_ref[...], bd32_ref[...], bd128_ref[...]

    def vec(row, width):
        return vec_ref[row:row + 1, 0:width]

    groups = [slice(r, r + PROJ_ROWS) for r in range(0, x_ref.shape[0], PROJ_ROWS)]
    projected = []
    for rows in groups:
        n = _modulated_norm(x_ref[rows, :], g_ref[...], mod_ref[3:4, :], mod_ref[4:5, :]).astype(BF16)
        n_ref[rows, :] = n
        projected.append(_dot(n, w_ref[...]))

    for rows, p in zip(groups, projected):
        c64, s64 = c64_ref[rows, :], s64_ref[rows, :]
        c32, s32 = c32_ref[rows, :], s32_ref[rows, :]
        cm, sm = cm_ref[rows, :], sm_ref[rows, :]

        naq_ref[rows, :] = _segnorm(p[:, C_NA_Q:C_NA_Q + 256], bd64, HEAD_DIM, vec(0, 256)).astype(BF16)
        nak_ref[rows, :] = _segnorm(p[:, C_NA_K:C_NA_K + 256], bd64, HEAD_DIM, vec(1, 256)).astype(BF16)
        nav_ref[rows, :] = p[:, C_NA_V:C_NA_V + 256].astype(BF16)

        q = _segnorm(p[:, C_GQ_Q:C_GQ_Q + 256], bd64, HEAD_DIM, vec(2, 256))
        gqn_ref[rows, :] = q.astype(BF16)
        gqr_ref[rows, :] = _rope(q, c64, s64, HEAD_DIM // 4).astype(BF16)
        k = _segnorm(p[:, C_GQ_K:C_GQ_K + 128], bd64, HEAD_DIM, vec(3, 128))
        gk_ref[rows, :] = _rope(k, c64, s64, HEAD_DIM // 4).astype(BF16)
        gv_ref[rows, :] = p[:, C_GQ_V:C_GQ_V + 128].astype(BF16)

        q = _segnorm(p[:, C_DF_Q:C_DF_Q + 256], bd32, DIFF_QK_DIM, vec(4, 256))
        dqn_ref[rows, :] = q.astype(BF16)
        dqr_ref[rows, :] = _rope(q, c32, s32, DIFF_QK_DIM // 4).astype(BF16)
        k = _segnorm(p[:, C_DF_K:C_DF_K + 256], bd32, DIFF_QK_DIM, vec(5, 256))
        dk_ref[rows, :] = _rope(k, c32, s32, DIFF_QK_DIM // 4).astype(BF16)
        dv_ref[rows, :] = p[:, C_DF_V:C_DF_V + 256].astype(BF16)

        cq = p[:, C_ML_CQ:C_ML_CQ + MLA_Q_RANK]
        cq = cq * lax.rsqrt(jnp.mean(cq * cq, axis=-1, keepdims=True) + RMS_EPS) * vec(6, 256)
        q = _segnorm(_dot(cq.astype(BF16), wuq_ref[...]), bd128, MLA_QK_DIM, vec(8, 512))
        mqn_ref[rows, :] = q.astype(BF16)
        mqr_ref[rows, :] = _rope(q, cm, sm, MLA_ROPE_DIM // 4).astype(BF16)
        ckv = p[:, C_ML_CKV:C_ML_CKV + MLA_KV_RANK]
        ckv = (ckv * lax.rsqrt(jnp.mean(ckv * ckv, axis=-1, keepdims=True) + RMS_EPS) * vec(7, 128)).astype(BF16)
        k_rope = pltpu.roll(p[:, C_ML_ROPE:C_ML_ROPE + 128], MLA_NOPE_DIM, 1)
        k = _dot(ckv, wuk_ref[...]) + _tile_lanes(k_rope, 512)
        k = _segnorm(k, bd128, MLA_QK_DIM, vec(9, 512))
        mk_ref[rows, :] = _rope(k, cm, sm, MLA_ROPE_DIM // 4).astype(BF16)
        mv_ref[rows, :] = _dot(ckv, wuv_ref[...]).astype(BF16)


_PROJ_OUT_WIDTHS = (1024, 256, 256, 256, 256, 256, 128, 128, 256, 256, 256, 256, 512, 512, 512, 256)


def _proj_call(x, mod, g, w_in, vecs, bds, wuq, wuk, wuv, tables, layer, n_tiles, mod_idx, tab_idx):
    d = x.shape[1]
    tm = TOKEN_TILE
    const2 = lambda i: (0, 0)
    in_specs = [
        pl.BlockSpec((tm, d), lambda i: (i, 0)),
        pl.BlockSpec((None, None, N_MOD, d), lambda i: (layer, mod_idx(i), 0, 0)),
        pl.BlockSpec((1, d), const2),
        pl.BlockSpec((None, d, IN_COLS_PAD), lambda i: (layer, 0, 0)),
        pl.BlockSpec((None, 16, 512), lambda i: (layer, 0, 0)),
        pl.BlockSpec((256, 256), const2),
        pl.BlockSpec((256, 256), const2),
        pl.BlockSpec((256, 256), const2),
        pl.BlockSpec((None, 256, 512), lambda i: (layer, 0, 0)),
        pl.BlockSpec((None, 128, 512), lambda i: (layer, 0, 0)),
        pl.BlockSpec((None, 128, 256), lambda i: (layer, 0, 0)),
    ] + [pl.BlockSpec((tm, LANES), lambda i: (tab_idx(i), 0))] * 6
    rows = n_tiles * tm
    return pl.pallas_call(
        _proj_kernel,
        grid=(n_tiles,),
        in_specs=in_specs,
        out_specs=[pl.BlockSpec((tm, w), lambda i: (i, 0)) for w in _PROJ_OUT_WIDTHS],
        out_shape=[jax.ShapeDtypeStruct((rows, w), BF16) for w in _PROJ_OUT_WIDTHS],
        compiler_params=_cparams(("arbitrary",)),
        name="mixer_projection",
    )(x, mod, g, w_in, vecs, *bds, wuq, wuk, wuv, *tables)


def _attend(softmaxes, bounded, lookahead):
    def scores_of(i):
        out = []
        for k_ref, rows, ks, _, _, q_t, bias in softmaxes[i]:
            s = _dot(k_ref[rows, ks * LANES:(ks + 1) * LANES], q_t)
            out.append(s if bias is None else s + bias())
        return out

    n = len(softmaxes)
    scores = {i: scores_of(i) for i in range(min(lookahead, n))}
    outs = []
    for i in range(n):
        if i + lookahead < n:
            scores[i + lookahead] = scores_of(i + lookahead)
        mine = scores.pop(i)
        if not bounded:
            m = functools.reduce(jnp.maximum, [jnp.max(s, axis=0, keepdims=True) for s in mine])
        l = acc = None
        for s, (_, _, _, v_t, vh, _, _) in zip(mine, softmaxes[i]):
            p = jnp.exp2(s if bounded else s - m)
            l_blk = jnp.sum(p, axis=0, keepdims=True)
            acc_blk = _dot(_head_rows(v_t, vh), p.astype(BF16))
            l, acc = (l_blk, acc_blk) if l is None else (l + l_blk, acc + acc_blk)
        outs.append(acc / l)
    return outs


def _score_bound(g_q, g_k, dim):
    return 1.01 * dim * (dim ** -0.5 * LOG2E) * jnp.max(jnp.abs(g_q)) * jnp.max(jnp.abs(g_k))


def _transpose_bf16(eye, a):
    return _dot_nt(eye[0:a.shape[1], 0:a.shape[1]], a).astype(BF16)


def _q_slabs_t(eye, q_ref):
    out = []
    for j0 in range(0, q_ref.shape[1], 2 * LANES):
        w = min(2 * LANES, q_ref.shape[1] - j0)
        q_t = _dot_nt(eye[0:w, 0:w], q_ref[:, j0:j0 + w]).astype(BF16)
        out += [q_t[r:r + LANES, :] for r in range(0, w, LANES)]
    return out


def _keep_rows(q_t, lo, hi):
    if lo == 0 and hi == q_t.shape[0]:
        return q_t
    row = lax.broadcasted_iota(jnp.int32, q_t.shape, 0)
    return jnp.where((row >= lo) & (row < hi), q_t, jnp.zeros_like(q_t))


def _head_rows(v_t, head):
    return v_t[head * HEAD_DIM:(head + 1) * HEAD_DIM, :]


def _store_heads(eye, o_ref, o_t_heads):
    for p in range(0, len(o_t_heads), 4):
        o_t = jnp.concatenate(o_t_heads[p:p + 4], axis=0).astype(BF16)
        o_ref[:, p * HEAD_DIM:p * HEAD_DIM + o_t.shape[0]] = _dot_nt(eye, o_t).astype(o_ref.dtype)


def _stage_values_t(eye, v_ref, vt_scr, n_chunks, chunk):
    for c in range(n_chunks):
        for j in range(v_ref.shape[1] // LANES):
            vt_scr[c, j * LANES:(j + 1) * LANES, :] = _transpose_bf16(
                eye, v_ref[c * chunk:(c + 1) * chunk, j * LANES:(j + 1) * LANES])


def _joint_kernel(*refs, groups, n_lat_tiles, seq, chunk, with_ctx_queries, diff_lam_init, bounded, lookahead):
    n_groups = len(groups)
    ins = [refs[6 * g:6 * g + 6] for g in range(n_groups)]
    pos = 6 * n_groups
    eye_ref = refs[pos]
    pos += 1
    if diff_lam_init is not None:
        lam_ref, gsub_ref = refs[pos:pos + 2]
        pos += 2
    o_refs = refs[pos:pos + n_groups]
    scratch = [refs[pos + n_groups + 2 * g:pos + n_groups + 2 * g + 2] for g in range(n_groups)]
    t = pl.program_id(1)
    n_chunks = seq // chunk
    eye = eye_ref[...]

    @pl.when(t == 0)
    def _():
        for (_, _, _, _, vl_ref, vc_ref), (vt_scr, vct_scr) in zip(ins, scratch):
            _stage_values_t(eye, vl_ref, vt_scr, n_chunks, chunk)
            _stage_values_t(eye, vc_ref, vct_scr, 1, vc_ref.shape[0])

    def operands(q_ref, subs):
        slabs = _q_slabs_t(eye, q_ref)
        return [_keep_rows(slabs[qs], lo, hi) for (qs, lo, hi, _, _) in subs]

    def softmaxes(latent):
        out = []
        for subs, (qr_ref, qn_ref, kl_ref, kc_ref, _, _), (vt_scr, vct_scr) in zip(groups, ins, scratch):
            qn_t = operands(qn_ref, subs)
            qr_t = operands(qr_ref, subs) if latent else None
            for i, (_, _, _, ks, vh) in enumerate(subs):
                blocks = []
                if latent:
                    blocks += [(kl_ref, slice(c * chunk, (c + 1) * chunk), ks, vt_scr[c], vh, qr_t[i], None)
                               for c in range(n_chunks)]
                blocks.append((kc_ref, slice(None), ks, vct_scr[0], vh, qn_t[i], None))
                out.append(blocks)
        return out

    def finish(outs):
        for subs, o_ref in zip(groups, o_refs):
            heads, outs = outs[:len(subs)], outs[len(subs):]
            if diff_lam_init is not None:
                lv = lam_ref[...]
                lam = (jnp.exp(jnp.sum(lv[0:1, :] * lv[1:2, :], axis=-1, keepdims=True))
                       - jnp.exp(jnp.sum(lv[2:3, :] * lv[3:4, :], axis=-1, keepdims=True)) + diff_lam_init)
                pairs, heads = zip(heads[0::2], heads[1::2]), []
                for o1, o2 in pairs:
                    d = o1 - lam * o2
                    ms = jnp.mean(d * d, axis=0, keepdims=True)
                    heads.append((d * lax.rsqrt(ms + RMS_EPS) * gsub_ref[...]) * (1.0 - diff_lam_init))
            _store_heads(eye, o_ref, heads)

    @pl.when(t < n_lat_tiles)
    def _():
        finish(_attend(softmaxes(True), bounded, lookahead))

    if with_ctx_queries:
        @pl.when(t >= n_lat_tiles)
        def _():
            finish(_attend(softmaxes(False), bounded, lookahead))


def _joint_call(operands, out_rows, *, groups, batch, seq, ctx_len, with_ctx_queries, name, bounded,
                diff_lam_init=None, lookahead=SCORE_LOOKAHEAD):
    n_groups = len(groups)
    tq = Q_TILE
    n_lat = seq // tq
    nqt = n_lat + (1 if with_ctx_queries else 0)
    ctx_row0 = batch * seq // ctx_len
    chunk = min(KEY_CHUNK, seq)
    n_chunks = seq // chunk

    def qrow(b, t):
        return jnp.where(t < n_lat, b * n_lat + t, batch * n_lat + b)

    in_specs, args, scratch = [], [], []
    for g in range(n_groups):
        qr, qn, k, v = operands[4 * g:4 * g + 4]
        in_specs += [
            pl.BlockSpec((tq, qr.shape[1]), lambda b, t: (qrow(b, t), 0)),
            pl.BlockSpec((tq, qn.shape[1]), lambda b, t: (qrow(b, t), 0)),
            pl.BlockSpec((seq, k.shape[1]), lambda b, t: (b, 0)),
            pl.BlockSpec((ctx_len, k.shape[1]), lambda b, t: (ctx_row0 + b, 0)),
            pl.BlockSpec((seq, v.shape[1]), lambda b, t: (b, 0)),
            pl.BlockSpec((ctx_len, v.shape[1]), lambda b, t: (ctx_row0 + b, 0)),
        ]
        args += [qr, qn, k, k, v, v]
        scratch += [pltpu.VMEM((n_chunks, v.shape[1], chunk), BF16), pltpu.VMEM((1, v.shape[1], ctx_len), BF16)]
    in_specs.append(pl.BlockSpec((tq, tq), lambda b, t: (0, 0)))
    args.append(jnp.eye(tq, dtype=BF16))
    if diff_lam_init is not None:
        in_specs += [pl.BlockSpec((4, DIFF_QK_DIM), lambda b, t: (0, 0)),
                     pl.BlockSpec((HEAD_DIM, tq), lambda b, t: (0, 0))]
        args += list(operands[4 * n_groups:])
    return pl.pallas_call(
        functools.partial(_joint_kernel, groups=groups, n_lat_tiles=n_lat, seq=seq, chunk=chunk,
                          with_ctx_queries=with_ctx_queries, diff_lam_init=diff_lam_init, bounded=bounded,
                          lookahead=lookahead),
        grid=(batch, nqt),
        in_specs=in_specs,
        out_specs=[pl.BlockSpec((tq, BRANCH_WIDTH), lambda b, t: (qrow(b, t), 0))] * n_groups,
        out_shape=[jax.ShapeDtypeStruct((out_rows, BRANCH_WIDTH), BF16)] * n_groups,
        scratch_shapes=scratch,
        compiler_params=_cparams(("arbitrary", "arbitrary")),
        name=name,
    )(*args)


def _na_kernel(q_ref, kl_ref, kc_ref, vl_ref, vc_ref, ab_ref, eye_ref, o_ref, vt_scr, vct_scr, *, n_lat_tiles,
               bounded):
    t = pl.program_id(2)
    tq = o_ref.shape[0]
    eye = eye_ref[...]
    rows_per_tile = tq // GRID_W
    band_tiles = NA_KEY_ROWS // rows_per_tile
    heads = range(BRANCH_HEADS)

    @pl.when(t == 0)
    def _():
        _stage_values_t(eye, vl_ref, vt_scr, n_lat_tiles, tq)
        _stage_values_t(eye, vc_ref, vct_scr, 1, vc_ref.shape[0])

    subs = tuple((h // 2, (h % 2) * HEAD_DIM, (h % 2 + 1) * HEAD_DIM, h // 2, h) for h in heads)

    def attend(blocks):
        slabs = _q_slabs_t(eye, q_ref)
        softmaxes = []
        for h, (qs, lo, hi, ks, vh) in enumerate(subs):
            q_t = _keep_rows(slabs[qs], lo, hi)
            softmaxes.append([(k_ref, rows, ks, v_t, vh, q_t, None if bias is None else (lambda h=h, bias=bias: bias[h]))
                              for k_ref, rows, v_t, bias in blocks])
        _store_heads(eye, o_ref, _attend(softmaxes, bounded, SCORE_LOOKAHEAD))

    ctx_block = (kc_ref, slice(None), vct_scr[0], None)

    @pl.when(t < n_lat_tiles)
    def _():
        tile0 = jnp.clip(t - NA_WIN_H // 2 // rows_per_tile, 0, n_lat_tiles - band_tiles)
        rows = pl.ds(pl.multiple_of(tile0 * tq, tq), band_tiles * tq)
        v_t = jnp.concatenate([vt_scr[tile0 + j] for j in range(band_tiles)], axis=1)
        attend([(kl_ref, rows, v_t, ab_ref), ctx_block])

    @pl.when(t >= n_lat_tiles)
    def _():
        attend([ctx_block])


def _na_call(operands, out_rows, *, batch, seq, ctx_len, with_ctx_queries, bounded):
    q, k, v, ab = operands
    tq = Q_TILE
    n_lat = seq // tq
    nqt = n_lat + (1 if with_ctx_queries else 0)
    ctx_row0 = batch * seq // ctx_len
    n_keys = NA_KEY_ROWS * GRID_W
    width = BRANCH_WIDTH

    def qrow(b, t):
        return jnp.where(t < n_lat, b * n_lat + t, batch * n_lat + b)

    def pattern(t):
        return jnp.where(t == 0, 0, jnp.where(t >= n_lat - 1, 2, 1))

    return pl.pallas_call(
        functools.partial(_na_kernel, n_lat_tiles=n_lat, bounded=bounded),
        grid=(batch, 1, nqt),
        in_specs=[
            pl.BlockSpec((tq, width), lambda b, blk, t: (qrow(b, t), 0)),
            pl.BlockSpec((seq, width), lambda b, blk, t: (b, 0)),
            pl.BlockSpec((ctx_len, width), lambda b, blk, t: (ctx_row0 + b, 0)),
            pl.BlockSpec((seq, width), lambda b, blk, t: (b, 0)),
            pl.BlockSpec((ctx_len, width), lambda b, blk, t: (ctx_row0 + b, 0)),
            pl.BlockSpec((None, BRANCH_HEADS, n_keys, tq), lambda b, blk, t: (pattern(t), 0, 0, 0)),
            pl.BlockSpec((tq, tq), lambda b, blk, t: (0, 0)),
        ],
        out_specs=pl.BlockSpec((tq, width), lambda b, blk, t: (qrow(b, t), 0)),
        out_shape=jax.ShapeDtypeStruct((out_rows, width), BF16),
        scratch_shapes=[pltpu.VMEM((n_lat, width, tq), BF16), pltpu.VMEM((1, width, ctx_len), BF16)],
        compiler_params=_cparams(("arbitrary", "arbitrary", "arbitrary")),
        name="neighbourhood_attention",
    )(q, k, k, v, v, ab, jnp.eye(tq, dtype=BF16))


def _merge_kernel(x_ref, mod_ref, n_ref, o0_ref, o1_ref, o2_ref, o3_ref, wg_ref, bg_ref, wb_ref, wo_ref,
                  out_ref, y_scr):
    n = n_ref[...]
    branch = [o_ref[...] for o_ref in (o0_ref, o1_ref, o2_ref, o3_ref)]
    for c0 in range(0, n.shape[1], MERGE_SLAB):
        cols = slice(c0, c0 + MERGE_SLAB)
        y = None
        for j in range(N_BRANCH):
            gate = jax.nn.sigmoid(_dot(n, wg_ref[j, :, cols]) + bg_ref[j, :, cols])
            term = gate * _dot(branch[j], wb_ref[j, :, cols])
            y = term if y is None else y + term
        y_scr[:, cols] = y.astype(BF16)
    out_ref[...] = x_ref[...] + mod_ref[5:6, :] * _dot(y_scr[...], wo_ref[...])


def _merge_call(x, mod, n, outs, w_gate, b_gate, w_branch, w_out, layer, n_tiles, mod_idx):
    d = x.shape[1]
    tm = TOKEN_TILE
    row = lambda i: (i, 0)
    return pl.pallas_call(
        _merge_kernel,
        grid=(n_tiles,),
        in_specs=[
            pl.BlockSpec((tm, d), row),
            pl.BlockSpec((None, None, N_MOD, d), lambda i: (layer, mod_idx(i), 0, 0)),
            pl.BlockSpec((tm, d), row),
        ] + [pl.BlockSpec((tm, BRANCH_WIDTH), row)] * 4 + [
            _resident((None, N_BRANCH, d, d), lambda i: (layer, 0, 0, 0)),
            _resident((None, N_BRANCH, 1, d), lambda i: (layer, 0, 0, 0)),
            _resident((None, N_BRANCH, BRANCH_WIDTH, d), lambda i: (layer, 0, 0, 0)),
            _resident((None, d, d), lambda i: (layer, 0, 0)),
        ],
        out_specs=pl.BlockSpec((tm, d), row),
        out_shape=jax.ShapeDtypeStruct((n_tiles * tm, d), F32),
        scratch_shapes=[pltpu.VMEM((tm, d), BF16)],
        compiler_params=_cparams(("arbitrary",)),
        name="branch_merge",
    )(x, mod, n, *outs, w_gate, b_gate, w_branch, w_out)


def _rope_tables(seq, rot_dim, pad_rows):
    t = jnp.arange(seq, dtype=jnp.int32)
    row = (t // GRID_W).astype(F32)
    col = (t % GRID_W).astype(F32)
    n_freq = rot_dim // 4
    inv = ROPE_THETA ** (-jnp.arange(n_freq, dtype=F32) / n_freq)
    ar = row[:, None] * inv[None, :]
    ac = col[:, None] * inv[None, :]
    ang = jnp.concatenate([ar, ar, ac, ac], axis=-1)
    sign = jnp.tile(jnp.concatenate([-jnp.ones((n_freq,), F32), jnp.ones((n_freq,), F32)]), 2)
    cos = jnp.concatenate([jnp.cos(ang), jnp.ones((pad_rows, rot_dim), F32)], axis=0)
    sin = jnp.concatenate([jnp.sin(ang) * sign[None, :], jnp.zeros((pad_rows, rot_dim), F32)], axis=0)
    return cos, sin


def _all_rope_tables(seq, pad_rows):
    c64, s64 = _rope_tables(seq, HEAD_DIM, pad_rows)
    c32, s32 = _rope_tables(seq, DIFF_QK_DIM, pad_rows)
    rows = seq + pad_rows
    ones = lambda w: jnp.ones((rows, w), F32)
    zeros = lambda w: jnp.zeros((rows, w), F32)
    pad = LANES - MLA_QK_DIM
    cm = jnp.concatenate([ones(MLA_NOPE_DIM), c32, ones(pad)], axis=-1)
    sm = jnp.concatenate([zeros(MLA_NOPE_DIM), s32, zeros(pad)], axis=-1)
    return (jnp.tile(c64, (1, 2)), jnp.tile(s64, (1, 2)), jnp.tile(c32, (1, 4)), jnp.tile(s32, (1, 4)), cm, sm)


def _block_diag_ones(width, seg):
    idx = np.arange(width) // seg
    return jnp.asarray(idx[:, None] == idx[None, :], dtype=BF16)


def _na_bias_tables(rpb, grid_rows):
    rows_per_tile = Q_TILE // GRID_W
    wh = min(NA_WIN_H, grid_rows)
    kinds = ((0, 0), (rows_per_tile, 0), (grid_rows - rows_per_tile, grid_rows - NA_KEY_ROWS))
    col = np.arange(GRID_W)
    c0 = np.clip(col - NA_WIN_W // 2, 0, GRID_W - NA_WIN_W)
    col_ok = (col[:, None] >= c0[None, :]) & (col[:, None] < c0[None, :] + NA_WIN_W)
    d_col = np.clip(col[:, None] - col[None, :], -(NA_WIN_W - 1), NA_WIN_W - 1) + (NA_WIN_W - 1)
    col_hot = (d_col[..., None] == np.arange(2 * NA_WIN_W - 1)).astype(np.float32)
    blocks = jnp.einsum("hab,dcb->hadc", rpb.astype(F32), jnp.asarray(col_hot), precision=lax.Precision.HIGHEST)
    blocks = jnp.where(jnp.asarray(col_ok)[None, None], blocks * LOG2E, NEG_BIG)
    masked = jnp.full(blocks.shape[:1] + (GRID_W, GRID_W), NEG_BIG, F32)
    tables = []
    for q_row0, k_row0 in kinds:
        r = q_row0 + np.arange(rows_per_tile)
        kr = k_row0 + np.arange(NA_KEY_ROWS)
        r0 = np.clip(r - wh // 2, 0, grid_rows - wh)
        key_rows = []
        for kk in range(NA_KEY_ROWS):
            in_window = (kr[kk] >= r0) & (kr[kk] < r0 + wh)
            key_rows.append(jnp.concatenate(
                [blocks[:, kr[kk] - r[qr] + (NA_WIN_H - 1)] if in_window[qr] else masked
                 for qr in range(rows_per_tile)], axis=-1))
        tables.append(jnp.concatenate(key_rows, axis=-2))
    return jnp.stack(tables, axis=0)


def _pad_heads(w, head_w, to_w):
    lead = w.shape[:-1]
    w = w.reshape(lead + (BRANCH_HEADS, head_w))
    w = jnp.pad(w, [(0, 0)] * len(lead) + [(0, 0), (0, to_w - head_w)])
    return w.reshape(lead + (BRANCH_HEADS * to_w,))


def _gqa_head_order(a, axis):
    parts = jnp.split(a, BRANCH_HEADS, axis=axis)
    return jnp.concatenate([parts[0], parts[2], parts[1], parts[3]], axis=axis)


def kernel(x, c, ctx, c_ctx, w_ada, b_ada, norm_g, ffn_w1, ffn_w3, ffn_w2, w_in, na_g_q, na_g_k, na_rpb, gqa_g_q, gqa_g_k, diff_g_q, diff_g_k, diff_lambda, diff_g_sub, mla_g_cq, mla_g_ckv, mla_w_uq, mla_w_ukv, mla_g_q, mla_g_k, w_gate, b_gate, w_branch, w_out):
    batch, seq, d = x.shape
    ctx_len = ctx.shape[1]
    depth = w_ada.shape[0]
    tm = TOKEN_TILE
    assert d == D_MODEL and ctx_len == Q_TILE and seq % tm == 0 and (batch * ctx_len) % tm == 0
    assert seq % min(KEY_CHUNK, seq) == 0 and seq // GRID_W >= NA_KEY_ROWS and batch < 16
    n_lat_rows = batch * seq
    lat_tiles = n_lat_rows // tm
    all_tiles = lat_tiles + batch * ctx_len // tm
    tiles_per_batch = seq // tm

    def mod_idx(i):
        return jnp.where(i < lat_tiles, i // tiles_per_batch, batch)

    def tab_idx(i):
        return jnp.where(i < lat_tiles, i % tiles_per_batch, tiles_per_batch)

    c16 = jnp.concatenate([c, c_ctx[None, :], jnp.zeros((16 - batch - 1, d), F32)], axis=0)
    mods = _mod_call(c16, w_ada, b_ada).reshape(depth, 16, N_MOD, d)
    w1, w3, w2 = ffn_w1.astype(BF16), ffn_w3.astype(BF16), ffn_w2.astype(BF16)
    na_w, gq_w, df_w, ml_w = jnp.split(w_in, [768, 1280, 2048], axis=-1)
    w_in_p = jnp.concatenate([
        na_w, _gqa_head_order(gq_w[..., :256], -1), gq_w[..., 256:], df_w, ml_w,
        jnp.zeros((depth, d, IN_COLS_PAD - w_in.shape[-1]), F32)], axis=-1).astype(BF16)
    wuq = _pad_heads(mla_w_uq, MLA_QK_DIM, LANES).astype(BF16)
    wukv = mla_w_ukv.reshape(depth, MLA_KV_RANK, BRANCH_HEADS, MLA_NOPE_DIM + HEAD_DIM)
    wuk = _pad_heads(wukv[..., :MLA_NOPE_DIM].reshape(depth, MLA_KV_RANK, -1), MLA_NOPE_DIM, LANES).astype(BF16)
    wuv = wukv[..., MLA_NOPE_DIM:].reshape(depth, MLA_KV_RANK, BRANCH_WIDTH).astype(BF16)

    def vec_row(v, reps=1, scale=1.0):
        v = jnp.tile(v, (1, reps)) * scale
        return jnp.pad(v, ((0, 0), (0, 512 - v.shape[-1])))

    pad_g = lambda g: jnp.pad(g, ((0, 0), (0, LANES - MLA_QK_DIM)))
    vec_rows = [
        vec_row(na_g_q, 4, HEAD_DIM ** -0.5 * LOG2E), vec_row(na_g_k, 4),
        vec_row(gqa_g_q, 4, HEAD_DIM ** -0.5 * LOG2E), vec_row(gqa_g_k, 2),
        vec_row(diff_g_q, 8, DIFF_QK_DIM ** -0.5 * LOG2E), vec_row(diff_g_k, 8),
        vec_row(mla_g_cq), vec_row(mla_g_ckv),
        vec_row(pad_g(mla_g_q), 4, MLA_QK_DIM ** -0.5 * LOG2E), vec_row(pad_g(mla_g_k), 4),
    ]
    vecs = jnp.stack(vec_rows + [jnp.zeros_like(vec_rows[0])] * (16 - len(vec_rows)), axis=1)
    bds = (_block_diag_ones(256, HEAD_DIM), _block_diag_ones(256, DIFF_QK_DIM), _block_diag_ones(256, LANES))
    tables = _all_rope_tables(seq, tm)
    na_ab = [_na_bias_tables(na_rpb[i], seq // GRID_W) for i in range(depth)]
    gsub = jnp.broadcast_to(diff_g_sub[:, :, None], (depth, HEAD_DIM, Q_TILE))
    wg, wo = w_gate.astype(BF16), w_out.astype(BF16)
    wb = jnp.concatenate([w_branch[:, 0:1], _gqa_head_order(w_branch[:, 1:2], 2), w_branch[:, 2:]], axis=1).astype(BF16)
    bg = b_gate.reshape(depth, N_BRANCH, 1, d)

    xa, x_tail = x.reshape(n_lat_rows, d), ctx.reshape(batch * ctx_len, d)

    gqa_subs = ((0, 0, 64, 0, 0), (0, 64, 128, 0, 1), (1, 0, 64, 0, 0), (1, 64, 128, 0, 1))
    diff_subs = tuple((h // 2, 32 * (2 * (h % 2) + j), 32 * (2 * (h % 2) + j + 1), h // 2, h)
                      for h in range(BRANCH_HEADS) for j in range(2))
    mla_subs = tuple((h, 0, LANES, h, h) for h in range(BRANCH_HEADS))

    for i in range(depth):
        need_ctx = i < depth - 1
        mod = mods
        g = norm_g[i].reshape(3, 1, d)
        xa = _ffn_call(xa, mod, g[0], w1, w3, w2, i, 0, 0, all_tiles, mod_idx, x_tail=x_tail)
        x_tail = None
        (n, naq, nak, nav, gqr, gqn, gk, gv, dqr, dqn, dk, dv, mqr, mqn, mk, mv) = _proj_call(
            xa, mod, g[1], w_in_p, vecs, bds, wuq, wuk, wuv, tables, i, all_tiles, mod_idx, tab_idx)
        out_rows = xa.shape[0] if need_ctx else n_lat_rows
        common = dict(batch=batch, seq=seq, ctx_len=ctx_len, with_ctx_queries=need_ctx)
        lam_init = 0.8 - 0.6 * math.exp(-0.3 * i)
        def dispatch(call, bound, *operands, **kw):
            return lax.cond(bound <= SCORE_BOUND,
                            lambda *a: call(a, out_rows, bounded=True, **kw, **common),
                            lambda *a: call(a, out_rows, bounded=False, **kw, **common), *operands)

        joint = functools.partial(dispatch, _joint_call)
        na_bound = _score_bound(na_g_q[i], na_g_k[i], HEAD_DIM) + LOG2E * jnp.max(jnp.abs(na_rpb[i]))

        gqa_bound = _score_bound(gqa_g_q[i], gqa_g_k[i], HEAD_DIM)
        mla_bound = _score_bound(mla_g_q[i], mla_g_k[i], MLA_QK_DIM)
        o_na = dispatch(_na_call, na_bound, naq, nak, nav, na_ab[i])
        o_gqa, o_mla = joint(jnp.maximum(gqa_bound, mla_bound), gqr, gqn, gk, gv, mqr, mqn, mk, mv,
                             groups=(gqa_subs, mla_subs), name="gqa_mla_attention")
        o_diff, = joint(_score_bound(diff_g_q[i], diff_g_k[i], DIFF_QK_DIM), dqr, dqn, dk, dv, diff_lambda[i], gsub[i],
                        groups=(diff_subs,), name="diff_attention", diff_lam_init=lam_init)
        outs = [o_na, o_gqa, o_diff, o_mla]
        n_tiles = all_tiles if need_ctx else lat_tiles
        xa = _merge_call(xa, mod, n, outs, wg, bg, wb, wo, i, n_tiles, mod_idx)
        xa = _ffn_call(xa, mod, g[2], w1, w3, w2, i, 1, 6, n_tiles, mod_idx)
    return xa.reshape(batch, seq, d)
```

```python
import functools
import math

import jax
import jax.numpy as jnp
import numpy as np
from jax import lax
from jax.experimental import pallas as pl
from jax.experimental.pallas import tpu as pltpu

F32 = jnp.float32
BF16 = jnp.bfloat16

D_MODEL = 1024
GRID_W = 64
ROPE_THETA = 10000.0
RMS_EPS = 1e-6
HEAD_DIM = 64
N_BRANCH = 4
BRANCH_HEADS = 4
BRANCH_WIDTH = 256
NA_WIN_H = 8
NA_WIN_W = 16
DIFF_QK_DIM = 32
MLA_Q_RANK = 256
MLA_KV_RANK = 128
MLA_NOPE_DIM = 64
MLA_ROPE_DIM = 32
MLA_QK_DIM = 96
D_FF = 2816
N_MOD = 9
LOG2E = 1.4426950408889634

LANES = 128
TOKEN_TILE = 512
Q_TILE = 256
KEY_CHUNK = 4096
FF_SLAB = 256
MERGE_SLAB = 256
PROJ_ROWS = 256
ONES_ROWS = 16
HEAD_ROWS = HEAD_DIM + ONES_ROWS
NA_KEY_ROWS = 12
SCORE_LOOKAHEAD = 2
NEG_BIG = -1e30
SCORE_BOUND = 40.0
VMEM_LIMIT = 56 * 1024 * 1024

C_NA_Q, C_NA_K, C_NA_V = 0, 256, 512
C_GQ_Q, C_GQ_K, C_GQ_V = 768, 1024, 1152
C_DF_Q, C_DF_K, C_DF_V = 1280, 1536, 1792
C_ML_CQ, C_ML_CKV, C_ML_ROPE = 2048, 2304, 2432
IN_COLS_PAD = 2560


def _cparams(sem):
    return pltpu.CompilerParams(dimension_semantics=sem, vmem_limit_bytes=VMEM_LIMIT)


def _dot(a, b):
    return jnp.dot(a, b, preferred_element_type=F32)


def _dot_nt(a, b):
    return lax.dot_general(a, b, (((1,), (1,)), ((), ())), preferred_element_type=F32)


def _silu(a):
    return a * jax.nn.sigmoid(a)


def _modulated_norm(x, g, shift, scale):
    ms = jnp.mean(x * x, axis=-1, keepdims=True)
    return (x * lax.rsqrt(ms + RMS_EPS) * g) * (1.0 + scale) + shift


def _mod_kernel(c_ref, w_ref, b_ref, o_ref):
    c = c_ref[...]
    o_ref[...] = _dot(_silu(c).astype(BF16), w_ref[...].astype(BF16)) + b_ref[...]


def _mod_call(c16, w_ada, b_ada):
    depth, d, nd = w_ada.shape
    tn = 1024
    return pl.pallas_call(
        _mod_kernel,
        grid=(depth, nd // tn),
        in_specs=[
            pl.BlockSpec((16, d), lambda l, j: (0, 0)),
            pl.BlockSpec((None, d, tn), lambda l, j: (l, 0, j)),
            pl.BlockSpec((None, 1, tn), lambda l, j: (l, 0, j)),
        ],
        out_specs=pl.BlockSpec((None, 16, tn), lambda l, j: (l, 0, j)),
        out_shape=jax.ShapeDtypeStruct((depth, 16, nd), F32),
        compiler_params=_cparams(("arbitrary", "arbitrary")),
        name="adaln_mod",
    )(c16, w_ada, b_ada.reshape(depth, 1, nd))


def _ffn_kernel(*refs, row0, head_tiles):
    if head_tiles is None:
        x_ref, mod_ref, g_ref, w1_ref, w3_ref, w2_ref, o_ref, act_scr = refs
        x = x_ref[...]
    else:
        x_ref, tail_ref, mod_ref, g_ref, w1_ref, w3_ref, w2_ref, o_ref, act_scr = refs
        x = jnp.where(pl.program_id(0) < head_tiles, x_ref[...], tail_ref[...])
    h = _modulated_norm(x, g_ref[...], mod_ref[row0:row0 + 1, :], mod_ref[row0 + 1:row0 + 2, :]).astype(BF16)
    for f0 in range(0, D_FF, FF_SLAB):
        a = _dot(h, w1_ref[:, f0:f0 + FF_SLAB])
        b = _dot(h, w3_ref[:, f0:f0 + FF_SLAB])
        act_scr[:, f0:f0 + FF_SLAB] = (_silu(a) * b).astype(BF16)
    o_ref[...] = x + (0.5 * mod_ref[row0 + 2:row0 + 3, :]) * _dot(act_scr[...], w2_ref[...])


def _resident(block_shape, index_map):
    return pl.BlockSpec(block_shape, index_map, pipeline_mode=pl.Buffered(1))


def _ffn_call(x, mod, g, w1, w3, w2, layer, half, row0, n_tiles, mod_idx, x_tail=None):
    d = x.shape[1]
    tm = TOKEN_TILE
    if x_tail is None:
        head_tiles, x_specs, xs = None, [pl.BlockSpec((tm, d), lambda i: (i, 0))], [x]
    else:
        head_tiles = x.shape[0] // tm
        x_specs = [pl.BlockSpec((tm, d), lambda i: (jnp.minimum(i, head_tiles - 1), 0)),
                   pl.BlockSpec((tm, d), lambda i: (jnp.maximum(i - head_tiles, 0), 0))]
        xs = [x, x_tail]
    return pl.pallas_call(
        functools.partial(_ffn_kernel, row0=row0, head_tiles=head_tiles),
        grid=(n_tiles,),
        in_specs=x_specs + [
            pl.BlockSpec((None, None, N_MOD, d), lambda i: (layer, mod_idx(i), 0, 0)),
            _resident((1, d), lambda i: (0, 0)),
            _resident((None, None, d, D_FF), lambda i: (layer, half, 0, 0)),
            _resident((None, None, d, D_FF), lambda i: (layer, half, 0, 0)),
            _resident((None, None, D_FF, d), lambda i: (layer, half, 0, 0)),
        ],
        out_specs=pl.BlockSpec((tm, d), lambda i: (i, 0)),
        out_shape=jax.ShapeDtypeStruct((n_tiles * tm, d), F32),
        scratch_shapes=[pltpu.VMEM((tm, D_FF), BF16)],
        compiler_params=_cparams(("arbitrary",)),
        name="ffn_half_step",
    )(*xs, mod, g, w1, w3, w2)


def _segnorm(x, bd, seg_len, gain):
    x2 = x * x
    hi = x2.astype(BF16)
    lo = (x2 - hi.astype(F32)).astype(BF16)
    groups = []
    for c0 in range(0, x.shape[1], bd.shape[0]):
        w = min(bd.shape[0], x.shape[1] - c0)
        groups.append(_dot(hi[:, c0:c0 + w], bd[0:w, 0:w]) + _dot(lo[:, c0:c0 + w], bd[0:w, 0:w]))
    ss = groups[0] if len(groups) == 1 else jnp.concatenate(groups, axis=-1)
    return x * lax.rsqrt(ss * (1.0 / seg_len) + RMS_EPS) * gain


def _tile_lanes(t, width):
    reps = width // t.shape[-1]
    return t if reps == 1 else jnp.concatenate([t] * reps, axis=-1)


def _rope(x, cos, sin_signed, quarter):
    width = x.shape[-1]
    lane = lax.broadcasted_iota(jnp.int32, x.shape, 1)
    first = (lane % (2 * quarter)) < quarter
    rot = jnp.where(first, pltpu.roll(x, width - quarter, 1), pltpu.roll(x, quarter, 1))
    return x * _tile_lanes(cos, width) + rot * _tile_lanes(sin_signed, width)


def _proj_kernel(x_ref, mod_ref, g_ref, w_ref, vec_ref, bd64_ref, bd32_ref, bd128_ref,
                 wuq_ref, wuk_ref, wuv_ref, c64_ref, s64_ref, c32_ref, s32_ref, cm_ref, sm_ref,
                 n_ref, naq_ref, nak_ref, nav_ref,
                 gqr_ref, gqn_ref, gk_ref, gv_ref,
                 dqr_ref, dqn_ref, dk_ref, dv_ref,
                 mqr_ref, mqn_ref, mk_ref, mv_ref):
    bd64, bd32, bd128 = bd64_ref[...], bd32_ref[...], bd128_ref[...]

    def vec(row, width):
        return vec_ref[row:row + 1, 0:width]

    groups = [slice(r, r + PROJ_ROWS) for r in range(0, x_ref.shape[0], PROJ_ROWS)]
    projected = []
    for rows in groups:
        n = _modulated_norm(x_ref[rows, :], g_ref[...], mod_ref[3:4, :], mod_ref[4:5, :]).astype(BF16)
        n_ref[rows, :] = n
        projected.append(_dot(n, w_ref[...]))

    for rows, p in zip(groups, projected):
        c64, s64 = c64_ref[rows, :], s64_ref[rows, :]
        c32, s32 = c32_ref[rows, :], s32_ref[rows, :]
        cm, sm = cm_ref[rows, :], sm_ref[rows, :]

        naq_ref[rows, :] = _segnorm(p[:, C_NA_Q:C_NA_Q + 256], bd64, HEAD_DIM, vec(0, 256)).astype(BF16)
        nak_ref[rows, :] = _segnorm(p[:, C_NA_K:C_NA_K + 256], bd64, HEAD_DIM, vec(1, 256)).astype(BF16)
        nav_ref[rows, :] = p[:, C_NA_V:C_NA_V + 256].astype(BF16)

        q = _segnorm(p[:, C_GQ_Q:C_GQ_Q + 256], bd64, HEAD_DIM, vec(2, 256))
        gqn_ref[rows, :] = q.astype(BF16)
        gqr_ref[rows, :] = _rope(q, c64, s64, HEAD_DIM // 4).astype(BF16)
        k = _segnorm(p[:, C_GQ_K:C_GQ_K + 128], bd64, HEAD_DIM, vec(3, 128))
        gk_ref[rows, :] = _rope(k, c64, s64, HEAD_DIM // 4).astype(BF16)
        gv_ref[rows, :] = p[:, C_GQ_V:C_GQ_V + 128].astype(BF16)

        q = _segnorm(p[:, C_DF_Q:C_DF_Q + 256], bd32, DIFF_QK_DIM, vec(4, 256))
        dqn_ref[rows, :] = q.astype(BF16)
        dqr_ref[rows, :] = _rope(q, c32, s32, DIFF_QK_DIM // 4).astype(BF16)
        k = _segnorm(p[:, C_DF_K:C_DF_K + 256], bd32, DIFF_QK_DIM, vec(5, 256))
        dk_ref[rows, :] = _rope(k, c32, s32, DIFF_QK_DIM // 4).astype(BF16)
        dv_ref[rows, :] = p[:, C_DF_V:C_DF_V + 256].astype(BF16)

        cq = p[:, C_ML_CQ:C_ML_CQ + MLA_Q_RANK]
        cq = cq * lax.rsqrt(jnp.mean(cq * cq, axis=-1, keepdims=True) + RMS_EPS) * vec(6, 256)
        q = _segnorm(_dot(cq.astype(BF16), wuq_ref[...]), bd128, MLA_QK_DIM, vec(8, 512))
        mqn_ref[rows, :] = q.astype(BF16)
        mqr_ref[rows, :] = _rope(q, cm, sm, MLA_ROPE_DIM // 4).astype(BF16)
        ckv = p[:, C_ML_CKV:C_ML_CKV + MLA_KV_RANK]
        ckv = (ckv * lax.rsqrt(jnp.mean(ckv * ckv, axis=-1, keepdims=True) + RMS_EPS) * vec(7, 128)).astype(BF16)
        k_rope = pltpu.roll(p[:, C_ML_ROPE:C_ML_ROPE + 128], MLA_NOPE_DIM, 1)
        k = _dot(ckv, wuk_ref[...]) + _tile_lanes(k_rope, 512)
        k = _segnorm(k, bd128, MLA_QK_DIM, vec(9, 512))
        mk_ref[rows, :] = _rope(k, cm, sm, MLA_ROPE_DIM // 4).astype(BF16)
        mv_ref[rows, :] = _dot(ckv, wuv_ref[...]).astype(BF16)


_PROJ_OUT_WIDTHS = (1024, 256, 256, 256, 256, 256, 128, 128, 256, 256, 256, 256, 512, 512, 512, 256)


def _proj_call(x, mod, g, w_in, vecs, bds, wuq, wuk, wuv, tables, layer, n_tiles, mod_idx, tab_idx):
    d = x.shape[1]
    tm = TOKEN_TILE
    const2 = lambda i: (0, 0)
    in_specs = [
        pl.BlockSpec((tm, d), lambda i: (i, 0)),
        pl.BlockSpec((None, None, N_MOD, d), lambda i: (layer, mod_idx(i), 0, 0)),
        pl.BlockSpec((1, d), const2),
        pl.BlockSpec((None, d, IN_COLS_PAD), lambda i: (layer, 0, 0)),
        pl.BlockSpec((None, 16, 512), lambda i: (layer, 0, 0)),
        pl.BlockSpec((256, 256), const2),
        pl.BlockSpec((256, 256), const2),
        pl.BlockSpec((256, 256), const2),
        pl.BlockSpec((None, 256, 512), lambda i: (layer, 0, 0)),
        pl.BlockSpec((None, 128, 512), lambda i: (layer, 0, 0)),
        pl.BlockSpec((None, 128, 256), lambda i: (layer, 0, 0)),
    ] + [pl.BlockSpec((tm, LANES), lambda i: (tab_idx(i), 0))] * 6
    rows = n_tiles * tm
    return pl.pallas_call(
        _proj_kernel,
        grid=(n_tiles,),
        in_specs=in_specs,
        out_specs=[pl.BlockSpec((tm, w), lambda i: (i, 0)) for w in _PROJ_OUT_WIDTHS],
        out_shape=[jax.ShapeDtypeStruct((rows, w), BF16) for w in _PROJ_OUT_WIDTHS],
        compiler_params=_cparams(("arbitrary",)),
        name="mixer_projection",
    )(x, mod, g, w_in, vecs, *bds, wuq, wuk, wuv, *tables)


def _attend(softmaxes, bounded, lookahead):
    def scores_of(i):
        out = []
        for k_ref, rows, ks, _, _, q_t, bias in softmaxes[i]:
            s = _dot(k_ref[rows, ks * LANES:(ks + 1) * LANES], q_t)
            out.append(s if bias is None else s + bias())
        return out

    n = len(softmaxes)
    scores = {i: scores_of(i) for i in range(min(lookahead, n))}
    outs = []
    for i in range(n):
        if i + lookahead < n:
            scores[i + lookahead] = scores_of(i + lookahead)
        mine = scores.pop(i)
        if not bounded:
            m = functools.reduce(jnp.maximum, [jnp.max(s, axis=0, keepdims=True) for s in mine])
        acc = None
        for s, (_, _, _, v_t, vh, _, _) in zip(mine, softmaxes[i]):
            p = jnp.exp2(s if bounded else s - m)
            acc_blk = _dot(_head_rows(v_t, vh), p.astype(BF16))
            acc = acc_blk if acc is None else acc + acc_blk
        outs.append(acc[0:HEAD_DIM, :] / acc[HEAD_DIM:HEAD_DIM + 1, :])
    return outs


def _score_bound(g_q, g_k, dim):
    return 1.01 * dim * (dim ** -0.5 * LOG2E) * jnp.max(jnp.abs(g_q)) * jnp.max(jnp.abs(g_k))


def _transpose_bf16(eye, a):
    return _dot_nt(eye[0:a.shape[1], 0:a.shape[1]], a).astype(BF16)


def _q_slabs_t(eye, q_ref):
    out = []
    for j0 in range(0, q_ref.shape[1], 2 * LANES):
        w = min(2 * LANES, q_ref.shape[1] - j0)
        q_t = _dot_nt(eye[0:w, 0:w], q_ref[:, j0:j0 + w]).astype(BF16)
        out += [q_t[r:r + LANES, :] for r in range(0, w, LANES)]
    return out


def _keep_rows(q_t, lo, hi):
    if lo == 0 and hi == q_t.shape[0]:
        return q_t
    row = lax.broadcasted_iota(jnp.int32, q_t.shape, 0)
    return jnp.where((row >= lo) & (row < hi), q_t, jnp.zeros_like(q_t))


def _head_rows(v_t, head):
    return v_t[head * HEAD_ROWS:(head + 1) * HEAD_ROWS, :]


def _staged_rows(v_width):
    return v_width // HEAD_DIM * HEAD_ROWS


def _store_heads(eye, o_ref, o_t_heads):
    for p in range(0, len(o_t_heads), 4):
        o_t = jnp.concatenate(o_t_heads[p:p + 4], axis=0).astype(BF16)
        o_ref[:, p * HEAD_DIM:p * HEAD_DIM + o_t.shape[0]] = _dot_nt(eye, o_t).astype(o_ref.dtype)


def _stage_values_t(eye, v_ref, vt_scr, n_chunks, chunk):
    ones = jnp.ones((ONES_ROWS, chunk), BF16)
    for c in range(n_chunks):
        for j in range(v_ref.shape[1] // LANES):
            pair_t = _transpose_bf16(eye, v_ref[c * chunk:(c + 1) * chunk, j * LANES:(j + 1) * LANES])
            for half in range(2):
                r0 = (2 * j + half) * HEAD_ROWS
                vt_scr[c, r0:r0 + HEAD_DIM, :] = pair_t[half * HEAD_DIM:(half + 1) * HEAD_DIM, :]
                vt_scr[c, r0 + HEAD_DIM:r0 + HEAD_ROWS, :] = ones


def _joint_kernel(*refs, groups, n_lat_tiles, seq, chunk, with_ctx_queries, diff_lam_init, bounded, lookahead):
    n_groups = len(groups)
    ins = [refs[6 * g:6 * g + 6] for g in range(n_groups)]
    pos = 6 * n_groups
    eye_ref = refs[pos]
    pos += 1
    if diff_lam_init is not None:
        lam_ref, gsub_ref = refs[pos:pos + 2]
        pos += 2
    o_refs = refs[pos:pos + n_groups]
    scratch = [refs[pos + n_groups + 2 * g:pos + n_groups + 2 * g + 2] for g in range(n_groups)]
    t = pl.program_id(1)
    n_chunks = seq // chunk
    eye = eye_ref[...]

    @pl.when(t == 0)
    def _():
        for (_, _, _, _, vl_ref, vc_ref), (vt_scr, vct_scr) in zip(ins, scratch):
            _stage_values_t(eye, vl_ref, vt_scr, n_chunks, chunk)
            _stage_values_t(eye, vc_ref, vct_scr, 1, vc_ref.shape[0])

    def operands(q_ref, subs):
        slabs = _q_slabs_t(eye, q_ref)
        return [_keep_rows(slabs[qs], lo, hi) for (qs, lo, hi, _, _) in subs]

    def softmaxes(latent):
        out = []
        for subs, (qr_ref, qn_ref, kl_ref, kc_ref, _, _), (vt_scr, vct_scr) in zip(groups, ins, scratch):
            qn_t = operands(qn_ref, subs)
            qr_t = operands(qr_ref, subs) if latent else None
            for i, (_, _, _, ks, vh) in enumerate(subs):
                blocks = []
                if latent:
                    blocks += [(kl_ref, slice(c * chunk, (c + 1) * chunk), ks, vt_scr[c], vh, qr_t[i], None)
                               for c in range(n_chunks)]
                blocks.append((kc_ref, slice(None), ks, vct_scr[0], vh, qn_t[i], None))
                out.append(blocks)
        return out

    def finish(outs):
        for subs, o_ref in zip(groups, o_refs):
            heads, outs = outs[:len(subs)], outs[len(subs):]
            if diff_lam_init is not None:
                lv = lam_ref[...]
                lam = (jnp.exp(jnp.sum(lv[0:1, :] * lv[1:2, :], axis=-1, keepdims=True))
                       - jnp.exp(jnp.sum(lv[2:3, :] * lv[3:4, :], axis=-1, keepdims=True)) + diff_lam_init)
                pairs, heads = zip(heads[0::2], heads[1::2]), []
                for o1, o2 in pairs:
                    d = o1 - lam * o2
                    ms = jnp.mean(d * d, axis=0, keepdims=True)
                    heads.append((d * lax.rsqrt(ms + RMS_EPS) * gsub_ref[...]) * (1.0 - diff_lam_init))
            _store_heads(eye, o_ref, heads)

    @pl.when(t < n_lat_tiles)
    def _():
        finish(_attend(softmaxes(True), bounded, lookahead))

    if with_ctx_queries:
        @pl.when(t >= n_lat_tiles)
        def _():
            finish(_attend(softmaxes(False), bounded, lookahead))


def _joint_call(operands, out_rows, *, groups, batch, seq, ctx_len, with_ctx_queries, name, bounded,
                diff_lam_init=None, lookahead=SCORE_LOOKAHEAD):
    n_groups = len(groups)
    tq = Q_TILE
    n_lat = seq // tq
    nqt = n_lat + (1 if with_ctx_queries else 0)
    ctx_row0 = batch * seq // ctx_len
    chunk = min(KEY_CHUNK, seq)
    n_chunks = seq // chunk

    def qrow(b, t):
        return jnp.where(t < n_lat, b * n_lat + t, batch * n_lat + b)

    in_specs, args, scratch = [], [], []
    for g in range(n_groups):
        qr, qn, k, v = operands[4 * g:4 * g + 4]
        in_specs += [
            pl.BlockSpec((tq, qr.shape[1]), lambda b, t: (qrow(b, t), 0)),
            pl.BlockSpec((tq, qn.shape[1]), lambda b, t: (qrow(b, t), 0)),
            pl.BlockSpec((seq, k.shape[1]), lambda b, t: (b, 0)),
            pl.BlockSpec((ctx_len, k.shape[1]), lambda b, t: (ctx_row0 + b, 0)),
            pl.BlockSpec((seq, v.shape[1]), lambda b, t: (b, 0)),
            pl.BlockSpec((ctx_len, v.shape[1]), lambda b, t: (ctx_row0 + b, 0)),
        ]
        args += [qr, qn, k, k, v, v]
        rows = _staged_rows(v.shape[1])
        scratch += [pltpu.VMEM((n_chunks, rows, chunk), BF16), pltpu.VMEM((1, rows, ctx_len), BF16)]
    in_specs.append(pl.BlockSpec((tq, tq), lambda b, t: (0, 0)))
    args.append(jnp.eye(tq, dtype=BF16))
    if diff_lam_init is not None:
        in_specs += [pl.BlockSpec((4, DIFF_QK_DIM), lambda b, t: (0, 0)),
                     pl.BlockSpec((HEAD_DIM, tq), lambda b, t: (0, 0))]
        args += list(operands[4 * n_groups:])
    return pl.pallas_call(
        functools.partial(_joint_kernel, groups=groups, n_lat_tiles=n_lat, seq=seq, chunk=chunk,
                          with_ctx_queries=with_ctx_queries, diff_lam_init=diff_lam_init, bounded=bounded,
                          lookahead=lookahead),
        grid=(batch, nqt),
        in_specs=in_specs,
        out_specs=[pl.BlockSpec((tq, BRANCH_WIDTH), lambda b, t: (qrow(b, t), 0))] * n_groups,
        out_shape=[jax.ShapeDtypeStruct((out_rows, BRANCH_WIDTH), BF16)] * n_groups,
        scratch_shapes=scratch,
        compiler_params=_cparams(("arbitrary", "arbitrary")),
        name=name,
    )(*args)


def _na_kernel(q_ref, kl_ref, kc_ref, vl_ref, vc_ref, ab_ref, eye_ref, o_ref, vt_scr, vct_scr, *, n_lat_tiles,
               bounded):
    t = pl.program_id(2)
    tq = o_ref.shape[0]
    eye = eye_ref[...]
    rows_per_tile = tq // GRID_W
    band_tiles = NA_KEY_ROWS // rows_per_tile
    heads = range(BRANCH_HEADS)

    @pl.when(t == 0)
    def _():
        _stage_values_t(eye, vl_ref, vt_scr, n_lat_tiles, tq)
        _stage_values_t(eye, vc_ref, vct_scr, 1, vc_ref.shape[0])

    subs = tuple((h // 2, (h % 2) * HEAD_DIM, (h % 2 + 1) * HEAD_DIM, h // 2, h) for h in heads)

    def attend(blocks):
        slabs = _q_slabs_t(eye, q_ref)
        softmaxes = []
        for h, (qs, lo, hi, ks, vh) in enumerate(subs):
            q_t = _keep_rows(slabs[qs], lo, hi)
            softmaxes.append([(k_ref, rows, ks, v_t, vh, q_t, None if bias is None else (lambda h=h, bias=bias: bias[h]))
                              for k_ref, rows, v_t, bias in blocks])
        _store_heads(eye, o_ref, _attend(softmaxes, bounded, SCORE_LOOKAHEAD))

    ctx_block = (kc_ref, slice(None), vct_scr[0], None)

    @pl.when(t < n_lat_tiles)
    def _():
        tile0 = jnp.clip(t - NA_WIN_H // 2 // rows_per_tile, 0, n_lat_tiles - band_tiles)
        rows = pl.ds(pl.multiple_of(tile0 * tq, tq), band_tiles * tq)
        v_t = jnp.concatenate([vt_scr[tile0 + j] for j in range(band_tiles)], axis=1)
        attend([(kl_ref, rows, v_t, ab_ref), ctx_block])

    @pl.when(t >= n_lat_tiles)
    def _():
        attend([ctx_block])


def _na_call(operands, out_rows, *, batch, seq, ctx_len, with_ctx_queries, bounded):
    q, k, v, ab = operands
    tq = Q_TILE
    n_lat = seq // tq
    nqt = n_lat + (1 if with_ctx_queries else 0)
    ctx_row0 = batch * seq // ctx_len
    n_keys = NA_KEY_ROWS * GRID_W
    width = BRANCH_WIDTH

    def qrow(b, t):
        return jnp.where(t < n_lat, b * n_lat + t, batch * n_lat + b)

    def pattern(t):
        return jnp.where(t == 0, 0, jnp.where(t >= n_lat - 1, 2, 1))

    return pl.pallas_call(
        functools.partial(_na_kernel, n_lat_tiles=n_lat, bounded=bounded),
        grid=(batch, 1, nqt),
        in_specs=[
            pl.BlockSpec((tq, width), lambda b, blk, t: (qrow(b, t), 0)),
            pl.BlockSpec((seq, width), lambda b, blk, t: (b, 0)),
            pl.BlockSpec((ctx_len, width), lambda b, blk, t: (ctx_row0 + b, 0)),
            pl.BlockSpec((seq, width), lambda b, blk, t: (b, 0)),
            pl.BlockSpec((ctx_len, width), lambda b, blk, t: (ctx_row0 + b, 0)),
            pl.BlockSpec((None, BRANCH_HEADS, n_keys, tq), lambda b, blk, t: (pattern(t), 0, 0, 0)),
            pl.BlockSpec((tq, tq), lambda b, blk, t: (0, 0)),
        ],
        out_specs=pl.BlockSpec((tq, width), lambda b, blk, t: (qrow(b, t), 0)),
        out_shape=jax.ShapeDtypeStruct((out_rows, width), BF16),
        scratch_shapes=[pltpu.VMEM((n_lat, _staged_rows(width), tq), BF16),
                        pltpu.VMEM((1, _staged_rows(width), ctx_len), BF16)],
        compiler_params=_cparams(("arbitrary", "arbitrary", "arbitrary")),
        name="neighbourhood_attention",
    )(q, k, k, v, v, ab, jnp.eye(tq, dtype=BF16))


def _merge_kernel(x_ref, mod_ref, n_ref, o0_ref, o1_ref, o2_ref, o3_ref, wg_ref, bg_ref, wb_ref, wo_ref,
                  out_ref, y_scr):
    n = n_ref[...]
    branch = [o_ref[...] for o_ref in (o0_ref, o1_ref, o2_ref, o3_ref)]
    for c0 in range(0, n.shape[1], MERGE_SLAB):
        cols = slice(c0, c0 + MERGE_SLAB)
        y = None
        for j in range(N_BRANCH):
            gate = jax.nn.sigmoid(_dot(n, wg_ref[j, :, cols]) + bg_ref[j, :, cols])
            term = gate * _dot(branch[j], wb_ref[j, :, cols])
            y = term if y is None else y + term
        y_scr[:, cols] = y.astype(BF16)
    out_ref[...] = x_ref[...] + mod_ref[5:6, :] * _dot(y_scr[...], wo_ref[...])


def _merge_call(x, mod, n, outs, w_gate, b_gate, w_branch, w_out, layer, n_tiles, mod_idx):
    d = x.shape[1]
    tm = TOKEN_TILE
    row = lambda i: (i, 0)
    return pl.pallas_call(
        _merge_kernel,
        grid=(n_tiles,),
        in_specs=[
            pl.BlockSpec((tm, d), row),
            pl.BlockSpec((None, None, N_MOD, d), lambda i: (layer, mod_idx(i), 0, 0)),
            pl.BlockSpec((tm, d), row),
        ] + [pl.BlockSpec((tm, BRANCH_WIDTH), row)] * 4 + [
            _resident((None, N_BRANCH, d, d), lambda i: (layer, 0, 0, 0)),
            _resident((None, N_BRANCH, 1, d), lambda i: (layer, 0, 0, 0)),
            _resident((None, N_BRANCH, BRANCH_WIDTH, d), lambda i: (layer, 0, 0, 0)),
            _resident((None, d, d), lambda i: (layer, 0, 0)),
        ],
        out_specs=pl.BlockSpec((tm, d), row),
        out_shape=jax.ShapeDtypeStruct((n_tiles * tm, d), F32),
        scratch_shapes=[pltpu.VMEM((tm, d), BF16)],
        compiler_params=_cparams(("arbitrary",)),
        name="branch_merge",
    )(x, mod, n, *outs, w_gate, b_gate, w_branch, w_out)


def _rope_tables(seq, rot_dim, pad_rows):
    t = jnp.arange(seq, dtype=jnp.int32)
    row = (t // GRID_W).astype(F32)
    col = (t % GRID_W).astype(F32)
    n_freq = rot_dim // 4
    inv = ROPE_THETA ** (-jnp.arange(n_freq, dtype=F32) / n_freq)
    ar = row[:, None] * inv[None, :]
    ac = col[:, None] * inv[None, :]
    ang = jnp.concatenate([ar, ar, ac, ac], axis=-1)
    sign = jnp.tile(jnp.concatenate([-jnp.ones((n_freq,), F32), jnp.ones((n_freq,), F32)]), 2)
    cos = jnp.concatenate([jnp.cos(ang), jnp.ones((pad_rows, rot_dim), F32)], axis=0)
    sin = jnp.concatenate([jnp.sin(ang) * sign[None, :], jnp.zeros((pad_rows, rot_dim), F32)], axis=0)
    return cos, sin


def _all_rope_tables(seq, pad_rows):
    c64, s64 = _rope_tables(seq, HEAD_DIM, pad_rows)
    c32, s32 = _rope_tables(seq, DIFF_QK_DIM, pad_rows)
    rows = seq + pad_rows
    ones = lambda w: jnp.ones((rows, w), F32)
    zeros = lambda w: jnp.zeros((rows, w), F32)
    pad = LANES - MLA_QK_DIM
    cm = jnp.concatenate([ones(MLA_NOPE_DIM), c32, ones(pad)], axis=-1)
    sm = jnp.concatenate([zeros(MLA_NOPE_DIM), s32, zeros(pad)], axis=-1)
    return (jnp.tile(c64, (1, 2)), jnp.tile(s64, (1, 2)), jnp.tile(c32, (1, 4)), jnp.tile(s32, (1, 4)), cm, sm)


def _block_diag_ones(width, seg):
    idx = np.arange(width) // seg
    return jnp.asarray(idx[:, None] == idx[None, :], dtype=BF16)


def _na_bias_tables(rpb, grid_rows):
    rows_per_tile = Q_TILE // GRID_W
    wh = min(NA_WIN_H, grid_rows)
    kinds = ((0, 0), (rows_per_tile, 0), (grid_rows - rows_per_tile, grid_rows - NA_KEY_ROWS))
    col = np.arange(GRID_W)
    c0 = np.clip(col - NA_WIN_W // 2, 0, GRID_W - NA_WIN_W)
    col_ok = (col[:, None] >= c0[None, :]) & (col[:, None] < c0[None, :] + NA_WIN_W)
    d_col = np.clip(col[:, None] - col[None, :], -(NA_WIN_W - 1), NA_WIN_W - 1) + (NA_WIN_W - 1)
    col_hot = (d_col[..., None] == np.arange(2 * NA_WIN_W - 1)).astype(np.float32)
    blocks = jnp.einsum("hab,dcb->hadc", rpb.astype(F32), jnp.asarray(col_hot), precision=lax.Precision.HIGHEST)
    blocks = jnp.where(jnp.asarray(col_ok)[None, None], blocks * LOG2E, NEG_BIG)
    masked = jnp.full(blocks.shape[:1] + (GRID_W, GRID_W), NEG_BIG, F32)
    tables = []
    for q_row0, k_row0 in kinds:
        r = q_row0 + np.arange(rows_per_tile)
        kr = k_row0 + np.arange(NA_KEY_ROWS)
        r0 = np.clip(r - wh // 2, 0, grid_rows - wh)
        key_rows = []
        for kk in range(NA_KEY_ROWS):
            in_window = (kr[kk] >= r0) & (kr[kk] < r0 + wh)
            key_rows.append(jnp.concatenate(
                [blocks[:, kr[kk] - r[qr] + (NA_WIN_H - 1)] if in_window[qr] else masked
                 for qr in range(rows_per_tile)], axis=-1))
        tables.append(jnp.concatenate(key_rows, axis=-2))
    return jnp.stack(tables, axis=0)


def _pad_heads(w, head_w, to_w):
    lead = w.shape[:-1]
    w = w.reshape(lead + (BRANCH_HEADS, head_w))
    w = jnp.pad(w, [(0, 0)] * len(lead) + [(0, 0), (0, to_w - head_w)])
    return w.reshape(lead + (BRANCH_HEADS * to_w,))


def _gqa_head_order(a, axis):
    parts = jnp.split(a, BRANCH_HEADS, axis=axis)
    return jnp.concatenate([parts[0], parts[2], parts[1], parts[3]], axis=axis)


def kernel(x, c, ctx, c_ctx, w_ada, b_ada, norm_g, ffn_w1, ffn_w3, ffn_w2, w_in, na_g_q, na_g_k, na_rpb, gqa_g_q, gqa_g_k, diff_g_q, diff_g_k, diff_lambda, diff_g_sub, mla_g_cq, mla_g_ckv, mla_w_uq, mla_w_ukv, mla_g_q, mla_g_k, w_gate, b_gate, w_branch, w_out):
    batch, seq, d = x.shape
    ctx_len = ctx.shape[1]
    depth = w_ada.shape[0]
    tm = TOKEN_TILE
    assert d == D_MODEL and ctx_len == Q_TILE and seq % tm == 0 and (batch * ctx_len) % tm == 0
    assert seq % min(KEY_CHUNK, seq) == 0 and seq // GRID_W >= NA_KEY_ROWS and batch < 16
    n_lat_rows = batch * seq
    lat_tiles = n_lat_rows // tm
    all_tiles = lat_tiles + batch * ctx_len // tm
    tiles_per_batch = seq // tm

    def mod_idx(i):
        return jnp.where(i < lat_tiles, i // tiles_per_batch, batch)

    def tab_idx(i):
        return jnp.where(i < lat_tiles, i % tiles_per_batch, tiles_per_batch)

    c16 = jnp.concatenate([c, c_ctx[None, :], jnp.zeros((16 - batch - 1, d), F32)], axis=0)
    mods = _mod_call(c16, w_ada, b_ada).reshape(depth, 16, N_MOD, d)
    w1, w3, w2 = ffn_w1.astype(BF16), ffn_w3.astype(BF16), ffn_w2.astype(BF16)
    na_w, gq_w, df_w, ml_w = jnp.split(w_in, [768, 1280, 2048], axis=-1)
    w_in_p = jnp.concatenate([
        na_w, _gqa_head_order(gq_w[..., :256], -1), gq_w[..., 256:], df_w, ml_w,
        jnp.zeros((depth, d, IN_COLS_PAD - w_in.shape[-1]), F32)], axis=-1).astype(BF16)
    wuq = _pad_heads(mla_w_uq, MLA_QK_DIM, LANES).astype(BF16)
    wukv = mla_w_ukv.reshape(depth, MLA_KV_RANK, BRANCH_HEADS, MLA_NOPE_DIM + HEAD_DIM)
    wuk = _pad_heads(wukv[..., :MLA_NOPE_DIM].reshape(depth, MLA_KV_RANK, -1), MLA_NOPE_DIM, LANES).astype(BF16)
    wuv = wukv[..., MLA_NOPE_DIM:].reshape(depth, MLA_KV_RANK, BRANCH_WIDTH).astype(BF16)

    def vec_row(v, reps=1, scale=1.0):
        v = jnp.tile(v, (1, reps)) * scale
        return jnp.pad(v, ((0, 0), (0, 512 - v.shape[-1])))

    pad_g = lambda g: jnp.pad(g, ((0, 0), (0, LANES - MLA_QK_DIM)))
    vec_rows = [
        vec_row(na_g_q, 4, HEAD_DIM ** -0.5 * LOG2E), vec_row(na_g_k, 4),
        vec_row(gqa_g_q, 4, HEAD_DIM ** -0.5 * LOG2E), vec_row(gqa_g_k, 2),
        vec_row(diff_g_q, 8, DIFF_QK_DIM ** -0.5 * LOG2E), vec_row(diff_g_k, 8),
        vec_row(mla_g_cq), vec_row(mla_g_ckv),
        vec_row(pad_g(mla_g_q), 4, MLA_QK_DIM ** -0.5 * LOG2E), vec_row(pad_g(mla_g_k), 4),
    ]
    vecs = jnp.stack(vec_rows + [jnp.zeros_like(vec_rows[0])] * (16 - len(vec_rows)), axis=1)
    bds = (_block_diag_ones(256, HEAD_DIM), _block_diag_ones(256, DIFF_QK_DIM), _block_diag_ones(256, LANES))
    tables = _all_rope_tables(seq, tm)
    na_ab = [_na_bias_tables(na_rpb[i], seq // GRID_W) for i in range(depth)]
    gsub = jnp.broadcast_to(diff_g_sub[:, :, None], (depth, HEAD_DIM, Q_TILE))
    wg, wo = w_gate.astype(BF16), w_out.astype(BF16)
    wb = jnp.concatenate([w_branch[:, 0:1], _gqa_head_order(w_branch[:, 1:2], 2), w_branch[:, 2:]], axis=1).astype(BF16)
    bg = b_gate.reshape(depth, N_BRANCH, 1, d)

    xa, x_tail = x.reshape(n_lat_rows, d), ctx.reshape(batch * ctx_len, d)

    gqa_subs = ((0, 0, 64, 0, 0), (0, 64, 128, 0, 1), (1, 0, 64, 0, 0), (1, 64, 128, 0, 1))
    diff_subs = tuple((h // 2, 32 * (2 * (h % 2) + j), 32 * (2 * (h % 2) + j + 1), h // 2, h)
                      for h in range(BRANCH_HEADS) for j in range(2))
    mla_subs = tuple((h, 0, LANES, h, h) for h in range(BRANCH_HEADS))

    for i in range(depth):
        need_ctx = i < depth - 1
        mod = mods
        g = norm_g[i].reshape(3, 1, d)
        xa = _ffn_call(xa, mod, g[0], w1, w3, w2, i, 0, 0, all_tiles, mod_idx, x_tail=x_tail)
        x_tail = None
        (n, naq, nak, nav, gqr, gqn, gk, gv, dqr, dqn, dk, dv, mqr, mqn, mk, mv) = _proj_call(
            xa, mod, g[1], w_in_p, vecs, bds, wuq, wuk, wuv, tables, i, all_tiles, mod_idx, tab_idx)
        out_rows = xa.shape[0] if need_ctx else n_lat_rows
        common = dict(batch=batch, seq=seq, ctx_len=ctx_len, with_ctx_queries=need_ctx)
        lam_init = 0.8 - 0.6 * math.exp(-0.3 * i)
        def dispatch(call, bound, *operands, **kw):
            return lax.cond(bound <= SCORE_BOUND,
                            lambda *a: call(a, out_rows, bounded=True, **kw, **common),
                            lambda *a: call(a, out_rows, bounded=False, **kw, **common), *operands)

        joint = functools.partial(dispatch, _joint_call)
        na_bound = _score_bound(na_g_q[i], na_g_k[i], HEAD_DIM) + LOG2E * jnp.max(jnp.abs(na_rpb[i]))

        gqa_bound = _score_bound(gqa_g_q[i], gqa_g_k[i], HEAD_DIM)
        mla_bound = _score_bound(mla_g_q[i], mla_g_k[i], MLA_QK_DIM)
        o_na = dispatch(_na_call, na_bound, naq, nak, nav, na_ab[i])
        o_gqa, o_mla = joint(jnp.maximum(gqa_bound, mla_bound), gqr, gqn, gk, gv, mqr, mqn, mk, mv,
                             groups=(gqa_subs, mla_subs), name="gqa_mla_attention")
        o_diff, = joint(_score_bound(diff_g_q[i], diff_g_k[i], DIFF_QK_DIM), dqr, dqn, dk, dv, diff_lambda[i], gsub[i],
                        groups=(diff_subs,), name="diff_attention", diff_lam_init=lam_init)
        outs = [o_na, o_gqa, o_diff, o_mla]
        n_tiles = all_tiles if need_ctx else lat_tiles
        xa = _merge_call(xa, mod, n, outs, wg, bg, wb, wo, i, n_tiles, mod_idx)
        xa = _ffn_call(xa, mod, g[2], w1, w3, w2, i, 1, 6, n_tiles, mod_idx)
    return xa.reshape(batch, seq, d)
```
